```python
import math
import jax, jax.numpy as jnp
from jax import lax
import numpy as np

D_MODEL = 2048
BATCH = 4
SEQ = 2048
DEPTH = 4
DEC_BATCH = 8
DEC_SEQ = 8
PAST_LEN = 16384
PAGE_SIZE = 128

N_HYB = (DEPTH + 1) // 2
N_CONV = DEPTH // 2
MOBA_W = D_MODEL // 2
N_HEADS = 8
HEAD_DIM = MOBA_W // N_HEADS
MOBA_BLOCK = 256
MOBA_TOPK = 3
Q_CHUNK = 16
S5_W = D_MODEL - MOBA_W
S5_GROUP = 16
S5_GROUPS = S5_W // S5_GROUP
S5_STATE = 64
CONV_W = 3
D_FF = 5632
ALPHA = (2.0 * DEPTH) ** 0.25
BETA = (8.0 * DEPTH) ** -0.25
LN_EPS = 1e-5
DT_MIN = 1e-3
DT_MAX = 1e-1

kernel_name = 'hybrid_moba_s5_shortconv_decode_step'


def layer_norm(x, g, b):
    xf = x.astype(jnp.float32)
    mu = jnp.mean(xf, axis=-1, keepdims=True)
    var = jnp.mean(jnp.square(xf - mu), axis=-1, keepdims=True)
    y = (xf - mu) * lax.rsqrt(var + LN_EPS) * g.astype(jnp.float32) + b.astype(jnp.float32)
    return y.astype(x.dtype)


def swiglu(x, w_gu, w_d):
    g, u = jnp.split(x @ w_gu, 2, axis=-1)
    return (jax.nn.silu(g) * u) @ w_d


def moba_attention(q, k, v, q_pos0):
    bsz, tq, nh, hd = q.shape
    tk = k.shape[1]
    nb = -(-tk // MOBA_BLOCK)
    pad = nb * MOBA_BLOCK - tk
    kb = jnp.pad(k, ((0, 0), (0, pad), (0, 0), (0, 0))).reshape(bsz, nb, MOBA_BLOCK, nh, hd)
    vb = jnp.pad(v, ((0, 0), (0, pad), (0, 0), (0, 0))).reshape(bsz, nb, MOBA_BLOCK, nh, hd)
    k_mean = jnp.mean(kb.astype(jnp.float32), axis=2)
    n_sel = min(MOBA_TOPK, nb)
    qc = min(Q_CHUNK, tq)
    n_chunks = -(-tq // qc)
    q_pad = jnp.pad(q, ((0, 0), (0, n_chunks * qc - tq), (0, 0), (0, 0)))
    q_chunks = q_pad.reshape(bsz, n_chunks, qc, nh, hd).transpose(1, 0, 2, 3, 4)
    pos_chunks = (q_pos0 + jnp.arange(n_chunks * qc, dtype=jnp.int32)).reshape(n_chunks, qc)
    b_idx = jnp.arange(bsz)[:, None, None, None]
    h_idx = jnp.arange(nh)[None, :, None, None]
    offs = jnp.arange(MOBA_BLOCK, dtype=jnp.int32)
    blk_ids = jnp.arange(nb, dtype=jnp.int32)
    scale = hd ** -0.5

    def one_chunk(args):
        qb, pos = args
        own = pos // MOBA_BLOCK
        gate = jnp.einsum('bqhd,bnhd->bhqn', qb.astype(jnp.float32), k_mean)
        fully_past = blk_ids[None, :] < own[:, None]
        gate = jnp.where(fully_past[None, None], gate, -jnp.inf)
        _, top_idx = lax.top_k(gate, n_sel)
        top_ok = jnp.arange(n_sel)[None, :] < own[:, None]
        own_idx = jnp.broadcast_to(jnp.minimum(own, nb - 1)[None, None, :, None], (bsz, nh, qc, 1))
        sel = jnp.concatenate([top_idx.astype(jnp.int32), own_idx], axis=-1)
        ok = jnp.concatenate([jnp.broadcast_to(top_ok[None, None], (bsz, nh, qc, n_sel)),
                              jnp.ones((bsz, nh, qc, 1), dtype=bool)], axis=-1)
        kg = kb[b_idx, sel, :, h_idx, :]
        vg = vb[b_idx, sel, :, h_idx, :]
        key_pos = sel[..., None] * MOBA_BLOCK + offs
        mask = ok[..., None] & (key_pos <= pos[None, None, :, None, None])
        s = jnp.einsum('bqhd,bhqskd->bhqsk', qb, kg).astype(jnp.float32) * scale
        s = jnp.where(mask, s, -jnp.inf).reshape(bsz, nh, qc, -1)
        prob = jax.nn.softmax(s, axis=-1).astype(v.dtype)
        return jnp.einsum('bhqk,bhqkd->bqhd', prob, vg.reshape(bsz, nh, qc, -1, hd))

    out = lax.map(one_chunk, (q_chunks, pos_chunks))
    return out.transpose(1, 0, 2, 3, 4).reshape(bsz, n_chunks * qc, nh, hd)[:, :tq]


def s5_mixer(u, h0_re, h0_im, a_re, a_im, log_dt, b_re, b_im, c_re, c_im, d_skip, glu_w, glu_b):
    bsz, t, _ = u.shape
    f32 = jnp.float32
    ug = u.astype(f32).reshape(bsz, t, S5_GROUPS, S5_GROUP)
    lam = lax.complex(a_re.astype(f32), a_im.astype(f32))
    dt = jnp.exp(log_dt.astype(f32))[:, None]
    lam_bar = jnp.exp(lam * dt)
    b_bar = ((lam_bar - 1.0) / lam)[..., None] * lax.complex(b_re.astype(f32), b_im.astype(f32))
    bu = jnp.einsum('gpc,btgc->btgp', b_bar, ug.astype(jnp.complex64))
    h0 = lax.complex(h0_re.astype(f32), h0_im.astype(f32))
    bu = bu.at[:, 0].add(lam_bar * h0)
    a = jnp.broadcast_to(lam_bar, bu.shape)

    def combine(left, right):
        a1, b1 = left
        a2, b2 = right
        return a1 * a2, a2 * b1 + b2

    _, h = lax.associative_scan(combine, (a, bu), axis=1)
    c = lax.complex(c_re.astype(f32), c_im.astype(f32))
    y = jnp.real(jnp.einsum('gcp,btgp->btgc', c, h)) + d_skip.astype(f32) * ug
    y = jax.nn.gelu(y.reshape(bsz, t, S5_W).astype(u.dtype))
    y = y * jax.nn.sigmoid(y @ glu_w + glu_b)
    h_last = h[:, -1]
    return y, jnp.real(h_last).astype(h0_re.dtype), jnp.imag(h_last).astype(h0_re.dtype)


def hybrid_mixer(x, k_past, v_past, h0_re, h0_im, p, i):
    bsz, t, _ = x.shape
    z = x @ p['hyb_w_in'][i]
    q, k, v, u = jnp.split(z, [MOBA_W, 2 * MOBA_W, 3 * MOBA_W], axis=-1)
    q = q.reshape(bsz, t, N_HEADS, HEAD_DIM)
    k = k.reshape(bsz, t, N_HEADS, HEAD_DIM)
    v = v.reshape(bsz, t, N_HEADS, HEAD_DIM)
    k_all = jnp.concatenate([k_past.astype(k.dtype), k], axis=1)
    v_all = jnp.concatenate([v_past.astype(v.dtype), v], axis=1)
    attn = moba_attention(q, k_all, v_all, k_past.shape[1]).reshape(bsz, t, MOBA_W)
    ssm, h_re, h_im = s5_mixer(u, h0_re, h0_im, p['s5_a_re'][i], p['s5_a_im'][i], p['s5_log_dt'][i],
                               p['s5_b_re'][i], p['s5_b_im'][i], p['s5_c_re'][i], p['s5_c_im'][i],
                               p['s5_d'][i], p['s5_glu_w'][i], p['s5_glu_b'][i])
    y = jnp.concatenate([attn, ssm], axis=-1) @ p['hyb_w_out'][i]
    return y, k, v, h_re, h_im


def conv_mixer(x, conv_state, w_in, conv_w, w_out):
    t = x.shape[1]
    bg, cg, v = jnp.split(x @ w_in, 3, axis=-1)
    h = cg * v
    hf = jnp.concatenate([conv_state.astype(h.dtype), h], axis=1)
    conv = conv_w[0] * hf[:, 0:t]
    for j in range(1, CONV_W):
        conv = conv + conv_w[j] * hf[:, j:j + t]
    y = (bg * conv) @ w_out
    return y, hf[:, t:]


def run_trunk(x, kv_past_fn, s5_re0, s5_im0, conv0, p):
    ks, vs, hrs, his, cvs = [], [], [], [], []
    for layer in range(DEPTH):
        i = layer // 2
        g, b = p['ln_g'][layer], p['ln_b'][layer]
        x = layer_norm(ALPHA * x + 0.5 * swiglu(x, p['ffn1_wgu'][layer], p['ffn1_wd'][layer]), g[0], b[0])
        if layer % 2 == 0:
            k_past, v_past = kv_past_fn(i)
            y, k_new, v_new, h_re, h_im = hybrid_mixer(x, k_past, v_past, s5_re0[i], s5_im0[i], p, i)
            ks.append(k_new)
            vs.append(v_new)
            hrs.append(h_re)
            his.append(h_im)
        else:
            y, c_new = conv_mixer(x, conv0[i], p['conv_w_in'][i], p['conv_w'][i], p['conv_w_out'][i])
            cvs.append(c_new)
        x = layer_norm(ALPHA * x + y, g[1], b[1])
        x = layer_norm(ALPHA * x + 0.5 * swiglu(x, p['ffn2_wgu'][layer], p['ffn2_wd'][layer]), g[2], b[2])
    return x, jnp.stack(ks), jnp.stack(vs), jnp.stack(hrs), jnp.stack(his), jnp.stack(cvs)


def setup_inputs(seed: int = 0) -> dict:
    key = jax.random.key(seed)
    ks = jax.random.split(key, 32)
    n_pages = PAST_LEN // PAGE_SIZE
    n_pool = (5 * DEC_BATCH * n_pages) // 4

    def nrm(k, shape, scale):
        return jax.random.normal(k, shape, jnp.float32) * scale

    page_table = jax.random.permutation(ks[7], n_pool)[:DEC_BATCH * n_pages].reshape(DEC_BATCH, n_pages).astype(jnp.int32)
    n_idx = jnp.arange(S5_STATE, dtype=jnp.float32)
    return {
        'x_prompt': nrm(ks[0], (BATCH, SEQ, D_MODEL), 1.0),
        'x_sample': nrm(ks[1], (DEC_BATCH, DEC_SEQ, D_MODEL), 1.0),
        'cache_k': nrm(ks[2], (N_HYB, n_pool, PAGE_SIZE, N_HEADS, HEAD_DIM), 1.0),
        'cache_v': nrm(ks[3], (N_HYB, n_pool, PAGE_SIZE, N_HEADS, HEAD_DIM), 1.0),
        'state_s5_re': nrm(ks[4], (N_HYB, DEC_BATCH, S5_GROUPS, S5_STATE), 0.1),
        'state_s5_im': nrm(ks[5], (N_HYB, DEC_BATCH, S5_GROUPS, S5_STATE), 0.1),
        'state_conv': nrm(ks[6], (N_CONV, DEC_BATCH, CONV_W - 1, D_MODEL), 1.0),
        'page_table': page_table,
        'ffn1_wgu': nrm(ks[8], (DEPTH, D_MODEL, 2 * D_FF), D_MODEL ** -0.5),
        'ffn1_wd': nrm(ks[9], (DEPTH, D_FF, D_MODEL), BETA * D_FF ** -0.5),
        'ffn2_wgu': nrm(ks[10], (DEPTH, D_MODEL, 2 * D_FF), D_MODEL ** -0.5),
        'ffn2_wd': nrm(ks[11], (DEPTH, D_FF, D_MODEL), BETA * D_FF ** -0.5),
        'ln_g': 1.0 + nrm(ks[12], (DEPTH, 3, D_MODEL), 0.01),
        'ln_b': nrm(ks[13], (DEPTH, 3, D_MODEL), 0.01),
        'hyb_w_in': nrm(ks[14], (N_HYB, D_MODEL, 3 * MOBA_W + S5_W), D_MODEL ** -0.5),
        'hyb_w_out': nrm(ks[15], (N_HYB, D_MODEL, D_MODEL), BETA * D_MODEL ** -0.5),
        's5_a_re': -0.5 + nrm(ks[16], (N_HYB, S5_GROUPS, S5_STATE), 0.01),
        's5_a_im': math.pi * n_idx + nrm(ks[17], (N_HYB, S5_GROUPS, S5_STATE), 0.01),
        's5_log_dt': jax.random.uniform(ks[18], (N_HYB, S5_GROUPS), jnp.float32, math.log(DT_MIN), math.log(DT_MAX)),
        's5_b_re': nrm(ks[19], (N_HYB, S5_GROUPS, S5_STATE, S5_GROUP), (2.0 * S5_GROUP) ** -0.5),
        's5_b_im': nrm(ks[20], (N_HYB, S5_GROUPS, S5_STATE, S5_GROUP), (2.0 * S5_GROUP) ** -0.5),
        's5_c_re': nrm(ks[21], (N_HYB, S5_GROUPS, S5_GROUP, S5_STATE), (2.0 * S5_STATE) ** -0.5),
        's5_c_im': nrm(ks[22], (N_HYB, S5_GROUPS, S5_GROUP, S5_STATE), (2.0 * S5_STATE) ** -0.5),
        's5_d': nrm(ks[23], (N_HYB, S5_GROUPS, S5_GROUP), 1.0),
        's5_glu_w': nrm(ks[24], (N_HYB, S5_W, S5_W), S5_W ** -0.5),
        's5_glu_b': nrm(ks[25], (N_HYB, S5_W), 0.01),
        'conv_w_in': nrm(ks[26], (N_CONV, D_MODEL, 3 * D_MODEL), D_MODEL ** -0.5),
        'conv_w': nrm(ks[27], (N_CONV, CONV_W, D_MODEL), CONV_W ** -0.5),
        'conv_w_out': nrm(ks[28], (N_CONV, D_MODEL, D_MODEL), BETA * D_MODEL ** -0.5),
    }


def reference(x_prompt, x_sample, cache_k, cache_v, state_s5_re, state_s5_im, state_conv, page_table,
              ffn1_wgu, ffn1_wd, ffn2_wgu, ffn2_wd, ln_g, ln_b, hyb_w_in, hyb_w_out,
              s5_a_re, s5_a_im, s5_log_dt, s5_b_re, s5_b_im, s5_c_re, s5_c_im, s5_d, s5_glu_w, s5_glu_b,
              conv_w_in, conv_w, conv_w_out):
    p = {'ffn1_wgu': ffn1_wgu, 'ffn1_wd': ffn1_wd, 'ffn2_wgu': ffn2_wgu, 'ffn2_wd': ffn2_wd,
         'ln_g': ln_g, 'ln_b': ln_b, 'hyb_w_in': hyb_w_in, 'hyb_w_out': hyb_w_out,
         's5_a_re': s5_a_re, 's5_a_im': s5_a_im, 's5_log_dt': s5_log_dt,
         's5_b_re': s5_b_re, 's5_b_im': s5_b_im, 's5_c_re': s5_c_re, 's5_c_im': s5_c_im,
         's5_d': s5_d, 's5_glu_w': s5_glu_w, 's5_glu_b': s5_glu_b,
         'conv_w_in': conv_w_in, 'conv_w': conv_w, 'conv_w_out': conv_w_out}
    bp = x_prompt.shape[0]

    def prompt_kv(i):
        empty = jnp.zeros((bp, 0, N_HEADS, HEAD_DIM), x_prompt.dtype)
        return empty, empty

    def sample_kv(i):
        db, n_pages = page_table.shape
        k_past = cache_k[i][page_table].reshape(db, n_pages * PAGE_SIZE, N_HEADS, HEAD_DIM)
        v_past = cache_v[i][page_table].reshape(db, n_pages * PAGE_SIZE, N_HEADS, HEAD_DIM)
        return k_past, v_past

    s5_zero = jnp.zeros((N_HYB, bp, S5_GROUPS, S5_STATE), jnp.float32)
    conv_zero = jnp.zeros((N_CONV, bp, CONV_W - 1, D_MODEL), x_prompt.dtype)
    y_prompt, k_prompt, v_prompt, s5_re_prompt, s5_im_prompt, conv_prompt = run_trunk(
        x_prompt, prompt_kv, s5_zero, s5_zero, conv_zero, p)
    y_sample, k_sample, v_sample, s5_re_sample, s5_im_sample, conv_sample = run_trunk(
        x_sample, sample_kv, state_s5_re, state_s5_im, state_conv, p)
    return (y_prompt, y_sample, k_prompt, v_prompt, s5_re_prompt, s5_im_prompt, conv_prompt,
            k_sample, v_sample, s5_re_sample, s5_im_sample, conv_sample)
```

```python
import functools

import jax
import jax.numpy as jnp
from jax import lax
from jax.experimental import pallas as pl
from jax.experimental.pallas import tpu as pltpu

MOBA_BLOCK = 256
MOBA_TOPK = 3
LN_EPS = 1e-5
SUBLANES = 8
S5_LANE_GROUPS = 8
VMEM_LIMIT_BYTES = 56 * 1024 * 1024

F32 = jnp.float32
BF16 = jnp.bfloat16
NEG_INF = float("-inf")


def _cparams(semantics):
    return pltpu.CompilerParams(dimension_semantics=semantics, vmem_limit_bytes=VMEM_LIMIT_BYTES)


def _layer_norm(y, g, b):
    mu = jnp.mean(y, axis=-1, keepdims=True)
    d = y - mu
    var = jnp.mean(d * d, axis=-1, keepdims=True)
    return d * lax.rsqrt(var + LN_EPS) * g + b


def _ffn_kernel(x32_ref, xb_ref, wg_ref, wu_ref, wd_ref, g_ref, b_ref, o32_ref, ob_ref, acc_ref, *, alpha, nf):
    j = pl.program_id(1)

    @pl.when(j == 0)
    def _():
        acc_ref[...] = jnp.zeros_like(acc_ref)

    xb = xb_ref[...]
    gate = jnp.dot(xb, wg_ref[...], preferred_element_type=F32)
    up = jnp.dot(xb, wu_ref[...], preferred_element_type=F32)
    h = (gate * jax.nn.sigmoid(gate)) * up
    acc_ref[...] += jnp.dot(h.astype(BF16), wd_ref[...], preferred_element_type=F32)

    @pl.when(j == nf - 1)
    def _():
        z = _layer_norm(alpha * x32_ref[...] + 0.5 * acc_ref[...], g_ref[...], b_ref[...])
        o32_ref[...] = z
        ob_ref[...] = z.astype(BF16)


def _ffn(x32, xb, wgu, wd, layer, g, b, alpha, tm, tf):
    m, d = x32.shape
    f = wd.shape[1]
    nf = f // tf
    return pl.pallas_call(
        functools.partial(_ffn_kernel, alpha=alpha, nf=nf),
        grid=(m // tm, nf),
        in_specs=[
            pl.BlockSpec((tm, d), lambda i, j: (i, 0)),
            pl.BlockSpec((tm, d), lambda i, j: (i, 0)),
            pl.BlockSpec((None, d, tf), lambda i, j: (layer, 0, j)),
            pl.BlockSpec((None, d, tf), lambda i, j: (layer, 0, j + nf)),
            pl.BlockSpec((None, tf, d), lambda i, j: (layer, j, 0)),
            pl.BlockSpec((1, d), lambda i, j: (0, 0)),
            pl.BlockSpec((1, d), lambda i, j: (0, 0)),
        ],
        out_specs=[pl.BlockSpec((tm, d), lambda i, j: (i, 0)), pl.BlockSpec((tm, d), lambda i, j: (i, 0))],
        out_shape=[jax.ShapeDtypeStruct((m, d), F32), jax.ShapeDtypeStruct((m, d), BF16)],
        scratch_shapes=[pltpu.VMEM((tm, d), F32)],
        compiler_params=_cparams(("parallel", "arbitrary")),
        name="ffn_ln",
    )(x32, xb, wgu, wgu, wd, g, b)


def _mm_kernel(x_ref, w_ref, o_ref):
    o_ref[...] = jnp.dot(x_ref[...], w_ref[...], preferred_element_type=F32)


def _mm(xb, w, layer, col0, n, tm, tn):
    m, k = xb.shape
    assert col0 % tn == 0 and n % tn == 0
    jb0 = col0 // tn
    return pl.pallas_call(
        _mm_kernel,
        grid=(m // tm, n // tn),
        in_specs=[
            pl.BlockSpec((tm, k), lambda i, j: (i, 0)),
            pl.BlockSpec((None, k, tn), lambda i, j: (layer, 0, jb0 + j)),
        ],
        out_specs=pl.BlockSpec((tm, tn), lambda i, j: (i, j)),
        out_shape=jax.ShapeDtypeStruct((m, n), F32),
        compiler_params=_cparams(("parallel", "arbitrary")),
        name="proj",
    )(xb, w)


def _mm_res_ln_kernel(*refs, nparts, alpha):
    a_refs = refs[:nparts]
    w_refs = refs[nparts:2 * nparts]
    x32_ref, g_ref, b_ref, o32_ref, ob_ref = refs[2 * nparts:]
    y = alpha * x32_ref[...]
    for a_ref, w_ref in zip(a_refs, w_refs):
        y = y + jnp.dot(a_ref[...], w_ref[...], preferred_element_type=F32)
    z = _layer_norm(y, g_ref[...], b_ref[...])
    o32_ref[...] = z
    ob_ref[...] = z.astype(BF16)


def _mm_res_ln(parts, w, layer, x32, g, b, alpha, tm):
    m, d = x32.shape
    kp = parts[0].shape[1]
    assert all(p.shape[1] == kp for p in parts)
    in_specs = [pl.BlockSpec((tm, kp), lambda i: (i, 0)) for _ in parts]
    in_specs += [pl.BlockSpec((None, kp, d), lambda i, r=r: (layer, r, 0)) for r in range(len(parts))]
    in_specs += [
        pl.BlockSpec((tm, d), lambda i: (i, 0)),
        pl.BlockSpec((1, d), lambda i: (0, 0)),
        pl.BlockSpec((1, d), lambda i: (0, 0)),
    ]
    return pl.pallas_call(
        functools.partial(_mm_res_ln_kernel, nparts=len(parts), alpha=alpha),
        grid=(m // tm,),
        in_specs=in_specs,
        out_specs=[pl.BlockSpec((tm, d), lambda i: (i, 0)), pl.BlockSpec((tm, d), lambda i: (i, 0))],
        out_shape=[jax.ShapeDtypeStruct((m, d), F32), jax.ShapeDtypeStruct((m, d), BF16)],
        compiler_params=_cparams(("parallel",)),
        name="out_proj_ln",
    )(*parts, *([w] * len(parts)), x32, g, b)


def _glu_kernel(y_ref, w_ref, b_ref, o_ref):
    y = y_ref[...]
    z = jnp.dot(y.astype(BF16), w_ref[...], preferred_element_type=F32) + b_ref[...]
    o_ref[...] = (y * jax.nn.sigmoid(z)).astype(o_ref.dtype)


def _glu(y, w, layer, b, tm):
    m, n = y.shape
    return pl.pallas_call(
        _glu_kernel,
        grid=(m // tm,),
        in_specs=[
            pl.BlockSpec((tm, n), lambda i: (i, 0)),
            pl.BlockSpec((None, n, n), lambda i: (layer, 0, 0)),
            pl.BlockSpec((1, n), lambda i: (0, 0)),
        ],
        out_specs=pl.BlockSpec((tm, n), lambda i: (i, 0)),
        out_shape=jax.ShapeDtypeStruct((m, n), BF16),
        compiler_params=_cparams(("parallel",)),
        name="s5_glu",
    )(y, w, b)


CONV_PAD = 8


def _conv_kernel(bg_ref, cg_ref, v_ref, st_ref, w_ref, c_ref, ns_ref, hbuf, *, tt, nt, cw):
    t = pl.program_id(2)
    lo = CONV_PAD - (cw - 1)

    @pl.when(t == 0)
    def _():
        hbuf[lo:CONV_PAD, :] = st_ref[...]

    hbuf[CONV_PAD:CONV_PAD + tt, :] = cg_ref[...] * v_ref[...]
    w = w_ref[...]
    conv = w[0:1, :] * hbuf[lo:lo + tt, :]
    for j in range(1, cw):
        conv = conv + w[j:j + 1, :] * hbuf[lo + j:lo + j + tt, :]
    c_ref[...] = (bg_ref[...] * conv).astype(c_ref.dtype)
    tail = hbuf[lo + tt:CONV_PAD + tt, :]

    @pl.when(t == nt - 1)
    def _():
        ns_ref[...] = tail

    hbuf[lo:CONV_PAD, :] = tail


def _conv_core(z, state, conv_w, layer, bsz, t_len, tt, tn):
    m, d3 = z.shape
    d = d3 // 3
    cw = conv_w.shape[1]
    nt, nj = t_len // tt, d // tn
    return pl.pallas_call(
        functools.partial(_conv_kernel, tt=tt, nt=nt, cw=cw),
        grid=(bsz, nj, nt),
        in_specs=[
            pl.BlockSpec((tt, tn), lambda b, j, t: (b * nt + t, j)),
            pl.BlockSpec((tt, tn), lambda b, j, t: (b * nt + t, j + nj)),
            pl.BlockSpec((tt, tn), lambda b, j, t: (b * nt + t, j + 2 * nj)),
            pl.BlockSpec((None, cw - 1, tn), lambda b, j, t: (b, 0, j)),
            pl.BlockSpec((None, cw, tn), lambda b, j, t: (layer, 0, j)),
        ],
        out_specs=[
            pl.BlockSpec((tt, tn), lambda b, j, t: (b * nt + t, j)),
            pl.BlockSpec((None, cw - 1, tn), lambda b, j, t: (b, 0, j)),
        ],
        out_shape=[jax.ShapeDtypeStruct((m, d), BF16), jax.ShapeDtypeStruct((bsz, cw - 1, d), F32)],
        scratch_shapes=[pltpu.VMEM((CONV_PAD + tt, tn), F32)],
        compiler_params=_cparams(("parallel", "parallel", "arbitrary")),
        name="conv_core",
    )(z, z, z, state, conv_w)


def _moba_select(gate, n_valid):
    rows, nb = gate.shape
    lane = lax.broadcasted_iota(jnp.int32, (rows, nb), 1)
    valid = lane < n_valid
    gate = jnp.where(valid, gate, NEG_INF)
    beaten_by = jnp.zeros((rows, nb), jnp.int32)
    for jp in range(nb):
        col = gate[:, jp:jp + 1]
        beats = (col > gate) | ((col == gate) & (lane > jp))
        beaten_by = beaten_by + jnp.where(beats, 1, 0)
    return valid & (beaten_by < MOBA_TOPK)


def _column(mat_f32, lane, j):
    return jnp.sum(jnp.where(lane == j, mat_f32, 0.0), axis=1, keepdims=True)


def _moba_prompt_kernel(q_ref, k_ref, v_ref, o_ref, kb_scr, vb_scr, km_scr, *, nb, scale):
    qi = pl.program_id(2)
    blk = MOBA_BLOCK

    @pl.when(qi == 0)
    def _():
        kb_scr[...] = k_ref[...].astype(BF16)
        vb_scr[...] = v_ref[...].astype(BF16)
        for j in range(nb):
            km_scr[j:j + 1, :] = jnp.mean(k_ref[j * blk:(j + 1) * blk, :], axis=0, keepdims=True)

    q = q_ref[...]
    qb = q.astype(BF16)
    gate = lax.dot_general(q, km_scr[...], (((1,), (1,)), ((), ())),
                           precision=lax.Precision.HIGHEST, preferred_element_type=F32)
    sel = _moba_select(gate, qi).astype(F32)
    lane = lax.broadcasted_iota(jnp.int32, sel.shape, 1)

    def scores(kj):
        return lax.dot_general(qb, kj, (((1,), (1,)), ((), ())), preferred_element_type=F32) * scale

    row0 = pl.multiple_of(qi * blk, blk)
    s = scores(kb_scr[pl.ds(row0, blk), :])
    causal = lax.broadcasted_iota(jnp.int32, (blk, blk), 1) <= lax.broadcasted_iota(jnp.int32, (blk, blk), 0)
    s = jnp.where(causal, s, NEG_INF)
    m = jnp.max(s, axis=1, keepdims=True)
    p = jnp.exp(s - m)
    l = jnp.sum(p, axis=1, keepdims=True)
    acc = jnp.dot(p.astype(BF16), vb_scr[pl.ds(row0, blk), :], preferred_element_type=F32)

    def body(j, carry):
        m, l, acc = carry
        r = pl.multiple_of(j * blk, blk)
        s = scores(kb_scr[pl.ds(r, blk), :])
        s = jnp.where(_column(sel, lane, j) > 0.5, s, NEG_INF)
        m_new = jnp.maximum(m, jnp.max(s, axis=1, keepdims=True))
        a = jnp.exp(m - m_new)
        p = jnp.exp(s - m_new)
        l = a * l + jnp.sum(p, axis=1, keepdims=True)
        acc = a * acc + jnp.dot(p.astype(BF16), vb_scr[pl.ds(r, blk), :], preferred_element_type=F32)
        return m_new, l, acc

    m, l, acc = lax.fori_loop(0, qi, body, (m, l, acc))
    o_ref[...] = (acc / l).astype(o_ref.dtype)


def _moba_prompt(q, k, v, bsz, t_len, nh, hd):
    m = q.shape[0]
    assert t_len % MOBA_BLOCK == 0
    nb = t_len // MOBA_BLOCK
    return pl.pallas_call(
        functools.partial(_moba_prompt_kernel, nb=nb, scale=hd ** -0.5),
        grid=(bsz, nh, nb),
        in_specs=[
            pl.BlockSpec((MOBA_BLOCK, hd), lambda b, h, i: (b * nb + i, h)),
            pl.BlockSpec((t_len, hd), lambda b, h, i: (b, h)),
            pl.BlockSpec((t_len, hd), lambda b, h, i: (b, h)),
        ],
        out_specs=pl.BlockSpec((MOBA_BLOCK, hd), lambda b, h, i: (b * nb + i, h)),
        out_shape=jax.ShapeDtypeStruct((m, nh * hd), BF16),
        scratch_shapes=[pltpu.VMEM((t_len, hd), BF16), pltpu.VMEM((t_len, hd), BF16), pltpu.VMEM((nb, hd), F32)],
        compiler_params=_cparams(("parallel", "parallel", "arbitrary")),
        name="moba_prompt",
    )(q, k, v)


def _kmean_kernel(pt_ref, *refs, pages_per_block, blocks_per_step):
    del pt_ref
    page_refs, o_ref = refs[:-1], refs[-1]
    for mblk in range(blocks_per_step):
        tot = jnp.sum(page_refs[mblk * pages_per_block][...], axis=0)
        for r in range(1, pages_per_block):
            tot = tot + jnp.sum(page_refs[mblk * pages_per_block + r][...], axis=0)
        o_ref[mblk] = tot / MOBA_BLOCK


def _kmean_past(cache_k, layer, page_table, blocks_per_step=2):
    _, _, page, nh, hd = cache_k.shape
    db, n_pages = page_table.shape
    assert MOBA_BLOCK % page == 0
    ppb = MOBA_BLOCK // page
    pps = ppb * blocks_per_step
    nblk = n_pages // ppb
    assert n_pages % pps == 0

    def page_spec(r):
        return pl.BlockSpec((None, None, page, nh, hd), lambda b, s, pt: (layer, pt[b, s * pps + r], 0, 0, 0))

    return pl.pallas_call(
        functools.partial(_kmean_kernel, pages_per_block=ppb, blocks_per_step=blocks_per_step),
        grid_spec=pltpu.PrefetchScalarGridSpec(
            num_scalar_prefetch=1,
            grid=(db, n_pages // pps),
            in_specs=[page_spec(r) for r in range(pps)],
            out_specs=pl.BlockSpec((None, blocks_per_step, nh, hd), lambda b, s, pt: (b, s, 0, 0)),
        ),
        out_shape=jax.ShapeDtypeStruct((db, nblk, nh, hd), F32),
        compiler_params=_cparams(("parallel", "arbitrary")),
        name="kmean_past",
    )(page_table, *([cache_k] * pps))


def _sample_attn_kernel(pt_ref, q_ref, kn_ref, vn_ref, km_ref, *refs, nh, n_past_blocks, pages_per_block, scale):
    del pt_ref
    kp_refs = refs[:pages_per_block]
    vp_refs = refs[pages_per_block:2 * pages_per_block]
    o_ref, sel_scr, m_scr, l_scr, acc_scr = refs[2 * pages_per_block:]
    j = pl.program_id(1)
    rows = q_ref.shape[0]
    nh_shift = nh.bit_length() - 1
    qb = q_ref[...].astype(BF16)

    def head_match(ncols):
        r = lax.broadcasted_iota(jnp.int32, (rows, ncols), 0)
        c = lax.broadcasted_iota(jnp.int32, (rows, ncols), 1)
        return r, c, (r & (nh - 1)) == (c & (nh - 1))

    def scores(k):
        return lax.dot_general(qb, k.astype(BF16), (((1,), (1,)), ((), ())), preferred_element_type=F32) * scale

    @pl.when(j == 0)
    def _():
        km = km_ref[...]
        g_all = lax.dot_general(q_ref[...], km, (((1,), (1,)), ((), ())),
                                precision=lax.Precision.HIGHEST, preferred_element_type=F32)
        _, c, same = head_match(n_past_blocks * nh)
        g_all = jnp.where(same, g_all, 0.0)
        pick_r = lax.broadcasted_iota(jnp.int32, (n_past_blocks * nh, n_past_blocks), 0)
        pick_c = lax.broadcasted_iota(jnp.int32, (n_past_blocks * nh, n_past_blocks), 1)
        pick = jnp.where((pick_r >> nh_shift) == pick_c, 1.0, 0.0).astype(F32)
        gate = jnp.dot(g_all, pick, precision=lax.Precision.HIGHEST, preferred_element_type=F32)
        sel_scr[...] = _moba_select(gate, n_past_blocks).astype(F32)
        r, c, same = head_match(kn_ref.shape[0])
        s = jnp.where(same & ((c >> nh_shift) <= (r >> nh_shift)), scores(kn_ref[...]), NEG_INF)
        m = jnp.max(s, axis=1, keepdims=True)
        p = jnp.exp(s - m)
        m_scr[...] = m
        l_scr[...] = jnp.sum(p, axis=1, keepdims=True)
        acc_scr[...] = jnp.dot(p.astype(BF16), vn_ref[...].astype(BF16), preferred_element_type=F32)

    sel = sel_scr[...]
    lane = lax.broadcasted_iota(jnp.int32, sel.shape, 1)
    chosen = _column(sel, lane, j) > 0.5
    for r in range(pages_per_block):
        k = kp_refs[r][...]
        _, _, same = head_match(k.shape[0])
        s = jnp.where(same & chosen, scores(k), NEG_INF)
        m = m_scr[...]
        m_new = jnp.maximum(m, jnp.max(s, axis=1, keepdims=True))
        a = jnp.exp(m - m_new)
        p = jnp.exp(s - m_new)
        m_scr[...] = m_new
        l_scr[...] = a * l_scr[...] + jnp.sum(p, axis=1, keepdims=True)
        acc_scr[...] = a * acc_scr[...] + jnp.dot(p.astype(BF16), vp_refs[r][...].astype(BF16),
                                                  preferred_element_type=F32)

    @pl.when(j == n_past_blocks - 1)
    def _():
        o_ref[...] = (acc_scr[...] / l_scr[...]).astype(o_ref.dtype)


def _sample_attn(q, k_new, v_new, cache_k, cache_v, layer, page_table, db, t_len, nh, hd):
    n_hyb, n_pool, page, _, _ = cache_k.shape
    n_pages = page_table.shape[1]
    past_len = n_pages * page
    assert past_len % MOBA_BLOCK == 0 and t_len <= MOBA_BLOCK and MOBA_BLOCK % page == 0
    assert nh & (nh - 1) == 0, "head matching uses bit masks"
    ppb = MOBA_BLOCK // page
    n_past_blocks = past_len // MOBA_BLOCK
    rows = t_len * nh
    kmean = _kmean_past(cache_k, layer, page_table).reshape(db, n_past_blocks * nh, hd)
    ck = cache_k.reshape(n_hyb, n_pool, page * nh, hd)
    cv = cache_v.reshape(n_hyb, n_pool, page * nh, hd)

    def page_spec(r):
        return pl.BlockSpec((None, None, page * nh, hd), lambda b, j, pt: (layer, pt[b, j * ppb + r], 0, 0))

    def row_spec(nrows):
        return pl.BlockSpec((None, nrows, hd), lambda b, j, pt: (b, 0, 0))

    out = pl.pallas_call(
        functools.partial(_sample_attn_kernel, nh=nh, n_past_blocks=n_past_blocks, pages_per_block=ppb,
                          scale=hd ** -0.5),
        grid_spec=pltpu.PrefetchScalarGridSpec(
            num_scalar_prefetch=1,
            grid=(db, n_past_blocks),
            in_specs=[row_spec(rows), row_spec(rows), row_spec(rows), row_spec(n_past_blocks * nh)]
            + [page_spec(r) for r in range(ppb)] * 2,
            out_specs=row_spec(rows),
            scratch_shapes=[pltpu.VMEM((rows, n_past_blocks), F32), pltpu.VMEM((rows, 1), F32),
                            pltpu.VMEM((rows, 1), F32), pltpu.VMEM((rows, hd), F32)],
        ),
        out_shape=jax.ShapeDtypeStruct((db, rows, hd), BF16),
        compiler_params=_cparams(("parallel", "arbitrary")),
        name="sample_attn",
    )(page_table, q.reshape(db, rows, hd), k_new.reshape(db, rows, hd), v_new.reshape(db, rows, hd), kmean,
      *([ck] * ppb), *([cv] * ppb))
    return out.reshape(db * t_len, nh * hd)


def _s5_kernel(u_ref, bre_ref, bim_ref, cre_ref, cimn_ref, lre_ref, lim_ref, llre_ref, llim_ref, d_ref,
               h0re_ref, h0im_ref, y_ref, htre_ref, htim_ref, hr_scr, hi_scr, cre_scr, cim_scr,
               *, seg_len, chained, n_chunks):
    c = pl.program_id(1)
    nseg = SUBLANES
    nstate = hr_scr.shape[1]
    n_kb = bre_ref.shape[0]
    ku = bre_ref.shape[1]
    ks = bre_ref.shape[2]

    for kb in range(n_kb):
        ukb = u_ref[:, kb * ku:(kb + 1) * ku]
        hr_scr[:, kb * ks:(kb + 1) * ks] = jnp.dot(ukb, bre_ref[kb], precision=lax.Precision.HIGHEST,
                                                   preferred_element_type=F32)
        hi_scr[:, kb * ks:(kb + 1) * ks] = jnp.dot(ukb, bim_ref[kb], precision=lax.Precision.HIGHEST,
                                                   preferred_element_type=F32)

    lr = jnp.broadcast_to(lre_ref[...], (nseg, nstate))
    li = jnp.broadcast_to(lim_ref[...], (nseg, nstate))

    def advance(i, hr, hi):
        r = pl.multiple_of(i * nseg, nseg)
        nr = lr * hr - li * hi + hr_scr[pl.ds(r, nseg), :]
        ni = lr * hi + li * hr + hi_scr[pl.ds(r, nseg), :]
        return r, nr, ni

    def scan_only(i, carry):
        _, nr, ni = advance(i, *carry)
        return nr, ni

    def scan_store(i, carry):
        r, nr, ni = advance(i, *carry)
        hr_scr[pl.ds(r, nseg), :] = nr
        hi_scr[pl.ds(r, nseg), :] = ni
        return nr, ni

    if chained:
        @pl.when(c == 0)
        def _():
            cre_scr[...] = jnp.zeros_like(cre_scr)
            cim_scr[...] = jnp.zeros_like(cim_scr)
            cre_scr[0:1, :] = h0re_ref[...]
            cim_scr[0:1, :] = h0im_ref[...]

        er, ei = lax.fori_loop(0, seg_len, scan_only, (cre_scr[...], cim_scr[...]))
        llr, lli = llre_ref[...], llim_ref[...]
        fr, fi = er[0:1, :], ei[0:1, :]
        for j in range(1, nseg):
            cre_scr[j:j + 1, :] = fr
            cim_scr[j:j + 1, :] = fi
            if j < nseg - 1:
                fr, fi = (er[j:j + 1, :] + (llr * fr - lli * fi), ei[j:j + 1, :] + (llr * fi + lli * fr))
        init = (cre_scr[...], cim_scr[...])
    else:
        init = (h0re_ref[...], h0im_ref[...])

    hr, hi = lax.fori_loop(0, seg_len, scan_store, init)

    if chained:
        last_r, last_i = hr[nseg - 1:nseg, :], hi[nseg - 1:nseg, :]
        cre_scr[...] = jnp.zeros_like(cre_scr)
        cim_scr[...] = jnp.zeros_like(cim_scr)
        cre_scr[0:1, :] = last_r
        cim_scr[0:1, :] = last_i

        @pl.when(c == n_chunks - 1)
        def _():
            htre_ref[...] = last_r
            htim_ref[...] = last_i
    else:
        htre_ref[...] = hr
        htim_ref[...] = hi

    for kb in range(n_kb):
        hre = hr_scr[:, kb * ks:(kb + 1) * ks].astype(BF16)
        him = hi_scr[:, kb * ks:(kb + 1) * ks].astype(BF16)
        ykb = (jnp.dot(hre, cre_ref[kb], preferred_element_type=F32)
               + jnp.dot(him, cimn_ref[kb], preferred_element_type=F32))
        cols = slice(kb * ku, (kb + 1) * ku)
        y_ref[:, cols] = jax.nn.gelu(ykb + d_ref[:, cols] * u_ref[:, cols])


def _s5_discretise(a_re, a_im, log_dt, b_re, b_im, c_re, c_im, d_skip, seg_len):
    g, p = a_re.shape
    ch = b_re.shape[-1]
    lam = lax.complex(a_re.astype(F32), a_im.astype(F32))
    dt = jnp.exp(log_dt.astype(F32))[:, None]
    lam_bar = jnp.exp(lam * dt)
    b_bar = ((lam_bar - 1.0) / lam)[..., None] * lax.complex(b_re.astype(F32), b_im.astype(F32))
    lam_seg = lam_bar
    assert seg_len & (seg_len - 1) == 0
    for _ in range(seg_len.bit_length() - 1):
        lam_seg = lam_seg * lam_seg
    gl = S5_LANE_GROUPS
    n_kb = g // gl
    eye = jnp.eye(gl, dtype=F32)

    def pack_b(x):
        x = x.reshape(n_kb, gl, p, ch).transpose(0, 1, 3, 2)
        return jnp.einsum("kgcp,gh->kgchp", x, eye).reshape(n_kb, gl * ch, gl * p)

    def pack_c(x):
        x = x.reshape(n_kb, gl, ch, p).transpose(0, 1, 3, 2)
        return jnp.einsum("kgpc,gh->kgphc", x, eye).reshape(n_kb, gl * p, gl * ch)

    flat = lambda x: x.reshape(1, g * p)
    return dict(
        bre=pack_b(jnp.real(b_bar)), bim=pack_b(jnp.imag(b_bar)),
        cre=pack_c(c_re.astype(F32)).astype(BF16), cimn=pack_c(-c_im.astype(F32)).astype(BF16),
        lre=flat(jnp.real(lam_bar)), lim=flat(jnp.imag(lam_bar)),
        llre=flat(jnp.real(lam_seg)), llim=flat(jnp.imag(lam_seg)),
        d=d_skip.astype(F32).reshape(1, g * ch),
    )


def _s5(u, h0_re, h0_im, prm, seg_len, chained):
    m, w = u.shape
    gp = prm["lre"].shape[1]
    n_kb, ku, ks = prm["bre"].shape
    chunk = SUBLANES * seg_len
    if chained:
        nseq = h0_re.shape[0]
        n_chunks = m // nseq // chunk
        grid = (nseq, n_chunks)
        u_spec = pl.BlockSpec((chunk, w), lambda b, c: (b * n_chunks + c, 0))
        h_spec = pl.BlockSpec((None, 1, gp), lambda b, c: (b, 0, 0))
        h_shape = jax.ShapeDtypeStruct((nseq, 1, gp), F32)
    else:
        assert m == chunk
        n_chunks = 1
        grid = (1, 1)
        u_spec = pl.BlockSpec((chunk, w), lambda b, c: (0, 0))
        h_spec = pl.BlockSpec((SUBLANES, gp), lambda b, c: (0, 0))
        h_shape = jax.ShapeDtypeStruct((SUBLANES, gp), F32)
    full = lambda a: pl.BlockSpec(a.shape, lambda b, c: (0,) * a.ndim)
    names = ("bre", "bim", "cre", "cimn", "lre", "lim", "llre", "llim", "d")

    def swap_rows(a, n_outer, n_inner):
        return a.reshape(m // chunk, n_outer, n_inner, w).transpose(0, 2, 1, 3).reshape(m, w)

    y, ht_re, ht_im = pl.pallas_call(
        functools.partial(_s5_kernel, seg_len=seg_len, chained=chained, n_chunks=n_chunks),
        grid=grid,
        in_specs=[u_spec] + [full(prm[n]) for n in names] + [h_spec, h_spec],
        out_specs=[u_spec, h_spec, h_spec],
        out_shape=[jax.ShapeDtypeStruct((m, w), F32), h_shape, h_shape],
        scratch_shapes=[pltpu.VMEM((chunk, gp), F32), pltpu.VMEM((chunk, gp), F32),
                        pltpu.VMEM((SUBLANES, gp), F32), pltpu.VMEM((SUBLANES, gp), F32)],
        compiler_params=_cparams(("parallel", "arbitrary")),
        name="s5_scan",
    )(swap_rows(u, SUBLANES, seg_len), *[prm[n] for n in names], h0_re, h0_im)
    return swap_rows(y, seg_len, SUBLANES), ht_re, ht_im


def _pick_tile(n, target):
    t = min(n, target)
    while n % t:
        t //= 2
    return t


def _run_trunk(x, prm, s5_prms, kv, s5_re0, s5_im0, conv0, chained):
    bsz, t_len, d = x.shape
    m = bsz * t_len
    depth = prm["ffn1_wgu"].shape[0]
    alpha = (2.0 * depth) ** 0.25
    nh, hd = kv["nh"], kv["hd"]
    moba_w = nh * hd
    s5_w = d - moba_w
    g_cnt, p_cnt = prm["s5_a_re"].shape[1:]
    d_ff = prm["ffn1_wd"].shape[1]
    tm = _pick_tile(m, 512)
    tf = _pick_tile(d_ff, 512)
    tm_small = _pick_tile(m, 256)

    x32 = x.reshape(m, d).astype(F32)
    xb = x32.astype(BF16)
    row = lambda a: a.reshape(1, -1).astype(F32)
    ks, vs, hrs, his, cvs = [], [], [], [], []
    for layer in range(depth):
        i = layer // 2
        g, b = prm["ln_g"][layer], prm["ln_b"][layer]
        x32, xb = _ffn(x32, xb, prm["ffn1_wgu"], prm["ffn1_wd"], layer, row(g[0]), row(b[0]), alpha, tm, tf)
        if layer % 2 == 0:
            w_in = prm["hyb_w_in"]
            tn = _pick_tile(moba_w, 512)
            q = _mm(xb, w_in, i, 0, moba_w, tm, tn)
            k = _mm(xb, w_in, i, moba_w, moba_w, tm, tn)
            v = _mm(xb, w_in, i, 2 * moba_w, moba_w, tm, tn)
            u = _mm(xb, w_in, i, 3 * moba_w, s5_w, tm, tn)
            if chained:
                attn = _moba_prompt(q, k, v, bsz, t_len, nh, hd)
            else:
                attn = _sample_attn(q, k, v, kv["cache_k"], kv["cache_v"], i, kv["page_table"], bsz, t_len, nh, hd)
            sp = s5_prms[i]
            if chained:
                y, h_re, h_im = _s5(u, s5_re0[i].reshape(bsz, 1, g_cnt * p_cnt),
                                    s5_im0[i].reshape(bsz, 1, g_cnt * p_cnt), sp, sp["seg_len"], True)
            else:
                y, h_re, h_im = _s5(u, s5_re0[i].reshape(bsz, g_cnt * p_cnt), s5_im0[i].reshape(bsz, g_cnt * p_cnt),
                                    sp, sp["seg_len"], False)
            ssm = _glu(y, prm["s5_glu_w"], i, row(prm["s5_glu_b"][i]), tm_small)
            x32, xb = _mm_res_ln([attn, ssm], prm["hyb_w_out"], i, x32, row(g[1]), row(b[1]), alpha, tm_small)
            ks.append(k.reshape(bsz, t_len, nh, hd))
            vs.append(v.reshape(bsz, t_len, nh, hd))
            hrs.append(h_re.reshape(bsz, g_cnt, p_cnt))
            his.append(h_im.reshape(bsz, g_cnt, p_cnt))
        else:
            z = _mm(xb, prm["conv_w_in"], i, 0, 3 * d, tm, _pick_tile(d, 512))
            tt = _pick_tile(t_len, 512)
            c, c_new = _conv_core(z, conv0[i], prm["conv_w"], i, bsz, t_len, tt, _pick_tile(d, 512))
            x32, xb = _mm_res_ln([c], prm["conv_w_out"], i, x32, row(g[1]), row(b[1]), alpha, tm_small)
            cvs.append(c_new)
        x32, xb = _ffn(x32, xb, prm["ffn2_wgu"], prm["ffn2_wd"], layer, row(g[2]), row(b[2]), alpha, tm, tf)
    return (x32.reshape(bsz, t_len, d), jnp.stack(ks), jnp.stack(vs), jnp.stack(hrs), jnp.stack(his),
            jnp.stack(cvs))


def kernel(x_prompt, x_sample, cache_k, cache_v, state_s5_re, state_s5_im, state_conv, page_table, ffn1_wgu, ffn1_wd, ffn2_wgu, ffn2_wd, ln_g, ln_b, hyb_w_in, hyb_w_out, s5_a_re, s5_a_im, s5_log_dt, s5_b_re, s5_b_im, s5_c_re, s5_c_im, s5_d, s5_glu_w, s5_glu_b, conv_w_in, conv_w, conv_w_out):
    n_hyb, _, _, nh, hd = cache_k.shape
    bp, t_prompt, d = x_prompt.shape
    db, t_sample, _ = x_sample.shape
    n_conv, cw = conv_w.shape[:2]
    g_cnt, p_cnt = s5_a_re.shape[1:]
    assert db == SUBLANES, "the sample scan lays the decode batch along the vreg sublanes"
    assert (s5_b_re.shape[-1] * S5_LANE_GROUPS) == 128

    prm = dict(
        ffn1_wgu=ffn1_wgu.astype(BF16), ffn1_wd=ffn1_wd.astype(BF16),
        ffn2_wgu=ffn2_wgu.astype(BF16), ffn2_wd=ffn2_wd.astype(BF16),
        ln_g=ln_g, ln_b=ln_b, hyb_w_in=hyb_w_in.astype(BF16), hyb_w_out=hyb_w_out.astype(BF16),
        s5_a_re=s5_a_re, s5_glu_w=s5_glu_w.astype(BF16), s5_glu_b=s5_glu_b,
        conv_w_in=conv_w_in.astype(BF16), conv_w=conv_w.astype(F32), conv_w_out=conv_w_out.astype(BF16),
    )

    def s5_prms(seg_len):
        out = []
        for i in range(n_hyb):
            sp = _s5_discretise(s5_a_re[i], s5_a_im[i], s5_log_dt[i], s5_b_re[i], s5_b_im[i], s5_c_re[i],
                                s5_c_im[i], s5_d[i], seg_len)
            sp["seg_len"] = seg_len
            out.append(sp)
        return out

    prompt_seg = _pick_tile(t_prompt // SUBLANES, 32)
    kv_none = dict(nh=nh, hd=hd)
    s5_zero = jnp.zeros((n_hyb, bp, g_cnt, p_cnt), F32)
    conv_zero = jnp.zeros((n_conv, bp, cw - 1, d), x_prompt.dtype)
    y_p, k_p, v_p, hr_p, hi_p, cv_p = _run_trunk(x_prompt, prm, s5_prms(prompt_seg), kv_none, s5_zero, s5_zero,
                                                 conv_zero, True)
    kv = dict(nh=nh, hd=hd, cache_k=cache_k, cache_v=cache_v, page_table=page_table)
    y_s, k_s, v_s, hr_s, hi_s, cv_s = _run_trunk(x_sample, prm, s5_prms(t_sample), kv, state_s5_re, state_s5_im,
                                                 state_conv, False)
    return (y_p, y_s, k_p, v_p, hr_p, hi_p, cv_p, k_s, v_s, hr_s, hi_s, cv_s)
```

```python
import functools

import jax
import jax.numpy as jnp
from jax import lax
from jax.experimental import pallas as pl
from jax.experimental.pallas import tpu as pltpu

MOBA_BLOCK = 256
MOBA_TOPK = 3
LN_EPS = 1e-5
SUBLANES = 8
S5_LANE_GROUPS = 8
S5_SCAN_LANES = 512
S5_SCAN_UNROLL = 4
FFN_ROW_SLAB = 256
VMEM_LIMIT_BYTES = 60 * 1024 * 1024

F32 = jnp.float32
BF16 = jnp.bfloat16
NEG_INF = float("-inf")


def _cparams(semantics):
    return pltpu.CompilerParams(dimension_semantics=semantics, vmem_limit_bytes=VMEM_LIMIT_BYTES)


def _layer_norm(y, g, b):
    mu = jnp.mean(y, axis=-1, keepdims=True)
    d = y - mu
    var = jnp.mean(d * d, axis=-1, keepdims=True)
    return d * lax.rsqrt(var + LN_EPS) * g + b


def _ffn_kernel(x32_ref, xb_ref, wg_ref, wu_ref, wd_ref, g_ref, b_ref, o32_ref, ob_ref, *, alpha, nf):
    j = pl.program_id(1)
    tm = xb_ref.shape[0]
    slab = min(tm, FFN_ROW_SLAB)
    @pl.when(j == 0)
    def _():
        o32_ref[...] = jnp.zeros_like(o32_ref)

    for r in range(tm // slab):
        rows = slice(r * slab, (r + 1) * slab)
        xb = xb_ref[rows, :]
        gate = jnp.dot(xb, wg_ref[...], preferred_element_type=F32)
        up = jnp.dot(xb, wu_ref[...], preferred_element_type=F32)
        h = (gate * jax.nn.sigmoid(gate)) * up
        o32_ref[rows, :] += jnp.dot(h.astype(BF16), wd_ref[...], preferred_element_type=F32)

    @pl.when(j == nf - 1)
    def _():
        for r in range(tm // slab):
            rows = slice(r * slab, (r + 1) * slab)
            z = _layer_norm(alpha * x32_ref[rows, :] + 0.5 * o32_ref[rows, :], g_ref[...], b_ref[...])
            o32_ref[rows, :] = z
            ob_ref[rows, :] = z.astype(BF16)


def _ffn(x32, xb, wgu, wd, layer, g, b, alpha, tm, tf):
    m, d = x32.shape
    f = wd.shape[1]
    nf = f // tf
    return pl.pallas_call(
        functools.partial(_ffn_kernel, alpha=alpha, nf=nf),
        grid=(m // tm, nf),
        in_specs=[
            pl.BlockSpec((tm, d), lambda i, j: (i, 0), pipeline_mode=pl.Buffered(1)),
            pl.BlockSpec((tm, d), lambda i, j: (i, 0)),
            pl.BlockSpec((None, d, tf), lambda i, j: (layer, 0, j)),
            pl.BlockSpec((None, d, tf), lambda i, j: (layer, 0, j + nf)),
            pl.BlockSpec((None, tf, d), lambda i, j: (layer, j, 0)),
            pl.BlockSpec((1, d), lambda i, j: (0, 0)),
            pl.BlockSpec((1, d), lambda i, j: (0, 0)),
        ],
        out_specs=[pl.BlockSpec((tm, d), lambda i, j: (i, 0)), pl.BlockSpec((tm, d), lambda i, j: (i, 0))],
        out_shape=[jax.ShapeDtypeStruct((m, d), F32), jax.ShapeDtypeStruct((m, d), BF16)],
        compiler_params=_cparams(("parallel", "arbitrary")),
        name="ffn_ln",
    )(x32, xb, wgu, wgu, wd, g, b)


def _mm_kernel(x_ref, w_ref, o_ref):
    o_ref[...] = jnp.dot(x_ref[...], w_ref[...], preferred_element_type=F32)


def _mm(xb, w, layer, col0, n, tm, tn):
    m, k = xb.shape
    assert col0 % tn == 0 and n % tn == 0
    jb0 = col0 // tn
    return pl.pallas_call(
        _mm_kernel,
        grid=(m // tm, n // tn),
        in_specs=[
            pl.BlockSpec((tm, k), lambda i, j: (i, 0)),
            pl.BlockSpec((None, k, tn), lambda i, j: (layer, 0, jb0 + j)),
        ],
        out_specs=pl.BlockSpec((tm, tn), lambda i, j: (i, j)),
        out_shape=jax.ShapeDtypeStruct((m, n), F32),
        compiler_params=_cparams(("parallel", "arbitrary")),
        name="proj",
    )(xb, w)


def _mm_res_ln_kernel(*refs, nparts, alpha):
    a_refs = refs[:nparts]
    w_refs = refs[nparts:2 * nparts]
    x32_ref, g_ref, b_ref, o32_ref, ob_ref = refs[2 * nparts:]
    y = alpha * x32_ref[...]
    for a_ref, w_ref in zip(a_refs, w_refs):
        y = y + jnp.dot(a_ref[...], w_ref[...], preferred_element_type=F32)
    z = _layer_norm(y, g_ref[...], b_ref[...])
    o32_ref[...] = z
    ob_ref[...] = z.astype(BF16)


def _mm_res_ln(parts, w, layer, x32, g, b, alpha, tm):
    m, d = x32.shape
    kp = parts[0].shape[1]
    assert all(p.shape[1] == kp for p in parts)
    in_specs = [pl.BlockSpec((tm, kp), lambda i: (i, 0)) for _ in parts]
    in_specs += [pl.BlockSpec((None, kp, d), lambda i, r=r: (layer, r, 0)) for r in range(len(parts))]
    in_specs += [
        pl.BlockSpec((tm, d), lambda i: (i, 0)),
        pl.BlockSpec((1, d), lambda i: (0, 0)),
        pl.BlockSpec((1, d), lambda i: (0, 0)),
    ]
    return pl.pallas_call(
        functools.partial(_mm_res_ln_kernel, nparts=len(parts), alpha=alpha),
        grid=(m // tm,),
        in_specs=in_specs,
        out_specs=[pl.BlockSpec((tm, d), lambda i: (i, 0)), pl.BlockSpec((tm, d), lambda i: (i, 0))],
        out_shape=[jax.ShapeDtypeStruct((m, d), F32), jax.ShapeDtypeStruct((m, d), BF16)],
        compiler_params=_cparams(("parallel",)),
        name="out_proj_ln",
    )(*parts, *([w] * len(parts)), x32, g, b)


def _glu_kernel(y_ref, w_ref, b_ref, o_ref):
    y = y_ref[...]
    z = jnp.dot(y.astype(BF16), w_ref[...], preferred_element_type=F32) + b_ref[...]
    o_ref[...] = (y * jax.nn.sigmoid(z)).astype(o_ref.dtype)


def _glu(y, w, layer, b, tm):
    m, n = y.shape
    return pl.pallas_call(
        _glu_kernel,
        grid=(m // tm,),
        in_specs=[
            pl.BlockSpec((tm, n), lambda i: (i, 0)),
            pl.BlockSpec((None, n, n), lambda i: (layer, 0, 0)),
            pl.BlockSpec((1, n), lambda i: (0, 0)),
        ],
        out_specs=pl.BlockSpec((tm, n), lambda i: (i, 0)),
        out_shape=jax.ShapeDtypeStruct((m, n), BF16),
        compiler_params=_cparams(("parallel",)),
        name="s5_glu",
    )(y, w, b)


CONV_PAD = 8


def _conv_kernel(bg_ref, cg_ref, v_ref, st_ref, w_ref, c_ref, ns_ref, hbuf, *, tt, nt, cw):
    t = pl.program_id(2)
    lo = CONV_PAD - (cw - 1)

    @pl.when(t == 0)
    def _():
        hbuf[lo:CONV_PAD, :] = st_ref[...]

    hbuf[CONV_PAD:CONV_PAD + tt, :] = cg_ref[...] * v_ref[...]
    w = w_ref[...]
    conv = w[0:1, :] * hbuf[lo:lo + tt, :]
    for j in range(1, cw):
        conv = conv + w[j:j + 1, :] * hbuf[lo + j:lo + j + tt, :]
    c_ref[...] = (bg_ref[...] * conv).astype(c_ref.dtype)
    tail = hbuf[lo + tt:CONV_PAD + tt, :]

    @pl.when(t == nt - 1)
    def _():
        ns_ref[...] = tail

    hbuf[lo:CONV_PAD, :] = tail


def _conv_core(z, state, conv_w, layer, bsz, t_len, tt, tn):
    m, d3 = z.shape
    d = d3 // 3
    cw = conv_w.shape[1]
    nt, nj = t_len // tt, d // tn
    return pl.pallas_call(
        functools.partial(_conv_kernel, tt=tt, nt=nt, cw=cw),
        grid=(bsz, nj, nt),
        in_specs=[
            pl.BlockSpec((tt, tn), lambda b, j, t: (b * nt + t, j)),
            pl.BlockSpec((tt, tn), lambda b, j, t: (b * nt + t, j + nj)),
            pl.BlockSpec((tt, tn), lambda b, j, t: (b * nt + t, j + 2 * nj)),
            pl.BlockSpec((None, cw - 1, tn), lambda b, j, t: (b, 0, j)),
            pl.BlockSpec((None, cw, tn), lambda b, j, t: (layer, 0, j)),
        ],
        out_specs=[
            pl.BlockSpec((tt, tn), lambda b, j, t: (b * nt + t, j)),
            pl.BlockSpec((None, cw - 1, tn), lambda b, j, t: (b, 0, j)),
        ],
        out_shape=[jax.ShapeDtypeStruct((m, d), BF16), jax.ShapeDtypeStruct((bsz, cw - 1, d), F32)],
        scratch_shapes=[pltpu.VMEM((CONV_PAD + tt, tn), F32)],
        compiler_params=_cparams(("parallel", "parallel", "arbitrary")),
        name="conv_core",
    )(z, z, z, state, conv_w)


def _moba_select(gate, n_valid):
    rows, nb = gate.shape
    lane = lax.broadcasted_iota(jnp.int32, (rows, nb), 1)
    valid = lane < n_valid
    gate = jnp.where(valid, gate, NEG_INF)
    beaten_by = jnp.zeros((rows, nb), jnp.int32)
    for jp in range(nb):
        col = gate[:, jp:jp + 1]
        beats = (col > gate) | ((col == gate) & (lane > jp))
        beaten_by = beaten_by + jnp.where(beats, 1, 0)
    return valid & (beaten_by < MOBA_TOPK)


def _column(mat_f32, lane, j):
    return jnp.sum(jnp.where(lane == j, mat_f32, 0.0), axis=1, keepdims=True)


def _moba_select_rows(gate_t, n_valid):
    nb, nq = gate_t.shape
    sub = lax.broadcasted_iota(jnp.int32, (nb, nq), 0)
    valid = sub < n_valid
    gate_t = jnp.where(valid, gate_t, NEG_INF)
    beaten_by = jnp.zeros((nb, nq), jnp.int32)
    for jp in range(nb):
        row = gate_t[jp:jp + 1, :]
        beats = (row > gate_t) | ((row == gate_t) & (sub > jp))
        beaten_by = beaten_by + jnp.where(beats, 1, 0)
    return valid & (beaten_by < MOBA_TOPK)


def _moba_prompt_kernel(q_ref, k_ref, v_ref, o_ref, kb_scr, vt_scr, km_scr, *, nb, scale):
    blk = MOBA_BLOCK
    kb_scr[...] = k_ref[...].astype(BF16)
    for j in range(nb):
        rows = slice(j * blk, (j + 1) * blk)
        km_scr[j:j + 1, :] = jnp.mean(k_ref[rows, :], axis=0, keepdims=True)
        vt_scr[:, rows] = v_ref[rows, :].T.astype(BF16)
    causal = lax.broadcasted_iota(jnp.int32, (blk, blk), 0) <= lax.broadcasted_iota(jnp.int32, (blk, blk), 1)
    causal_bias = jnp.where(causal, 0.0, NEG_INF).astype(F32)

    for qi in range(nb):
        n_keys = (qi + 1) * blk
        q = q_ref[qi * blk:(qi + 1) * blk, :]
        s_all = lax.dot_general(kb_scr[0:n_keys, :], q.astype(BF16), (((1,), (1,)), ((), ())),
                                preferred_element_type=F32) * scale
        if qi > 0:
            gate_t = lax.dot_general(km_scr[...], q, (((1,), (1,)), ((), ())),
                                     precision=lax.Precision.HIGHEST, preferred_element_type=F32)
            sel_bias = jnp.where(_moba_select_rows(gate_t, qi), 0.0, NEG_INF).astype(F32)
        s_blocks = [s_all[j * blk:(j + 1) * blk, :] + sel_bias[j:j + 1, :] for j in range(qi)]
        s_blocks.append(s_all[qi * blk:n_keys, :] + causal_bias)
        top = s_blocks[0]
        for sb in s_blocks[1:]:
            top = jnp.maximum(top, sb)
        m = jnp.max(top, axis=0, keepdims=True)
        p_blocks = [jnp.exp(sb - m) for sb in s_blocks]
        tot = p_blocks[0]
        for pb in p_blocks[1:]:
            tot = tot + pb
        l = jnp.sum(tot, axis=0, keepdims=True)
        p_all = jnp.concatenate([pb.astype(BF16) for pb in p_blocks], axis=0)
        acc = jnp.dot(vt_scr[:, 0:n_keys], p_all, preferred_element_type=F32)
        o_ref[qi * blk:(qi + 1) * blk, :] = (acc / l).T.astype(o_ref.dtype)


def _moba_prompt(q, k, v, bsz, t_len, nh, hd):
    m = q.shape[0]
    assert t_len % MOBA_BLOCK == 0
    nb = t_len // MOBA_BLOCK
    seq_spec = pl.BlockSpec((t_len, hd), lambda b, h: (b, h))
    return pl.pallas_call(
        functools.partial(_moba_prompt_kernel, nb=nb, scale=hd ** -0.5),
        grid=(bsz, nh),
        in_specs=[seq_spec, seq_spec, seq_spec],
        out_specs=seq_spec,
        out_shape=jax.ShapeDtypeStruct((m, nh * hd), BF16),
        scratch_shapes=[pltpu.VMEM((t_len, hd), BF16), pltpu.VMEM((hd, t_len), BF16),
                        pltpu.VMEM((nb, hd), F32)],
        compiler_params=_cparams(("parallel", "parallel")),
        name="moba_prompt",
    )(q, k, v)


def _kmean_kernel(pt_ref, *refs, pages_per_block, blocks_per_step):
    del pt_ref
    page_refs, o_ref = refs[:-1], refs[-1]
    for mblk in range(blocks_per_step):
        tot = jnp.sum(page_refs[mblk * pages_per_block][...], axis=0)
        for r in range(1, pages_per_block):
            tot = tot + jnp.sum(page_refs[mblk * pages_per_block + r][...], axis=0)
        o_ref[mblk] = tot / MOBA_BLOCK


def _kmean_past(cache_k, layer, page_table, blocks_per_step=2):
    _, _, page, nh, hd = cache_k.shape
    db, n_pages = page_table.shape
    assert MOBA_BLOCK % page == 0
    ppb = MOBA_BLOCK // page
    pps = ppb * blocks_per_step
    nblk = n_pages // ppb
    assert n_pages % pps == 0

    def page_spec(r):
        return pl.BlockSpec((None, None, page, nh, hd), lambda b, s, pt: (layer, pt[b, s * pps + r], 0, 0, 0))

    return pl.pallas_call(
        functools.partial(_kmean_kernel, pages_per_block=ppb, blocks_per_step=blocks_per_step),
        grid_spec=pltpu.PrefetchScalarGridSpec(
            num_scalar_prefetch=1,
            grid=(db, n_pages // pps),
            in_specs=[page_spec(r) for r in range(pps)],
            out_specs=pl.BlockSpec((None, blocks_per_step, nh, hd), lambda b, s, pt: (b, s, 0, 0)),
        ),
        out_shape=jax.ShapeDtypeStruct((db, nblk, nh, hd), F32),
        compiler_params=_cparams(("parallel", "arbitrary")),
        name="kmean_past",
    )(page_table, *([cache_k] * pps))


def _sample_attn_kernel(pt_ref, q_ref, kn_ref, vn_ref, km_ref, *refs, nh, n_past_blocks, pages_per_block, scale):
    del pt_ref
    kp_refs = refs[:pages_per_block]
    vp_refs = refs[pages_per_block:2 * pages_per_block]
    o_ref, sel_scr, m_scr, l_scr, acc_scr = refs[2 * pages_per_block:]
    j = pl.program_id(1)
    rows = q_ref.shape[0]
    nh_shift = nh.bit_length() - 1
    qb = q_ref[...].astype(BF16)

    def head_match(ncols):
        r = lax.broadcasted_iota(jnp.int32, (rows, ncols), 0)
        c = lax.broadcasted_iota(jnp.int32, (rows, ncols), 1)
        return r, c, (r & (nh - 1)) == (c & (nh - 1))

    def scores(k):
        return lax.dot_general(qb, k.astype(BF16), (((1,), (1,)), ((), ())), preferred_element_type=F32) * scale

    @pl.when(j == 0)
    def _():
        km = km_ref[...]
        g_all = lax.dot_general(q_ref[...], km, (((1,), (1,)), ((), ())),
                                precision=lax.Precision.HIGHEST, preferred_element_type=F32)
        _, c, same = head_match(n_past_blocks * nh)
        g_all = jnp.where(same, g_all, 0.0)
        pick_r = lax.broadcasted_iota(jnp.int32, (n_past_blocks * nh, n_past_blocks), 0)
        pick_c = lax.broadcasted_iota(jnp.int32, (n_past_blocks * nh, n_past_blocks), 1)
        pick = jnp.where((pick_r >> nh_shift) == pick_c, 1.0, 0.0).astype(F32)
        gate = jnp.dot(g_all, pick, precision=lax.Precision.HIGHEST, preferred_element_type=F32)
        sel_scr[...] = _moba_select(gate, n_past_blocks).astype(F32)
        r, c, same = head_match(kn_ref.shape[0])
        s = jnp.where(same & ((c >> nh_shift) <= (r >> nh_shift)), scores(kn_ref[...]), NEG_INF)
        m = jnp.max(s, axis=1, keepdims=True)
        p = jnp.exp(s - m)
        m_scr[...] = m
        l_scr[...] = jnp.sum(p, axis=1, keepdims=True)
        acc_scr[...] = jnp.dot(p.astype(BF16), vn_ref[...].astype(BF16), preferred_element_type=F32)

    sel = sel_scr[...]
    lane = lax.broadcasted_iota(jnp.int32, sel.shape, 1)
    chosen = _column(sel, lane, j) > 0.5
    for r in range(pages_per_block):
        k = kp_refs[r][...]
        _, _, same = head_match(k.shape[0])
        s = jnp.where(same & chosen, scores(k), NEG_INF)
        m = m_scr[...]
        m_new = jnp.maximum(m, jnp.max(s, axis=1, keepdims=True))
        a = jnp.exp(m - m_new)
        p = jnp.exp(s - m_new)
        m_scr[...] = m_new
        l_scr[...] = a * l_scr[...] + jnp.sum(p, axis=1, keepdims=True)
        acc_scr[...] = a * acc_scr[...] + jnp.dot(p.astype(BF16), vp_refs[r][...].astype(BF16),
                                                  preferred_element_type=F32)

    @pl.when(j == n_past_blocks - 1)
    def _():
        o_ref[...] = (acc_scr[...] / l_scr[...]).astype(o_ref.dtype)


def _sample_attn(q, k_new, v_new, cache_k, cache_v, layer, page_table, db, t_len, nh, hd):
    n_hyb, n_pool, page, _, _ = cache_k.shape
    n_pages = page_table.shape[1]
    past_len = n_pages * page
    assert past_len % MOBA_BLOCK == 0 and t_len <= MOBA_BLOCK and MOBA_BLOCK % page == 0
    assert nh & (nh - 1) == 0, "head matching uses bit masks"
    ppb = MOBA_BLOCK // page
    n_past_blocks = past_len // MOBA_BLOCK
    rows = t_len * nh
    kmean = _kmean_past(cache_k, layer, page_table).reshape(db, n_past_blocks * nh, hd)
    ck = cache_k.reshape(n_hyb, n_pool, page * nh, hd)
    cv = cache_v.reshape(n_hyb, n_pool, page * nh, hd)

    def page_spec(r):
        return pl.BlockSpec((None, None, page * nh, hd), lambda b, j, pt: (layer, pt[b, j * ppb + r], 0, 0))

    def row_spec(nrows):
        return pl.BlockSpec((None, nrows, hd), lambda b, j, pt: (b, 0, 0))

    out = pl.pallas_call(
        functools.partial(_sample_attn_kernel, nh=nh, n_past_blocks=n_past_blocks, pages_per_block=ppb,
                          scale=hd ** -0.5),
        grid_spec=pltpu.PrefetchScalarGridSpec(
            num_scalar_prefetch=1,
            grid=(db, n_past_blocks),
            in_specs=[row_spec(rows), row_spec(rows), row_spec(rows), row_spec(n_past_blocks * nh)]
            + [page_spec(r) for r in range(ppb)] * 2,
            out_specs=row_spec(rows),
            scratch_shapes=[pltpu.VMEM((rows, n_past_blocks), F32), pltpu.VMEM((rows, 1), F32),
                            pltpu.VMEM((rows, 1), F32), pltpu.VMEM((rows, hd), F32)],
        ),
        out_shape=jax.ShapeDtypeStruct((db, rows, hd), BF16),
        compiler_params=_cparams(("parallel", "arbitrary")),
        name="sample_attn",
    )(page_table, q.reshape(db, rows, hd), k_new.reshape(db, rows, hd), v_new.reshape(db, rows, hd), kmean,
      *([ck] * ppb), *([cv] * ppb))
    return out.reshape(db * t_len, nh * hd)


def _s5_kernel(u_ref, bre_ref, bim_ref, cre_ref, cimn_ref, lre_ref, lim_ref, llre_ref, llim_ref, d_ref,
               h0re_ref, h0im_ref, y_ref, htre_ref, htim_ref, hr_scr, hi_scr, cre_scr, cim_scr,
               *, seg_len, chained, n_chunks):
    c = pl.program_id(1)
    nseg = SUBLANES
    nstate = hr_scr.shape[1]
    n_kb = bre_ref.shape[0]
    ku = bre_ref.shape[1]
    ks = bre_ref.shape[2]

    for kb in range(n_kb):
        ukb = u_ref[:, kb * ku:(kb + 1) * ku].astype(BF16)
        hr_scr[:, kb * ks:(kb + 1) * ks] = jnp.dot(ukb, bre_ref[kb], preferred_element_type=F32)
        hi_scr[:, kb * ks:(kb + 1) * ks] = jnp.dot(ukb, bim_ref[kb], preferred_element_type=F32)

    if chained:
        @pl.when(c == 0)
        def _():
            cre_scr[...] = jnp.zeros_like(cre_scr)
            cim_scr[...] = jnp.zeros_like(cim_scr)
            cre_scr[0:1, :] = h0re_ref[...]
            cim_scr[0:1, :] = h0im_ref[...]

    for lc in range(nstate // S5_SCAN_LANES):
        cols = slice(lc * S5_SCAN_LANES, (lc + 1) * S5_SCAN_LANES)
        lr = jnp.broadcast_to(lre_ref[:, cols], (nseg, S5_SCAN_LANES))
        li = jnp.broadcast_to(lim_ref[:, cols], (nseg, S5_SCAN_LANES))

        def advance(i, hr, hi):
            r = pl.multiple_of(i * nseg, nseg)
            nr = lr * hr - li * hi + hr_scr[pl.ds(r, nseg), cols]
            ni = lr * hi + li * hr + hi_scr[pl.ds(r, nseg), cols]
            return r, nr, ni

        def scan_only(i, carry):
            _, nr, ni = advance(i, *carry)
            return nr, ni

        def scan_store(i, carry):
            r, nr, ni = advance(i, *carry)
            hr_scr[pl.ds(r, nseg), cols] = nr
            hi_scr[pl.ds(r, nseg), cols] = ni
            return nr, ni

        if chained:
            er, ei = lax.fori_loop(0, seg_len, scan_only, (cre_scr[:, cols], cim_scr[:, cols]),
                                   unroll=S5_SCAN_UNROLL)
            llr, lli = llre_ref[:, cols], llim_ref[:, cols]
            fr, fi = er[0:1, :], ei[0:1, :]
            for j in range(1, nseg):
                cre_scr[j:j + 1, cols] = fr
                cim_scr[j:j + 1, cols] = fi
                if j < nseg - 1:
                    fr, fi = (er[j:j + 1, :] + (llr * fr - lli * fi), ei[j:j + 1, :] + (llr * fi + lli * fr))
            init = (cre_scr[:, cols], cim_scr[:, cols])
        else:
            init = (h0re_ref[:, cols], h0im_ref[:, cols])

        hr, hi = lax.fori_loop(0, seg_len, scan_store, init, unroll=S5_SCAN_UNROLL)

        if chained:
            last_r, last_i = hr[nseg - 1:nseg, :], hi[nseg - 1:nseg, :]
            cre_scr[:, cols] = jnp.zeros((nseg, S5_SCAN_LANES), F32)
            cim_scr[:, cols] = jnp.zeros((nseg, S5_SCAN_LANES), F32)
            cre_scr[0:1, cols] = last_r
            cim_scr[0:1, cols] = last_i

            @pl.when(c == n_chunks - 1)
            def _():
                htre_ref[:, cols] = last_r
                htim_ref[:, cols] = last_i
        else:
            htre_ref[:, cols] = hr
            htim_ref[:, cols] = hi

    for kb in range(n_kb):
        hre = hr_scr[:, kb * ks:(kb + 1) * ks].astype(BF16)
        him = hi_scr[:, kb * ks:(kb + 1) * ks].astype(BF16)
        ykb = (jnp.dot(hre, cre_ref[kb], preferred_element_type=F32)
               + jnp.dot(him, cimn_ref[kb], preferred_element_type=F32))
        cols = slice(kb * ku, (kb + 1) * ku)
        y_ref[:, cols] = jax.nn.gelu(ykb + d_ref[:, cols] * u_ref[:, cols])


def _s5_discretise(a_re, a_im, log_dt, b_re, b_im, c_re, c_im, d_skip, seg_len):
    g, p = a_re.shape
    ch = b_re.shape[-1]
    lam = lax.complex(a_re.astype(F32), a_im.astype(F32))
    dt = jnp.exp(log_dt.astype(F32))[:, None]
    lam_bar = jnp.exp(lam * dt)
    b_bar = ((lam_bar - 1.0) / lam)[..., None] * lax.complex(b_re.astype(F32), b_im.astype(F32))
    lam_seg = lam_bar
    assert seg_len & (seg_len - 1) == 0
    for _ in range(seg_len.bit_length() - 1):
        lam_seg = lam_seg * lam_seg
    gl = S5_LANE_GROUPS
    n_kb = g // gl
    eye = jnp.eye(gl, dtype=F32)

    def pack_b(x):
        x = x.reshape(n_kb, gl, p, ch).transpose(0, 1, 3, 2)
        return jnp.einsum("kgcp,gh->kgchp", x, eye).reshape(n_kb, gl * ch, gl * p)

    def pack_c(x):
        x = x.reshape(n_kb, gl, ch, p).transpose(0, 1, 3, 2)
        return jnp.einsum("kgpc,gh->kgphc", x, eye).reshape(n_kb, gl * p, gl * ch)

    flat = lambda x: x.reshape(1, g * p)
    return dict(
        bre=pack_b(jnp.real(b_bar)).astype(BF16), bim=pack_b(jnp.imag(b_bar)).astype(BF16),
        cre=pack_c(c_re.astype(F32)).astype(BF16), cimn=pack_c(-c_im.astype(F32)).astype(BF16),
        lre=flat(jnp.real(lam_bar)), lim=flat(jnp.imag(lam_bar)),
        llre=flat(jnp.real(lam_seg)), llim=flat(jnp.imag(lam_seg)),
        d=d_skip.astype(F32).reshape(1, g * ch),
    )


def _s5(u, h0_re, h0_im, prm, seg_len, chained):
    m, w = u.shape
    gp = prm["lre"].shape[1]
    n_kb, ku, ks = prm["bre"].shape
    chunk = SUBLANES * seg_len
    if chained:
        nseq = h0_re.shape[0]
        n_chunks = m // nseq // chunk
        grid = (nseq, n_chunks)
        u_spec = pl.BlockSpec((chunk, w), lambda b, c: (b * n_chunks + c, 0))
        h_spec = pl.BlockSpec((None, 1, gp), lambda b, c: (b, 0, 0))
        h_shape = jax.ShapeDtypeStruct((nseq, 1, gp), F32)
    else:
        assert m == chunk
        n_chunks = 1
        grid = (1, 1)
        u_spec = pl.BlockSpec((chunk, w), lambda b, c: (0, 0))
        h_spec = pl.BlockSpec((SUBLANES, gp), lambda b, c: (0, 0))
        h_shape = jax.ShapeDtypeStruct((SUBLANES, gp), F32)
    full = lambda a: pl.BlockSpec(a.shape, lambda b, c: (0,) * a.ndim)
    names = ("bre", "bim", "cre", "cimn", "lre", "lim", "llre", "llim", "d")

    def swap_rows(a, n_outer, n_inner):
        return a.reshape(m // chunk, n_outer, n_inner, w).transpose(0, 2, 1, 3).reshape(m, w)

    y, ht_re, ht_im = pl.pallas_call(
        functools.partial(_s5_kernel, seg_len=seg_len, chained=chained, n_chunks=n_chunks),
        grid=grid,
        in_specs=[u_spec] + [full(prm[n]) for n in names] + [h_spec, h_spec],
        out_specs=[u_spec, h_spec, h_spec],
        out_shape=[jax.ShapeDtypeStruct((m, w), F32), h_shape, h_shape],
        scratch_shapes=[pltpu.VMEM((chunk, gp), F32), pltpu.VMEM((chunk, gp), F32),
                        pltpu.VMEM((SUBLANES, gp), F32), pltpu.VMEM((SUBLANES, gp), F32)],
        compiler_params=_cparams(("parallel", "arbitrary")),
        name="s5_scan",
    )(swap_rows(u, SUBLANES, seg_len), *[prm[n] for n in names], h0_re, h0_im)
    return swap_rows(y, seg_len, SUBLANES), ht_re, ht_im


def _pick_tile(n, target):
    t = min(n, target)
    while n % t:
        t //= 2
    return t


def _run_trunk(x, prm, s5_prms, kv, s5_re0, s5_im0, conv0, chained):
    bsz, t_len, d = x.shape
    m = bsz * t_len
    depth = prm["ffn1_wgu"].shape[0]
    alpha = (2.0 * depth) ** 0.25
    nh, hd = kv["nh"], kv["hd"]
    moba_w = nh * hd
    s5_w = d - moba_w
    g_cnt, p_cnt = prm["s5_a_re"].shape[1:]
    d_ff = prm["ffn1_wd"].shape[1]
    tm = _pick_tile(m, 1024)
    tf = _pick_tile(d_ff, 512)
    tm_small = _pick_tile(m, 512)

    x32 = x.reshape(m, d).astype(F32)
    xb = x32.astype(BF16)
    row = lambda a: a.reshape(1, -1).astype(F32)
    ks, vs, hrs, his, cvs = [], [], [], [], []
    for layer in range(depth):
        i = layer // 2
        g, b = prm["ln_g"][layer], prm["ln_b"][layer]
        x32, xb = _ffn(x32, xb, prm["ffn1_wgu"], prm["ffn1_wd"], layer, row(g[0]), row(b[0]), alpha, tm, tf)
        if layer % 2 == 0:
            w_in = prm["hyb_w_in"]
            tn = _pick_tile(moba_w, 1024)
            q = _mm(xb, w_in, i, 0, moba_w, tm, tn)
            k = _mm(xb, w_in, i, moba_w, moba_w, tm, tn)
            v = _mm(xb, w_in, i, 2 * moba_w, moba_w, tm, tn)
            u = _mm(xb, w_in, i, 3 * moba_w, s5_w, tm, tn)
            if chained:
                attn = _moba_prompt(q, k, v, bsz, t_len, nh, hd)
            else:
                attn = _sample_attn(q, k, v, kv["cache_k"], kv["cache_v"], i, kv["page_table"], bsz, t_len, nh, hd)
            sp = s5_prms[i]
            if chained:
                y, h_re, h_im = _s5(u, s5_re0[i].reshape(bsz, 1, g_cnt * p_cnt),
                                    s5_im0[i].reshape(bsz, 1, g_cnt * p_cnt), sp, sp["seg_len"], True)
            else:
                y, h_re, h_im = _s5(u, s5_re0[i].reshape(bsz, g_cnt * p_cnt), s5_im0[i].reshape(bsz, g_cnt * p_cnt),
                                    sp, sp["seg_len"], False)
            ssm = _glu(y, prm["s5_glu_w"], i, row(prm["s5_glu_b"][i]), tm_small)
            x32, xb = _mm_res_ln([attn, ssm], prm["hyb_w_out"], i, x32, row(g[1]), row(b[1]), alpha, tm_small)
            ks.append(k.reshape(bsz, t_len, nh, hd))
            vs.append(v.reshape(bsz, t_len, nh, hd))
            hrs.append(h_re.reshape(bsz, g_cnt, p_cnt))
            his.append(h_im.reshape(bsz, g_cnt, p_cnt))
        else:
            z = _mm(xb, prm["conv_w_in"], i, 0, 3 * d, tm, _pick_tile(d, 2048))
            tt = _pick_tile(t_len, 512)
            c, c_new = _conv_core(z, conv0[i], prm["conv_w"], i, bsz, t_len, tt, _pick_tile(d, 512))
            x32, xb = _mm_res_ln([c], prm["conv_w_out"], i, x32, row(g[1]), row(b[1]), alpha, tm_small)
            cvs.append(c_new)
        x32, xb = _ffn(x32, xb, prm["ffn2_wgu"], prm["ffn2_wd"], layer, row(g[2]), row(b[2]), alpha, tm, tf)
    return (x32.reshape(bsz, t_len, d), jnp.stack(ks), jnp.stack(vs), jnp.stack(hrs), jnp.stack(his),
            jnp.stack(cvs))


def kernel(x_prompt, x_sample, cache_k, cache_v, state_s5_re, state_s5_im, state_conv, page_table, ffn1_wgu, ffn1_wd, ffn2_wgu, ffn2_wd, ln_g, ln_b, hyb_w_in, hyb_w_out, s5_a_re, s5_a_im, s5_log_dt, s5_b_re, s5_b_im, s5_c_re, s5_c_im, s5_d, s5_glu_w, s5_glu_b, conv_w_in, conv_w, conv_w_out):
    n_hyb, _, _, nh, hd = cache_k.shape
    bp, t_prompt, d = x_prompt.shape
    db, t_sample, _ = x_sample.shape
    n_conv, cw = conv_w.shape[:2]
    g_cnt, p_cnt = s5_a_re.shape[1:]
    assert db == SUBLANES, "the sample scan lays the decode batch along the vreg sublanes"
    assert (s5_b_re.shape[-1] * S5_LANE_GROUPS) == 128

    prm = dict(
        ffn1_wgu=ffn1_wgu.astype(BF16), ffn1_wd=ffn1_wd.astype(BF16),
        ffn2_wgu=ffn2_wgu.astype(BF16), ffn2_wd=ffn2_wd.astype(BF16),
        ln_g=ln_g, ln_b=ln_b, hyb_w_in=hyb_w_in.astype(BF16), hyb_w_out=hyb_w_out.astype(BF16),
        s5_a_re=s5_a_re, s5_glu_w=s5_glu_w.astype(BF16), s5_glu_b=s5_glu_b,
        conv_w_in=conv_w_in.astype(BF16), conv_w=conv_w.astype(F32), conv_w_out=conv_w_out.astype(BF16),
    )

    def s5_prms(seg_len):
        out = []
        for i in range(n_hyb):
            sp = _s5_discretise(s5_a_re[i], s5_a_im[i], s5_log_dt[i], s5_b_re[i], s5_b_im[i], s5_c_re[i],
                                s5_c_im[i], s5_d[i], seg_len)
            sp["seg_len"] = seg_len
            out.append(sp)
        return out

    prompt_seg = _pick_tile(t_prompt // SUBLANES, 32)
    kv_none = dict(nh=nh, hd=hd)
    s5_zero = jnp.zeros((n_hyb, bp, g_cnt, p_cnt), F32)
    conv_zero = jnp.zeros((n_conv, bp, cw - 1, d), x_prompt.dtype)
    y_p, k_p, v_p, hr_p, hi_p, cv_p = _run_trunk(x_prompt, prm, s5_prms(prompt_seg), kv_none, s5_zero, s5_zero,
                                                 conv_zero, True)
    kv = dict(nh=nh, hd=hd, cache_k=cache_k, cache_v=cache_v, page_table=page_table)
    y_s, k_s, v_s, hr_s, hi_s, cv_s = _run_trunk(x_sample, prm, s5_prms(t_sample), kv, state_s5_re, state_s5_im,
                                                 state_conv, False)
    return (y_p, y_s, k_p, v_p, hr_p, hi_p, cv_p, k_s, v_s, hr_s, hi_s, cv_s)
```

```python
import functools

import jax
import jax.numpy as jnp
from jax import lax
from jax.experimental import pallas as pl
from jax.experimental.pallas import tpu as pltpu

MOBA_BLOCK = 256
MOBA_TOPK = 3
LN_EPS = 1e-5
LANES = 128
SUBLANES = 8
S5_LANE_GROUPS = 8
S5_SCAN_LANES = 512
S5_SCAN_UNROLL = 4
VMEM_LIMIT_BYTES = 56 * 1024 * 1024

F32 = jnp.float32
BF16 = jnp.bfloat16
NEG_INF = float("-inf")


def _cparams(semantics):
    return pltpu.CompilerParams(dimension_semantics=semantics, vmem_limit_bytes=VMEM_LIMIT_BYTES)


def _layer_norm(y, g, b):
    mu = jnp.mean(y, axis=-1, keepdims=True)
    d = y - mu
    var = jnp.mean(d * d, axis=-1, keepdims=True)
    return d * lax.rsqrt(var + LN_EPS) * g + b


def _ffn_kernel(x32_ref, xb_ref, wg_ref, wu_ref, wd_ref, g_ref, b_ref, o32_ref, ob_ref, acc_ref, *, alpha, nf):
    j = pl.program_id(1)

    @pl.when(j == 0)
    def _():
        acc_ref[...] = jnp.zeros_like(acc_ref)

    xb = xb_ref[...]
    gate = jnp.dot(xb, wg_ref[...], preferred_element_type=F32)
    up = jnp.dot(xb, wu_ref[...], preferred_element_type=F32)
    h = (gate * jax.nn.sigmoid(gate)) * up
    acc_ref[...] += jnp.dot(h.astype(BF16), wd_ref[...], preferred_element_type=F32)

    @pl.when(j == nf - 1)
    def _():
        z = _layer_norm(alpha * x32_ref[...] + 0.5 * acc_ref[...], g_ref[...], b_ref[...])
        o32_ref[...] = z
        ob_ref[...] = z.astype(BF16)


def _ffn(x32, xb, wgu, wd, layer, g, b, alpha, tm, tf):
    m, d = x32.shape
    f = wd.shape[1]
    nf = f // tf
    return pl.pallas_call(
        functools.partial(_ffn_kernel, alpha=alpha, nf=nf),
        grid=(m // tm, nf),
        in_specs=[
            pl.BlockSpec((tm, d), lambda i, j: (i, 0)),
            pl.BlockSpec((tm, d), lambda i, j: (i, 0)),
            pl.BlockSpec((None, d, tf), lambda i, j: (layer, 0, j)),
            pl.BlockSpec((None, d, tf), lambda i, j: (layer, 0, j + nf)),
            pl.BlockSpec((None, tf, d), lambda i, j: (layer, j, 0)),
            pl.BlockSpec((1, d), lambda i, j: (0, 0)),
            pl.BlockSpec((1, d), lambda i, j: (0, 0)),
        ],
        out_specs=[pl.BlockSpec((tm, d), lambda i, j: (i, 0)), pl.BlockSpec((tm, d), lambda i, j: (i, 0))],
        out_shape=[jax.ShapeDtypeStruct((m, d), F32), jax.ShapeDtypeStruct((m, d), BF16)],
        scratch_shapes=[pltpu.VMEM((tm, d), F32)],
        compiler_params=_cparams(("parallel", "arbitrary")),
        name="ffn_ln",
    )(x32, xb, wgu, wgu, wd, g, b)


def _mm_kernel(x_ref, w_ref, o_ref):
    o_ref[...] = jnp.dot(x_ref[...], w_ref[...], preferred_element_type=F32)


def _mm(xb, w, layer, col0, n, tm, tn):
    m, k = xb.shape
    assert col0 % tn == 0 and n % tn == 0
    jb0 = col0 // tn
    return pl.pallas_call(
        _mm_kernel,
        grid=(m // tm, n // tn),
        in_specs=[
            pl.BlockSpec((tm, k), lambda i, j: (i, 0)),
            pl.BlockSpec((None, k, tn), lambda i, j: (layer, 0, jb0 + j)),
        ],
        out_specs=pl.BlockSpec((tm, tn), lambda i, j: (i, j)),
        out_shape=jax.ShapeDtypeStruct((m, n), F32),
        compiler_params=_cparams(("parallel", "arbitrary")),
        name="proj",
    )(xb, w)


def _mm_split_kernel(x_ref, w_ref, *o_refs, per_step):
    j = pl.program_id(1)
    y = jnp.dot(x_ref[...], w_ref[...], preferred_element_type=F32)
    width = y.shape[1] // per_step
    for jj in range(len(o_refs) // per_step):
        @pl.when(j == jj)
        def _():
            for r in range(per_step):
                o_refs[jj * per_step + r][...] = y[:, r * width:(r + 1) * width]


def _mm_split(xb, w, layer, n_out, tm, per_step):
    m, k = xb.shape
    width = w.shape[2] // n_out
    tn = per_step * width
    out_spec = pl.BlockSpec((tm, width), lambda i, j: (i, 0))
    return pl.pallas_call(
        functools.partial(_mm_split_kernel, per_step=per_step),
        grid=(m // tm, n_out // per_step),
        in_specs=[
            pl.BlockSpec((tm, k), lambda i, j: (i, 0)),
            pl.BlockSpec((None, k, tn), lambda i, j: (layer, 0, j)),
        ],
        out_specs=[out_spec] * n_out,
        out_shape=[jax.ShapeDtypeStruct((m, width), F32)] * n_out,
        compiler_params=_cparams(("parallel", "arbitrary")),
        name="proj_split",
    )(xb, w)


def _mm_res_ln_kernel(*refs, nparts, alpha):
    a_refs = refs[:nparts]
    w_refs = refs[nparts:2 * nparts]
    x32_ref, g_ref, b_ref, o32_ref, ob_ref = refs[2 * nparts:]
    y = alpha * x32_ref[...]
    for a_ref, w_ref in zip(a_refs, w_refs):
        y = y + jnp.dot(a_ref[...], w_ref[...], preferred_element_type=F32)
    z = _layer_norm(y, g_ref[...], b_ref[...])
    o32_ref[...] = z
    ob_ref[...] = z.astype(BF16)


def _mm_res_ln(parts, w, layer, x32, g, b, alpha, tm):
    m, d = x32.shape
    kp = parts[0].shape[1]
    assert all(p.shape[1] == kp for p in parts)
    in_specs = [pl.BlockSpec((tm, kp), lambda i: (i, 0)) for _ in parts]
    in_specs += [pl.BlockSpec((None, kp, d), lambda i, r=r: (layer, r, 0)) for r in range(len(parts))]
    in_specs += [
        pl.BlockSpec((tm, d), lambda i: (i, 0)),
        pl.BlockSpec((1, d), lambda i: (0, 0)),
        pl.BlockSpec((1, d), lambda i: (0, 0)),
    ]
    return pl.pallas_call(
        functools.partial(_mm_res_ln_kernel, nparts=len(parts), alpha=alpha),
        grid=(m // tm,),
        in_specs=in_specs,
        out_specs=[pl.BlockSpec((tm, d), lambda i: (i, 0)), pl.BlockSpec((tm, d), lambda i: (i, 0))],
        out_shape=[jax.ShapeDtypeStruct((m, d), F32), jax.ShapeDtypeStruct((m, d), BF16)],
        compiler_params=_cparams(("parallel",)),
        name="out_proj_ln",
    )(*parts, *([w] * len(parts)), x32, g, b)


def _glu_kernel(y_ref, w_ref, b_ref, o_ref):
    y = y_ref[...]
    z = jnp.dot(y.astype(BF16), w_ref[...], preferred_element_type=F32) + b_ref[...]
    o_ref[...] = (y * jax.nn.sigmoid(z)).astype(o_ref.dtype)


def _glu(y, w, layer, b, tm):
    m, n = y.shape
    return pl.pallas_call(
        _glu_kernel,
        grid=(m // tm,),
        in_specs=[
            pl.BlockSpec((tm, n), lambda i: (i, 0)),
            pl.BlockSpec((None, n, n), lambda i: (layer, 0, 0)),
            pl.BlockSpec((1, n), lambda i: (0, 0)),
        ],
        out_specs=pl.BlockSpec((tm, n), lambda i: (i, 0)),
        out_shape=jax.ShapeDtypeStruct((m, n), BF16),
        compiler_params=_cparams(("parallel",)),
        name="s5_glu",
    )(y, w, b)


CONV_PAD = 8


def _conv_tile(bg, cg, v, st_ref, w_ref, c_ref, ns_ref, hbuf, *, tt, nt, cw):
    t = pl.program_id(2)
    lo = CONV_PAD - (cw - 1)

    @pl.when(t == 0)
    def _():
        hbuf[lo:CONV_PAD, :] = st_ref[...]

    hbuf[CONV_PAD:CONV_PAD + tt, :] = cg * v
    w = w_ref[...]
    conv = w[0:1, :] * hbuf[lo:lo + tt, :]
    for j in range(1, cw):
        conv = conv + w[j:j + 1, :] * hbuf[lo + j:lo + j + tt, :]
    c_ref[...] = (bg * conv).astype(c_ref.dtype)
    tail = hbuf[lo + tt:CONV_PAD + tt, :]

    @pl.when(t == nt - 1)
    def _():
        ns_ref[...] = tail

    hbuf[lo:CONV_PAD, :] = tail


def _conv_kernel(bg_ref, cg_ref, v_ref, st_ref, w_ref, c_ref, ns_ref, hbuf, **kw):
    _conv_tile(bg_ref[...], cg_ref[...], v_ref[...], st_ref, w_ref, c_ref, ns_ref, hbuf, **kw)


def _conv_in_kernel(x_ref, wb_ref, wc_ref, wv_ref, st_ref, w_ref, c_ref, ns_ref, hbuf, **kw):
    x = x_ref[...]
    bg, cg, v = (jnp.dot(x, w[...], preferred_element_type=F32) for w in (wb_ref, wc_ref, wv_ref))
    _conv_tile(bg, cg, v, st_ref, w_ref, c_ref, ns_ref, hbuf, **kw)


def _conv_core(z, state, conv_w, layer, bsz, t_len, tt, tn):
    m, d3 = z.shape
    d = d3 // 3
    cw = conv_w.shape[1]
    nt, nj = t_len // tt, d // tn
    return pl.pallas_call(
        functools.partial(_conv_kernel, tt=tt, nt=nt, cw=cw),
        grid=(bsz, nj, nt),
        in_specs=[
            pl.BlockSpec((tt, tn), lambda b, j, t: (b * nt + t, j)),
            pl.BlockSpec((tt, tn), lambda b, j, t: (b * nt + t, j + nj)),
            pl.BlockSpec((tt, tn), lambda b, j, t: (b * nt + t, j + 2 * nj)),
            pl.BlockSpec((None, cw - 1, tn), lambda b, j, t: (b, 0, j)),
            pl.BlockSpec((None, cw, tn), lambda b, j, t: (layer, 0, j)),
        ],
        out_specs=[
            pl.BlockSpec((tt, tn), lambda b, j, t: (b * nt + t, j)),
            pl.BlockSpec((None, cw - 1, tn), lambda b, j, t: (b, 0, j)),
        ],
        out_shape=[jax.ShapeDtypeStruct((m, d), BF16), jax.ShapeDtypeStruct((bsz, cw - 1, d), F32)],
        scratch_shapes=[pltpu.VMEM((CONV_PAD + tt, tn), F32)],
        compiler_params=_cparams(("parallel", "parallel", "arbitrary")),
        name="conv_core",
    )(z, z, z, state, conv_w)


def _conv_in_core(xb, w_in, state, conv_w, layer, bsz, t_len, tt, tn):
    m, d = xb.shape
    cw = conv_w.shape[1]
    nt, nj = t_len // tt, d // tn
    return pl.pallas_call(
        functools.partial(_conv_in_kernel, tt=tt, nt=nt, cw=cw),
        grid=(bsz, nj, nt),
        in_specs=[
            pl.BlockSpec((tt, d), lambda b, j, t: (b * nt + t, 0)),
            pl.BlockSpec((None, d, tn), lambda b, j, t: (layer, 0, j)),
            pl.BlockSpec((None, d, tn), lambda b, j, t: (layer, 0, j + nj)),
            pl.BlockSpec((None, d, tn), lambda b, j, t: (layer, 0, j + 2 * nj)),
            pl.BlockSpec((None, cw - 1, tn), lambda b, j, t: (b, 0, j)),
            pl.BlockSpec((None, cw, tn), lambda b, j, t: (layer, 0, j)),
        ],
        out_specs=[
            pl.BlockSpec((tt, tn), lambda b, j, t: (b * nt + t, j)),
            pl.BlockSpec((None, cw - 1, tn), lambda b, j, t: (b, 0, j)),
        ],
        out_shape=[jax.ShapeDtypeStruct((m, d), BF16), jax.ShapeDtypeStruct((bsz, cw - 1, d), F32)],
        scratch_shapes=[pltpu.VMEM((CONV_PAD + tt, tn), F32)],
        compiler_params=_cparams(("parallel", "parallel", "arbitrary")),
        name="conv_in_core",
    )(xb, w_in, w_in, w_in, state, conv_w)


def _moba_select_rows(gate_t, n_valid):
    nb, nq = gate_t.shape
    sub = lax.broadcasted_iota(jnp.int32, (nb, nq), 0)
    valid = sub < n_valid
    gate_t = jnp.where(valid, gate_t, NEG_INF)
    beaten_by = jnp.zeros((nb, nq), jnp.int32)
    for jp in range(nb):
        row = gate_t[jp:jp + 1, :]
        beats = (row > gate_t) | ((row == gate_t) & (sub > jp))
        beaten_by = beaten_by + jnp.where(beats, 1, 0)
    return valid & (beaten_by < MOBA_TOPK)


def _moba_prompt_kernel(q_ref, k_ref, v_ref, o_ref, kb_scr, vt_scr, km_scr, *, nb, scale):
    blk = MOBA_BLOCK
    kb_scr[...] = k_ref[...].astype(BF16)
    for j in range(nb):
        rows = slice(j * blk, (j + 1) * blk)
        km_scr[j:j + 1, :] = jnp.mean(k_ref[rows, :], axis=0, keepdims=True)
        vt_scr[:, rows] = v_ref[rows, :].T.astype(BF16)
    causal = lax.broadcasted_iota(jnp.int32, (blk, blk), 0) <= lax.broadcasted_iota(jnp.int32, (blk, blk), 1)
    causal_bias = jnp.where(causal, 0.0, NEG_INF).astype(F32)

    for qi in range(nb):
        n_keys = (qi + 1) * blk
        q = q_ref[qi * blk:(qi + 1) * blk, :]
        s_all = lax.dot_general(kb_scr[0:n_keys, :], q.astype(BF16), (((1,), (1,)), ((), ())),
                                preferred_element_type=F32) * scale
        if qi > 0:
            gate_t = lax.dot_general(km_scr[...], q, (((1,), (1,)), ((), ())),
                                     precision=lax.Precision.HIGHEST, preferred_element_type=F32)
            sel_bias = jnp.where(_moba_select_rows(gate_t, qi), 0.0, NEG_INF).astype(F32)
        s_blocks = [s_all[j * blk:(j + 1) * blk, :] + sel_bias[j:j + 1, :] for j in range(qi)]
        s_blocks.append(s_all[qi * blk:n_keys, :] + causal_bias)
        top = s_blocks[0]
        for sb in s_blocks[1:]:
            top = jnp.maximum(top, sb)
        m = jnp.max(top, axis=0, keepdims=True)
        p_blocks = [jnp.exp(sb - m) for sb in s_blocks]
        tot = p_blocks[0]
        for pb in p_blocks[1:]:
            tot = tot + pb
        l = jnp.sum(tot, axis=0, keepdims=True)
        p_all = jnp.concatenate([pb.astype(BF16) for pb in p_blocks], axis=0)
        acc = jnp.dot(vt_scr[:, 0:n_keys], p_all, preferred_element_type=F32)
        o_ref[qi * blk:(qi + 1) * blk, :] = (acc / l).T.astype(o_ref.dtype)


def _moba_prompt(q, k, v, bsz, t_len, nh, hd):
    m = q.shape[0]
    assert t_len % MOBA_BLOCK == 0
    nb = t_len // MOBA_BLOCK
    seq_spec = pl.BlockSpec((t_len, hd), lambda b, h: (b, h))
    return pl.pallas_call(
        functools.partial(_moba_prompt_kernel, nb=nb, scale=hd ** -0.5),
        grid=(bsz, nh),
        in_specs=[seq_spec, seq_spec, seq_spec],
        out_specs=seq_spec,
        out_shape=jax.ShapeDtypeStruct((m, nh * hd), BF16),
        scratch_shapes=[pltpu.VMEM((t_len, hd), BF16), pltpu.VMEM((hd, t_len), BF16),
                        pltpu.VMEM((nb, hd), F32)],
        compiler_params=_cparams(("parallel", "parallel")),
        name="moba_prompt",
    )(q, k, v)


def _moba_select_grouped(gate, n_valid_blocks, group):
    rows, nl = gate.shape
    lane = lax.broadcasted_iota(jnp.int32, (rows, nl), 1)
    valid = lane < n_valid_blocks * group
    gate = jnp.where(valid, gate, NEG_INF)
    beaten_by = jnp.zeros((rows, nl), jnp.int32)
    for jp in range(0, nl, group):
        col = gate[:, jp:jp + 1]
        beats = (col > gate) | ((col == gate) & (lane >= jp + group))
        beaten_by = beaten_by + jnp.where(beats, 1, 0)
    return valid & (beaten_by < MOBA_TOPK)


def _fold_lanes(x, op):
    tiles = [x[:, i:i + LANES] for i in range(0, x.shape[1], LANES)]
    out = tiles[0]
    for t in tiles[1:]:
        out = op(out, t)
    return out


def _sample_attn_kernel(pt_ref, q_ref, kn_ref, vn_ref, hbias_ref, obias_ref, *refs, nh, n_past_blocks,
                        pages_per_step, pages_per_block, scale):
    del pt_ref
    kp_refs = refs[:pages_per_step]
    vp_refs = refs[pages_per_step:2 * pages_per_step]
    o_ref, gate_scr, m_scr, l_scr, acc_scr = refs[2 * pages_per_step:]
    step = pl.program_id(1)
    n_steps = pl.num_programs(1)
    rows, hd = q_ref.shape
    n_pages = m_scr.shape[1]
    q = q_ref[...]
    qb = q.astype(BF16)
    q3 = q.reshape(rows // nh, nh, hd)
    lane = lax.broadcasted_iota(jnp.int32, (rows, n_pages), 1)

    def put_column(scr, pg, col):
        scr[...] = jnp.where(lane == pg, col, scr[...])

    for blk0 in range(0, pages_per_step, pages_per_block):
        ksum = jnp.zeros((nh, hd), F32)
        for r in range(blk0, blk0 + pages_per_block):
            pg = step * pages_per_step + r
            k = kp_refs[r][...]
            ksum = ksum + jnp.sum(k.reshape(k.shape[0] // nh, nh, hd), axis=0)
            s = lax.dot_general(qb, k.astype(BF16), (((1,), (1,)), ((), ())), preferred_element_type=F32)
            s = s * scale + hbias_ref[...]
            m = jnp.max(_fold_lanes(s, jnp.maximum), axis=1, keepdims=True)
            p = jnp.exp(s - m)
            put_column(m_scr, pg, m)
            put_column(l_scr, pg, jnp.sum(_fold_lanes(p, jnp.add), axis=1, keepdims=True))
            acc_scr[pg] = jnp.dot(p.astype(BF16), vp_refs[r][...].astype(BF16), preferred_element_type=F32)
        kmean = ksum / MOBA_BLOCK
        gcol = jnp.sum(q3 * kmean[None, :, :], axis=2, keepdims=True).reshape(rows, 1)
        for r in range(blk0, blk0 + pages_per_block):
            put_column(gate_scr, step * pages_per_step + r, gcol)

    @pl.when(step == n_steps - 1)
    def _():
        keep = _moba_select_grouped(gate_scr[...], n_past_blocks, pages_per_block)
        m_pages = m_scr[...] + jnp.where(keep, 0.0, NEG_INF)
        s = lax.dot_general(qb, kn_ref[...].astype(BF16), (((1,), (1,)), ((), ())), preferred_element_type=F32)
        s = s * scale + obias_ref[...]
        m_own = jnp.max(s, axis=1, keepdims=True)
        m_tot = jnp.maximum(m_own, jnp.max(m_pages, axis=1, keepdims=True))
        p = jnp.exp(s - m_tot)
        w = jnp.exp(m_pages - m_tot)
        l = jnp.sum(p, axis=1, keepdims=True) + jnp.sum(w * l_scr[...], axis=1, keepdims=True)
        acc = jnp.dot(p.astype(BF16), vn_ref[...].astype(BF16), preferred_element_type=F32)
        for pg in range(n_pages):
            acc = acc + w[:, pg:pg + 1] * acc_scr[pg]
        o_ref[...] = (acc / l).astype(o_ref.dtype)


def _sample_attn(q, k_new, v_new, cache_k, cache_v, layer, page_table, db, t_len, nh, hd, pages_per_step=4):
    n_hyb, n_pool, page, _, _ = cache_k.shape
    n_pages = page_table.shape[1]
    past_len = n_pages * page
    assert past_len % MOBA_BLOCK == 0 and t_len <= MOBA_BLOCK and MOBA_BLOCK % page == 0
    assert nh & (nh - 1) == 0, "head matching uses bit masks"
    ppb = MOBA_BLOCK // page
    assert pages_per_step % ppb == 0 and n_pages % pages_per_step == 0
    rows = t_len * nh
    ck = cache_k.reshape(n_hyb, n_pool, page * nh, hd)
    cv = cache_v.reshape(n_hyb, n_pool, page * nh, hd)
    r = jnp.arange(rows, dtype=jnp.int32)[:, None]
    c = jnp.arange(page * nh, dtype=jnp.int32)[None, :]
    same_head = (r & (nh - 1)) == (c & (nh - 1))
    head_bias = jnp.where(same_head, 0.0, NEG_INF).astype(F32)
    own_bias = jnp.where(same_head[:, :rows] & (c[:, :rows] // nh <= r // nh), 0.0, NEG_INF).astype(F32)

    def page_spec(i):
        return pl.BlockSpec((None, None, page * nh, hd),
                            lambda b, s, pt: (layer, pt[b, s * pages_per_step + i], 0, 0))

    def row_spec(nrows):
        return pl.BlockSpec((None, nrows, hd), lambda b, s, pt: (b, 0, 0))

    full = lambda a: pl.BlockSpec(a.shape, lambda b, s, pt: (0,) * a.ndim)
    out = pl.pallas_call(
        functools.partial(_sample_attn_kernel, nh=nh, n_past_blocks=past_len // MOBA_BLOCK,
                          pages_per_step=pages_per_step, pages_per_block=ppb, scale=hd ** -0.5),
        grid_spec=pltpu.PrefetchScalarGridSpec(
            num_scalar_prefetch=1,
            grid=(db, n_pages // pages_per_step),
            in_specs=[row_spec(rows), row_spec(rows), row_spec(rows), full(head_bias), full(own_bias)]
            + [page_spec(i) for i in range(pages_per_step)] * 2,
            out_specs=row_spec(rows),
            scratch_shapes=[pltpu.VMEM((rows, n_pages), F32), pltpu.VMEM((rows, n_pages), F32),
                            pltpu.VMEM((rows, n_pages), F32), pltpu.VMEM((n_pages, rows, hd), F32)],
        ),
        out_shape=jax.ShapeDtypeStruct((db, rows, hd), BF16),
        compiler_params=_cparams(("parallel", "arbitrary")),
        name="sample_attn",
    )(page_table, q.reshape(db, rows, hd), k_new.reshape(db, rows, hd), v_new.reshape(db, rows, hd),
      head_bias, own_bias, *([ck] * pages_per_step), *([cv] * pages_per_step))
    return out.reshape(db * t_len, nh * hd)


def _s5_kernel(u_ref, bre_ref, bim_ref, cre_ref, cimn_ref, lre_ref, lim_ref, llre_ref, llim_ref, d_ref,
               h0re_ref, h0im_ref, y_ref, htre_ref, htim_ref, hr_scr, hi_scr, cre_scr, cim_scr,
               *, seg_len, chained, n_chunks):
    c = pl.program_id(1)
    nseg = SUBLANES
    nstate = hr_scr.shape[1]
    n_kb = bre_ref.shape[0]
    ku = bre_ref.shape[1]
    ks = bre_ref.shape[2]

    for kb in range(n_kb):
        ukb = u_ref[:, kb * ku:(kb + 1) * ku].astype(BF16)
        hr_scr[:, kb * ks:(kb + 1) * ks] = jnp.dot(ukb, bre_ref[kb], preferred_element_type=F32)
        hi_scr[:, kb * ks:(kb + 1) * ks] = jnp.dot(ukb, bim_ref[kb], preferred_element_type=F32)

    if chained:
        @pl.when(c == 0)
        def _():
            cre_scr[...] = jnp.zeros_like(cre_scr)
            cim_scr[...] = jnp.zeros_like(cim_scr)
            cre_scr[0:1, :] = h0re_ref[...]
            cim_scr[0:1, :] = h0im_ref[...]

    for lc in range(nstate // S5_SCAN_LANES):
        cols = slice(lc * S5_SCAN_LANES, (lc + 1) * S5_SCAN_LANES)
        lr = jnp.broadcast_to(lre_ref[:, cols], (nseg, S5_SCAN_LANES))
        li = jnp.broadcast_to(lim_ref[:, cols], (nseg, S5_SCAN_LANES))

        def advance(i, hr, hi):
            r = pl.multiple_of(i * nseg, nseg)
            nr = lr * hr - li * hi + hr_scr[pl.ds(r, nseg), cols]
            ni = lr * hi + li * hr + hi_scr[pl.ds(r, nseg), cols]
            return r, nr, ni

        def scan_only(i, carry):
            _, nr, ni = advance(i, *carry)
            return nr, ni

        def scan_store(i, carry):
            r, nr, ni = advance(i, *carry)
            hr_scr[pl.ds(r, nseg), cols] = nr
            hi_scr[pl.ds(r, nseg), cols] = ni
            return nr, ni

        if chained:
            er, ei = lax.fori_loop(0, seg_len, scan_only, (cre_scr[:, cols], cim_scr[:, cols]),
                                   unroll=S5_SCAN_UNROLL)
            llr, lli = llre_ref[:, cols], llim_ref[:, cols]
            fr, fi = er[0:1, :], ei[0:1, :]
            for j in range(1, nseg):
                cre_scr[j:j + 1, cols] = fr
                cim_scr[j:j + 1, cols] = fi
                if j < nseg - 1:
                    fr, fi = (er[j:j + 1, :] + (llr * fr - lli * fi), ei[j:j + 1, :] + (llr * fi + lli * fr))
            init = (cre_scr[:, cols], cim_scr[:, cols])
        else:
            init = (h0re_ref[:, cols], h0im_ref[:, cols])

        hr, hi = lax.fori_loop(0, seg_len, scan_store, init, unroll=S5_SCAN_UNROLL)

        if chained:
            last_r, last_i = hr[nseg - 1:nseg, :], hi[nseg - 1:nseg, :]
            cre_scr[:, cols] = jnp.zeros((nseg, S5_SCAN_LANES), F32)
            cim_scr[:, cols] = jnp.zeros((nseg, S5_SCAN_LANES), F32)
            cre_scr[0:1, cols] = last_r
            cim_scr[0:1, cols] = last_i

            @pl.when(c == n_chunks - 1)
            def _():
                htre_ref[:, cols] = last_r
                htim_ref[:, cols] = last_i
        else:
            htre_ref[:, cols] = hr
            htim_ref[:, cols] = hi

    for kb in range(n_kb):
        hre = hr_scr[:, kb * ks:(kb + 1) * ks].astype(BF16)
        him = hi_scr[:, kb * ks:(kb + 1) * ks].astype(BF16)
        ykb = (jnp.dot(hre, cre_ref[kb], preferred_element_type=F32)
               + jnp.dot(him, cimn_ref[kb], preferred_element_type=F32))
        cols = slice(kb * ku, (kb + 1) * ku)
        y_ref[:, cols] = jax.nn.gelu(ykb + d_ref[:, cols] * u_ref[:, cols])


def _s5_discretise(a_re, a_im, log_dt, b_re, b_im, c_re, c_im, d_skip, seg_len):
    g, p = a_re.shape
    ch = b_re.shape[-1]
    lam = lax.complex(a_re.astype(F32), a_im.astype(F32))
    dt = jnp.exp(log_dt.astype(F32))[:, None]
    lam_bar = jnp.exp(lam * dt)
    b_bar = ((lam_bar - 1.0) / lam)[..., None] * lax.complex(b_re.astype(F32), b_im.astype(F32))
    lam_seg = lam_bar
    assert seg_len & (seg_len - 1) == 0
    for _ in range(seg_len.bit_length() - 1):
        lam_seg = lam_seg * lam_seg
    gl = S5_LANE_GROUPS
    n_kb = g // gl
    eye = jnp.eye(gl, dtype=F32)

    def pack_b(x):
        x = x.reshape(n_kb, gl, p, ch).transpose(0, 1, 3, 2)
        return jnp.einsum("kgcp,gh->kgchp", x, eye).reshape(n_kb, gl * ch, gl * p)

    def pack_c(x):
        x = x.reshape(n_kb, gl, ch, p).transpose(0, 1, 3, 2)
        return jnp.einsum("kgpc,gh->kgphc", x, eye).reshape(n_kb, gl * p, gl * ch)

    flat = lambda x: x.reshape(1, g * p)
    return dict(
        bre=pack_b(jnp.real(b_bar)).astype(BF16), bim=pack_b(jnp.imag(b_bar)).astype(BF16),
        cre=pack_c(c_re.astype(F32)).astype(BF16), cimn=pack_c(-c_im.astype(F32)).astype(BF16),
        lre=flat(jnp.real(lam_bar)), lim=flat(jnp.imag(lam_bar)),
        llre=flat(jnp.real(lam_seg)), llim=flat(jnp.imag(lam_seg)),
        d=d_skip.astype(F32).reshape(1, g * ch),
    )


def _s5(u, h0_re, h0_im, prm, seg_len, chained):
    m, w = u.shape
    gp = prm["lre"].shape[1]
    n_kb, ku, ks = prm["bre"].shape
    chunk = SUBLANES * seg_len
    if chained:
        nseq = h0_re.shape[0]
        n_chunks = m // nseq // chunk
        grid = (nseq, n_chunks)
        u_spec = pl.BlockSpec((chunk, w), lambda b, c: (b * n_chunks + c, 0))
        h_spec = pl.BlockSpec((None, 1, gp), lambda b, c: (b, 0, 0))
        h_shape = jax.ShapeDtypeStruct((nseq, 1, gp), F32)
    else:
        assert m == chunk
        n_chunks = 1
        grid = (1, 1)
        u_spec = pl.BlockSpec((chunk, w), lambda b, c: (0, 0))
        h_spec = pl.BlockSpec((SUBLANES, gp), lambda b, c: (0, 0))
        h_shape = jax.ShapeDtypeStruct((SUBLANES, gp), F32)
    full = lambda a: pl.BlockSpec(a.shape, lambda b, c: (0,) * a.ndim)
    names = ("bre", "bim", "cre", "cimn", "lre", "lim", "llre", "llim", "d")

    def swap_rows(a, n_outer, n_inner):
        return a.reshape(m // chunk, n_outer, n_inner, w).transpose(0, 2, 1, 3).reshape(m, w)

    y, ht_re, ht_im = pl.pallas_call(
        functools.partial(_s5_kernel, seg_len=seg_len, chained=chained, n_chunks=n_chunks),
        grid=grid,
        in_specs=[u_spec] + [full(prm[n]) for n in names] + [h_spec, h_spec],
        out_specs=[u_spec, h_spec, h_spec],
        out_shape=[jax.ShapeDtypeStruct((m, w), F32), h_shape, h_shape],
        scratch_shapes=[pltpu.VMEM((chunk, gp), F32), pltpu.VMEM((chunk, gp), F32),
                        pltpu.VMEM((SUBLANES, gp), F32), pltpu.VMEM((SUBLANES, gp), F32)],
        compiler_params=_cparams(("parallel", "arbitrary")),
        name="s5_scan",
    )(swap_rows(u, SUBLANES, seg_len), *[prm[n] for n in names], h0_re, h0_im)
    return swap_rows(y, seg_len, SUBLANES), ht_re, ht_im


def _pick_tile(n, target):
    t = min(n, target)
    while n % t:
        t //= 2
    return t


def _run_trunk(x, prm, s5_prms, kv, s5_re0, s5_im0, conv0, chained):
    bsz, t_len, d = x.shape
    m = bsz * t_len
    depth = prm["ffn1_wgu"].shape[0]
    alpha = (2.0 * depth) ** 0.25
    nh, hd = kv["nh"], kv["hd"]
    moba_w = nh * hd
    s5_w = d - moba_w
    g_cnt, p_cnt = prm["s5_a_re"].shape[1:]
    d_ff = prm["ffn1_wd"].shape[1]
    tm = _pick_tile(m, 1024)
    tm_ffn = _pick_tile(m, 512)
    tf = _pick_tile(d_ff, 512)
    tm_small = _pick_tile(m, 512)

    x32 = x.reshape(m, d).astype(F32)
    xb = x32.astype(BF16)
    row = lambda a: a.reshape(1, -1).astype(F32)
    ks, vs, hrs, his, cvs = [], [], [], [], []
    for layer in range(depth):
        i = layer // 2
        g, b = prm["ln_g"][layer], prm["ln_b"][layer]
        x32, xb = _ffn(x32, xb, prm["ffn1_wgu"], prm["ffn1_wd"], layer, row(g[0]), row(b[0]), alpha, tm_ffn, tf)
        if layer % 2 == 0:
            assert s5_w == moba_w
            q, k, v, u = _mm_split(xb, prm["hyb_w_in"], i, 4, tm_ffn, 2)
            if chained:
                attn = _moba_prompt(q, k, v, bsz, t_len, nh, hd)
            else:
                attn = _sample_attn(q, k, v, kv["cache_k"], kv["cache_v"], i, kv["page_table"], bsz, t_len, nh, hd)
            sp = s5_prms[i]
            if chained:
                y, h_re, h_im = _s5(u, s5_re0[i].reshape(bsz, 1, g_cnt * p_cnt),
                                    s5_im0[i].reshape(bsz, 1, g_cnt * p_cnt), sp, sp["seg_len"], True)
            else:
                y, h_re, h_im = _s5(u, s5_re0[i].reshape(bsz, g_cnt * p_cnt), s5_im0[i].reshape(bsz, g_cnt * p_cnt),
                                    sp, sp["seg_len"], False)
            ssm = _glu(y, prm["s5_glu_w"], i, row(prm["s5_glu_b"][i]), tm_small)
            x32, xb = _mm_res_ln([attn, ssm], prm["hyb_w_out"], i, x32, row(g[1]), row(b[1]), alpha, tm_small)
            ks.append(k.reshape(bsz, t_len, nh, hd))
            vs.append(v.reshape(bsz, t_len, nh, hd))
            hrs.append(h_re.reshape(bsz, g_cnt, p_cnt))
            his.append(h_im.reshape(bsz, g_cnt, p_cnt))
        else:
            tt = _pick_tile(t_len, 512)
            if tt % 16 == 0:
                c, c_new = _conv_in_core(xb, prm["conv_w_in"], conv0[i], prm["conv_w"], i, bsz, t_len, tt,
                                         _pick_tile(d, 512))
            else:
                z = _mm(xb, prm["conv_w_in"], i, 0, 3 * d, tm, _pick_tile(d, 2048))
                c, c_new = _conv_core(z, conv0[i], prm["conv_w"], i, bsz, t_len, tt, _pick_tile(d, 512))
            x32, xb = _mm_res_ln([c], prm["conv_w_out"], i, x32, row(g[1]), row(b[1]), alpha, tm_small)
            cvs.append(c_new)
        x32, xb = _ffn(x32, xb, prm["ffn2_wgu"], prm["ffn2_wd"], layer, row(g[2]), row(b[2]), alpha, tm_ffn, tf)
    return (x32.reshape(bsz, t_len, d), jnp.stack(ks), jnp.stack(vs), jnp.stack(hrs), jnp.stack(his),
            jnp.stack(cvs))


def kernel(x_prompt, x_sample, cache_k, cache_v, state_s5_re, state_s5_im, state_conv, page_table, ffn1_wgu, ffn1_wd, ffn2_wgu, ffn2_wd, ln_g, ln_b, hyb_w_in, hyb_w_out, s5_a_re, s5_a_im, s5_log_dt, s5_b_re, s5_b_im, s5_c_re, s5_c_im, s5_d, s5_glu_w, s5_glu_b, conv_w_in, conv_w, conv_w_out):
    n_hyb, _, _, nh, hd = cache_k.shape
    bp, t_prompt, d = x_prompt.shape
    db, t_sample, _ = x_sample.shape
    n_conv, cw = conv_w.shape[:2]
    g_cnt, p_cnt = s5_a_re.shape[1:]
    assert db == SUBLANES, "the sample scan lays the decode batch along the vreg sublanes"
    assert (s5_b_re.shape[-1] * S5_LANE_GROUPS) == 128

    prm = dict(
        ffn1_wgu=ffn1_wgu.astype(BF16), ffn1_wd=ffn1_wd.astype(BF16),
        ffn2_wgu=ffn2_wgu.astype(BF16), ffn2_wd=ffn2_wd.astype(BF16),
        ln_g=ln_g, ln_b=ln_b, hyb_w_in=hyb_w_in.astype(BF16), hyb_w_out=hyb_w_out.astype(BF16),
        s5_a_re=s5_a_re, s5_glu_w=s5_glu_w.astype(BF16), s5_glu_b=s5_glu_b,
        conv_w_in=conv_w_in.astype(BF16), conv_w=conv_w.astype(F32), conv_w_out=conv_w_out.astype(BF16),
    )

    def s5_prms(seg_len):
        out = []
        for i in range(n_hyb):
            sp = _s5_discretise(s5_a_re[i], s5_a_im[i], s5_log_dt[i], s5_b_re[i], s5_b_im[i], s5_c_re[i],
                                s5_c_im[i], s5_d[i], seg_len)
            sp["seg_len"] = seg_len
            out.append(sp)
        return out

    prompt_seg = _pick_tile(t_prompt // SUBLANES, 32)
    kv_none = dict(nh=nh, hd=hd)
    s5_zero = jnp.zeros((n_hyb, bp, g_cnt, p_cnt), F32)
    conv_zero = jnp.zeros((n_conv, bp, cw - 1, d), x_prompt.dtype)
    y_p, k_p, v_p, hr_p, hi_p, cv_p = _run_trunk(x_prompt, prm, s5_prms(prompt_seg), kv_none, s5_zero, s5_zero,
                                                 conv_zero, True)
    kv = dict(nh=nh, hd=hd, cache_k=cache_k, cache_v=cache_v, page_table=page_table)
    y_s, k_s, v_s, hr_s, hi_s, cv_s = _run_trunk(x_sample, prm, s5_prms(t_sample), kv, state_s5_re, state_s5_im,
                                                 state_conv, False)
    return (y_p, y_s, k_p, v_p, hr_p, hi_p, cv_p, k_s, v_s, hr_s, hi_s, cv_s)
```

```python
import functools

import jax
import jax.numpy as jnp
from jax import lax
from jax.experimental import pallas as pl
from jax.experimental.pallas import tpu as pltpu

MOBA_BLOCK = 256
MOBA_TOPK = 3
LN_EPS = 1e-5
LANES = 128
SUBLANES = 8
S5_LANE_GROUPS = 8
S5_SCAN_LANES = 512
S5_SCAN_UNROLL = 4
MAX_ROW_TILE = 640
VMEM_LIMIT_BYTES = 56 * 1024 * 1024

F32 = jnp.float32
BF16 = jnp.bfloat16
NEG_INF = float("-inf")


def _cparams(semantics):
    return pltpu.CompilerParams(dimension_semantics=semantics, vmem_limit_bytes=VMEM_LIMIT_BYTES)


def _layer_norm(y, g, b):
    mu = jnp.mean(y, axis=-1, keepdims=True)
    d = y - mu
    var = jnp.mean(d * d, axis=-1, keepdims=True)
    return d * lax.rsqrt(var + LN_EPS) * g + b


def _ffn_kernel(x32_ref, xb_ref, wg_ref, wu_ref, wd_ref, g_ref, b_ref, o32_ref, ob_ref, *, alpha, nf):
    j = pl.program_id(1)

    @pl.when(j == 0)
    def _():
        o32_ref[...] = jnp.zeros_like(o32_ref)

    xb = xb_ref[...]
    gate = jnp.dot(xb, wg_ref[...], preferred_element_type=F32)
    up = jnp.dot(xb, wu_ref[...], preferred_element_type=F32)
    h = (gate * jax.nn.sigmoid(gate)) * up
    o32_ref[...] += jnp.dot(h.astype(BF16), wd_ref[...], preferred_element_type=F32)

    @pl.when(j == nf - 1)
    def _():
        z = _layer_norm(alpha * x32_ref[...] + 0.5 * o32_ref[...], g_ref[...], b_ref[...])
        o32_ref[...] = z
        ob_ref[...] = z.astype(BF16)


def _ffn(x32, xb, wgu, wd, layer, g, b, alpha, tm, tf):
    m, d = x32.shape
    f = wd.shape[1]
    nf = f // tf
    return pl.pallas_call(
        functools.partial(_ffn_kernel, alpha=alpha, nf=nf),
        grid=(m // tm, nf),
        in_specs=[
            pl.BlockSpec((tm, d), lambda i, j: (i, 0)),
            pl.BlockSpec((tm, d), lambda i, j: (i, 0)),
            pl.BlockSpec((None, d, tf), lambda i, j: (layer, 0, j)),
            pl.BlockSpec((None, d, tf), lambda i, j: (layer, 0, j + nf)),
            pl.BlockSpec((None, tf, d), lambda i, j: (layer, j, 0)),
            pl.BlockSpec((1, d), lambda i, j: (0, 0)),
            pl.BlockSpec((1, d), lambda i, j: (0, 0)),
        ],
        out_specs=[pl.BlockSpec((tm, d), lambda i, j: (i, 0)), pl.BlockSpec((tm, d), lambda i, j: (i, 0))],
        out_shape=[jax.ShapeDtypeStruct((m, d), F32), jax.ShapeDtypeStruct((m, d), BF16)],
        compiler_params=_cparams(("parallel", "arbitrary")),
        name="ffn_ln",
    )(x32, xb, wgu, wgu, wd, g, b)


def _mm_kernel(x_ref, w_ref, o_ref):
    o_ref[...] = jnp.dot(x_ref[...], w_ref[...], preferred_element_type=F32)


def _mm(xb, w, layer, col0, n, tm, tn):
    m, k = xb.shape
    assert col0 % tn == 0 and n % tn == 0
    jb0 = col0 // tn
    return pl.pallas_call(
        _mm_kernel,
        grid=(m // tm, n // tn),
        in_specs=[
            pl.BlockSpec((tm, k), lambda i, j: (i, 0)),
            pl.BlockSpec((None, k, tn), lambda i, j: (layer, 0, jb0 + j)),
        ],
        out_specs=pl.BlockSpec((tm, tn), lambda i, j: (i, j)),
        out_shape=jax.ShapeDtypeStruct((m, n), F32),
        compiler_params=_cparams(("parallel", "arbitrary")),
        name="proj",
    )(xb, w)


def _mm_split_kernel(x_ref, w_ref, *o_refs, per_step):
    j = pl.program_id(1)
    y = jnp.dot(x_ref[...], w_ref[...], preferred_element_type=F32)
    width = y.shape[1] // per_step
    for jj in range(len(o_refs) // per_step):
        @pl.when(j == jj)
        def _():
            for r in range(per_step):
                o_refs[jj * per_step + r][...] = y[:, r * width:(r + 1) * width]


def _mm_split(xb, w, layer, n_out, tm, per_step):
    m, k = xb.shape
    width = w.shape[2] // n_out
    tn = per_step * width
    out_spec = pl.BlockSpec((tm, width), lambda i, j: (i, 0))
    return pl.pallas_call(
        functools.partial(_mm_split_kernel, per_step=per_step),
        grid=(m // tm, n_out // per_step),
        in_specs=[
            pl.BlockSpec((tm, k), lambda i, j: (i, 0)),
            pl.BlockSpec((None, k, tn), lambda i, j: (layer, 0, j)),
        ],
        out_specs=[out_spec] * n_out,
        out_shape=[jax.ShapeDtypeStruct((m, width), F32)] * n_out,
        compiler_params=_cparams(("parallel", "arbitrary")),
        name="proj_split",
    )(xb, w)


def _mm_res_ln_kernel(*refs, nparts, alpha):
    a_refs = refs[:nparts]
    w_refs = refs[nparts:2 * nparts]
    x32_ref, g_ref, b_ref, o32_ref, ob_ref = refs[2 * nparts:]
    y = alpha * x32_ref[...]
    for a_ref, w_ref in zip(a_refs, w_refs):
        y = y + jnp.dot(a_ref[...], w_ref[...], preferred_element_type=F32)
    z = _layer_norm(y, g_ref[...], b_ref[...])
    o32_ref[...] = z
    ob_ref[...] = z.astype(BF16)


def _mm_res_ln(parts, w, layer, x32, g, b, alpha, tm):
    m, d = x32.shape
    kp = parts[0].shape[1]
    assert all(p.shape[1] == kp for p in parts)
    in_specs = [pl.BlockSpec((tm, kp), lambda i: (i, 0)) for _ in parts]
    in_specs += [pl.BlockSpec((None, kp, d), lambda i, r=r: (layer, r, 0)) for r in range(len(parts))]
    in_specs += [
        pl.BlockSpec((tm, d), lambda i: (i, 0)),
        pl.BlockSpec((1, d), lambda i: (0, 0)),
        pl.BlockSpec((1, d), lambda i: (0, 0)),
    ]
    return pl.pallas_call(
        functools.partial(_mm_res_ln_kernel, nparts=len(parts), alpha=alpha),
        grid=(m // tm,),
        in_specs=in_specs,
        out_specs=[pl.BlockSpec((tm, d), lambda i: (i, 0)), pl.BlockSpec((tm, d), lambda i: (i, 0))],
        out_shape=[jax.ShapeDtypeStruct((m, d), F32), jax.ShapeDtypeStruct((m, d), BF16)],
        compiler_params=_cparams(("parallel",)),
        name="out_proj_ln",
    )(*parts, *([w] * len(parts)), x32, g, b)


def _glu_kernel(y_ref, w_ref, b_ref, o_ref):
    y = y_ref[...]
    z = jnp.dot(y.astype(BF16), w_ref[...], preferred_element_type=F32) + b_ref[...]
    o_ref[...] = (y * jax.nn.sigmoid(z)).astype(o_ref.dtype)


def _glu(y, w, layer, b, tm):
    m, n = y.shape
    return pl.pallas_call(
        _glu_kernel,
        grid=(m // tm,),
        in_specs=[
            pl.BlockSpec((tm, n), lambda i: (i, 0)),
            pl.BlockSpec((None, n, n), lambda i: (layer, 0, 0)),
            pl.BlockSpec((1, n), lambda i: (0, 0)),
        ],
        out_specs=pl.BlockSpec((tm, n), lambda i: (i, 0)),
        out_shape=jax.ShapeDtypeStruct((m, n), BF16),
        compiler_params=_cparams(("parallel",)),
        name="s5_glu",
    )(y, w, b)


CONV_PAD = 8


def _conv_tile(bg, cg, v, st_ref, w_ref, c_ref, ns_ref, hbuf, *, tt, nt, cw):
    t = pl.program_id(2)
    lo = CONV_PAD - (cw - 1)

    @pl.when(t == 0)
    def _():
        hbuf[lo:CONV_PAD, :] = st_ref[...]

    hbuf[CONV_PAD:CONV_PAD + tt, :] = cg * v
    w = w_ref[...]
    conv = w[0:1, :] * hbuf[lo:lo + tt, :]
    for j in range(1, cw):
        conv = conv + w[j:j + 1, :] * hbuf[lo + j:lo + j + tt, :]
    c_ref[...] = (bg * conv).astype(c_ref.dtype)
    tail = hbuf[lo + tt:CONV_PAD + tt, :]

    @pl.when(t == nt - 1)
    def _():
        ns_ref[...] = tail

    hbuf[lo:CONV_PAD, :] = tail


def _conv_kernel(bg_ref, cg_ref, v_ref, st_ref, w_ref, c_ref, ns_ref, hbuf, **kw):
    _conv_tile(bg_ref[...], cg_ref[...], v_ref[...], st_ref, w_ref, c_ref, ns_ref, hbuf, **kw)


def _conv_in_kernel(x_ref, wb_ref, wc_ref, wv_ref, st_ref, w_ref, c_ref, ns_ref, hbuf, **kw):
    x = x_ref[...]
    bg, cg, v = (jnp.dot(x, w[...], preferred_element_type=F32) for w in (wb_ref, wc_ref, wv_ref))
    _conv_tile(bg, cg, v, st_ref, w_ref, c_ref, ns_ref, hbuf, **kw)


def _conv_core(z, state, conv_w, layer, bsz, t_len, tt, tn):
    m, d3 = z.shape
    d = d3 // 3
    cw = conv_w.shape[1]
    nt, nj = t_len // tt, d // tn
    return pl.pallas_call(
        functools.partial(_conv_kernel, tt=tt, nt=nt, cw=cw),
        grid=(bsz, nj, nt),
        in_specs=[
            pl.BlockSpec((tt, tn), lambda b, j, t: (b * nt + t, j)),
            pl.BlockSpec((tt, tn), lambda b, j, t: (b * nt + t, j + nj)),
            pl.BlockSpec((tt, tn), lambda b, j, t: (b * nt + t, j + 2 * nj)),
            pl.BlockSpec((None, cw - 1, tn), lambda b, j, t: (b, 0, j)),
            pl.BlockSpec((None, cw, tn), lambda b, j, t: (layer, 0, j)),
        ],
        out_specs=[
            pl.BlockSpec((tt, tn), lambda b, j, t: (b * nt + t, j)),
            pl.BlockSpec((None, cw - 1, tn), lambda b, j, t: (b, 0, j)),
        ],
        out_shape=[jax.ShapeDtypeStruct((m, d), BF16), jax.ShapeDtypeStruct((bsz, cw - 1, d), F32)],
        scratch_shapes=[pltpu.VMEM((CONV_PAD + tt, tn), F32)],
        compiler_params=_cparams(("parallel", "parallel", "arbitrary")),
        name="conv_core",
    )(z, z, z, state, conv_w)


def _conv_in_core(xb, w_in, state, conv_w, layer, bsz, t_len, tt, tn):
    m, d = bsz * t_len, xb.shape[1]
    cw = conv_w.shape[1]
    nt, nj = t_len // tt, d // tn
    return pl.pallas_call(
        functools.partial(_conv_in_kernel, tt=tt, nt=nt, cw=cw),
        grid=(bsz, nj, nt),
        in_specs=[
            pl.BlockSpec((tt, d), lambda b, j, t: (b * nt + t, 0)),
            pl.BlockSpec((None, d, tn), lambda b, j, t: (layer, 0, j)),
            pl.BlockSpec((None, d, tn), lambda b, j, t: (layer, 0, j + nj)),
            pl.BlockSpec((None, d, tn), lambda b, j, t: (layer, 0, j + 2 * nj)),
            pl.BlockSpec((None, cw - 1, tn), lambda b, j, t: (b, 0, j)),
            pl.BlockSpec((None, cw, tn), lambda b, j, t: (layer, 0, j)),
        ],
        out_specs=[
            pl.BlockSpec((tt, tn), lambda b, j, t: (b * nt + t, j)),
            pl.BlockSpec((None, cw - 1, tn), lambda b, j, t: (b, 0, j)),
        ],
        out_shape=[jax.ShapeDtypeStruct((m, d), BF16), jax.ShapeDtypeStruct((bsz, cw - 1, d), F32)],
        scratch_shapes=[pltpu.VMEM((CONV_PAD + tt, tn), F32)],
        compiler_params=_cparams(("parallel", "parallel", "arbitrary")),
        name="conv_in_core",
    )(xb, w_in, w_in, w_in, state, conv_w)


def _moba_select_rows(gate_t, n_valid):
    nb, nq = gate_t.shape
    sub = lax.broadcasted_iota(jnp.int32, (nb, nq), 0)
    valid = sub < n_valid
    gate_t = jnp.where(valid, gate_t, NEG_INF)
    beaten_by = jnp.zeros((nb, nq), jnp.int32)
    for jp in range(nb):
        row = gate_t[jp:jp + 1, :]
        beats = (row > gate_t) | ((row == gate_t) & (sub > jp))
        beaten_by = beaten_by + jnp.where(beats, 1, 0)
    return valid & (beaten_by < MOBA_TOPK)


def _moba_prompt_kernel(q_ref, k_ref, v_ref, o_ref, kb_scr, vt_scr, km_scr, *, nb, scale):
    blk = MOBA_BLOCK
    kb_scr[...] = k_ref[...].astype(BF16)
    for j in range(nb):
        rows = slice(j * blk, (j + 1) * blk)
        km_scr[j:j + 1, :] = jnp.mean(k_ref[rows, :], axis=0, keepdims=True)
        vt_scr[:, rows] = v_ref[rows, :].T.astype(BF16)
    causal = lax.broadcasted_iota(jnp.int32, (blk, blk), 0) <= lax.broadcasted_iota(jnp.int32, (blk, blk), 1)
    causal_bias = jnp.where(causal, 0.0, NEG_INF).astype(F32)

    for qi in range(nb):
        n_keys = (qi + 1) * blk
        q = q_ref[qi * blk:(qi + 1) * blk, :]
        s_all = lax.dot_general(kb_scr[0:n_keys, :], q.astype(BF16), (((1,), (1,)), ((), ())),
                                preferred_element_type=F32) * scale
        if qi > 0:
            gate_t = lax.dot_general(km_scr[...], q, (((1,), (1,)), ((), ())),
                                     precision=lax.Precision.HIGHEST, preferred_element_type=F32)
            sel_bias = jnp.where(_moba_select_rows(gate_t, qi), 0.0, NEG_INF).astype(F32)
        s_blocks = [s_all[j * blk:(j + 1) * blk, :] + sel_bias[j:j + 1, :] for j in range(qi)]
        s_blocks.append(s_all[qi * blk:n_keys, :] + causal_bias)
        top = s_blocks[0]
        for sb in s_blocks[1:]:
            top = jnp.maximum(top, sb)
        m = jnp.max(top, axis=0, keepdims=True)
        p_blocks = [jnp.exp(sb - m) for sb in s_blocks]
        tot = p_blocks[0]
        for pb in p_blocks[1:]:
            tot = tot + pb
        l = jnp.sum(tot, axis=0, keepdims=True)
        p_all = jnp.concatenate([pb.astype(BF16) for pb in p_blocks], axis=0)
        acc = jnp.dot(vt_scr[:, 0:n_keys], p_all, preferred_element_type=F32)
        o_ref[qi * blk:(qi + 1) * blk, :] = (acc / l).T.astype(o_ref.dtype)


def _moba_prompt(q, k, v, bsz, t_len, nh, hd):
    m = bsz * t_len
    assert t_len % MOBA_BLOCK == 0
    nb = t_len // MOBA_BLOCK
    seq_spec = pl.BlockSpec((t_len, hd), lambda b, h: (b, h))
    return pl.pallas_call(
        functools.partial(_moba_prompt_kernel, nb=nb, scale=hd ** -0.5),
        grid=(bsz, nh),
        in_specs=[seq_spec, seq_spec, seq_spec],
        out_specs=seq_spec,
        out_shape=jax.ShapeDtypeStruct((m, nh * hd), BF16),
        scratch_shapes=[pltpu.VMEM((t_len, hd), BF16), pltpu.VMEM((hd, t_len), BF16),
                        pltpu.VMEM((nb, hd), F32)],
        compiler_params=_cparams(("parallel", "parallel")),
        name="moba_prompt",
    )(q, k, v)


def _moba_select_grouped(gate, n_valid_blocks, group):
    rows, nl = gate.shape
    lane = lax.broadcasted_iota(jnp.int32, (rows, nl), 1)
    valid = lane < n_valid_blocks * group
    gate = jnp.where(valid, gate, NEG_INF)
    beaten_by = jnp.zeros((rows, nl), jnp.int32)
    for jp in range(0, nl, group):
        col = gate[:, jp:jp + 1]
        beats = (col > gate) | ((col == gate) & (lane >= jp + group))
        beaten_by = beaten_by + jnp.where(beats, 1, 0)
    return valid & (beaten_by < MOBA_TOPK)


def _fold_lanes(x, op):
    tiles = [x[:, i:i + LANES] for i in range(0, x.shape[1], LANES)]
    out = tiles[0]
    for t in tiles[1:]:
        out = op(out, t)
    return out


def _sample_attn_kernel(pt_ref, q_ref, kn_ref, vn_ref, hbias_ref, obias_ref, *refs, nh, n_past_blocks,
                        pages_per_step, pages_per_block, scale):
    del pt_ref
    kp_refs = refs[:pages_per_step]
    vp_refs = refs[pages_per_step:2 * pages_per_step]
    o_ref, gate_scr, m_scr, l_scr, acc_scr = refs[2 * pages_per_step:]
    step = pl.program_id(1)
    n_steps = pl.num_programs(1)
    rows, hd = q_ref.shape
    n_pages = m_scr.shape[1]
    q = q_ref[...]
    qb = q.astype(BF16)
    q3 = q.reshape(rows // nh, nh, hd)
    lane = lax.broadcasted_iota(jnp.int32, (rows, n_pages), 1)

    def put_column(scr, pg, col):
        scr[...] = jnp.where(lane == pg, col, scr[...])

    for blk0 in range(0, pages_per_step, pages_per_block):
        ksum = jnp.zeros((nh, hd), F32)
        for r in range(blk0, blk0 + pages_per_block):
            pg = step * pages_per_step + r
            k = kp_refs[r][...]
            ksum = ksum + jnp.sum(k.reshape(k.shape[0] // nh, nh, hd), axis=0)
            s = lax.dot_general(qb, k.astype(BF16), (((1,), (1,)), ((), ())), preferred_element_type=F32)
            s = s * scale + hbias_ref[...]
            m = jnp.max(_fold_lanes(s, jnp.maximum), axis=1, keepdims=True)
            p = jnp.exp(s - m)
            put_column(m_scr, pg, m)
            put_column(l_scr, pg, jnp.sum(_fold_lanes(p, jnp.add), axis=1, keepdims=True))
            acc_scr[pg] = jnp.dot(p.astype(BF16), vp_refs[r][...].astype(BF16), preferred_element_type=F32)
        kmean = ksum / MOBA_BLOCK
        gcol = jnp.sum(q3 * kmean[None, :, :], axis=2, keepdims=True).reshape(rows, 1)
        for r in range(blk0, blk0 + pages_per_block):
            put_column(gate_scr, step * pages_per_step + r, gcol)

    @pl.when(step == n_steps - 1)
    def _():
        keep = _moba_select_grouped(gate_scr[...], n_past_blocks, pages_per_block)
        m_pages = m_scr[...] + jnp.where(keep, 0.0, NEG_INF)
        s = lax.dot_general(qb, kn_ref[...].astype(BF16), (((1,), (1,)), ((), ())), preferred_element_type=F32)
        s = s * scale + obias_ref[...]
        m_own = jnp.max(s, axis=1, keepdims=True)
        m_tot = jnp.maximum(m_own, jnp.max(m_pages, axis=1, keepdims=True))
        p = jnp.exp(s - m_tot)
        w = jnp.exp(m_pages - m_tot)
        l = jnp.sum(p, axis=1, keepdims=True) + jnp.sum(w * l_scr[...], axis=1, keepdims=True)
        acc = jnp.dot(p.astype(BF16), vn_ref[...].astype(BF16), preferred_element_type=F32)
        for pg in range(n_pages):
            acc = acc + w[:, pg:pg + 1] * acc_scr[pg]
        o_ref[...] = (acc / l).astype(o_ref.dtype)


def _sample_attn(q, k_new, v_new, cache_k, cache_v, layer, page_table, db, t_len, nh, hd, pages_per_step=4):
    n_hyb, n_pool, page, _, _ = cache_k.shape
    n_pages = page_table.shape[1]
    past_len = n_pages * page
    assert past_len % MOBA_BLOCK == 0 and t_len <= MOBA_BLOCK and MOBA_BLOCK % page == 0
    assert nh & (nh - 1) == 0, "head matching uses bit masks"
    ppb = MOBA_BLOCK // page
    assert pages_per_step % ppb == 0 and n_pages % pages_per_step == 0
    rows = t_len * nh
    ck = cache_k.reshape(n_hyb, n_pool, page * nh, hd)
    cv = cache_v.reshape(n_hyb, n_pool, page * nh, hd)
    r = jnp.arange(rows, dtype=jnp.int32)[:, None]
    c = jnp.arange(page * nh, dtype=jnp.int32)[None, :]
    same_head = (r & (nh - 1)) == (c & (nh - 1))
    head_bias = jnp.where(same_head, 0.0, NEG_INF).astype(F32)
    own_bias = jnp.where(same_head[:, :rows] & (c[:, :rows] // nh <= r // nh), 0.0, NEG_INF).astype(F32)

    def page_spec(i):
        return pl.BlockSpec((None, None, page * nh, hd),
                            lambda b, s, pt: (layer, pt[b, s * pages_per_step + i], 0, 0))

    def row_spec(nrows):
        return pl.BlockSpec((None, nrows, hd), lambda b, s, pt: (b, 0, 0))

    full = lambda a: pl.BlockSpec(a.shape, lambda b, s, pt: (0,) * a.ndim)
    out = pl.pallas_call(
        functools.partial(_sample_attn_kernel, nh=nh, n_past_blocks=past_len // MOBA_BLOCK,
                          pages_per_step=pages_per_step, pages_per_block=ppb, scale=hd ** -0.5),
        grid_spec=pltpu.PrefetchScalarGridSpec(
            num_scalar_prefetch=1,
            grid=(db, n_pages // pages_per_step),
            in_specs=[row_spec(rows), row_spec(rows), row_spec(rows), full(head_bias), full(own_bias)]
            + [page_spec(i) for i in range(pages_per_step)] * 2,
            out_specs=row_spec(rows),
            scratch_shapes=[pltpu.VMEM((rows, n_pages), F32), pltpu.VMEM((rows, n_pages), F32),
                            pltpu.VMEM((rows, n_pages), F32), pltpu.VMEM((n_pages, rows, hd), F32)],
        ),
        out_shape=jax.ShapeDtypeStruct((db, rows, hd), BF16),
        compiler_params=_cparams(("parallel", "arbitrary")),
        name="sample_attn",
    )(page_table, q.reshape(db, rows, hd), k_new.reshape(db, rows, hd), v_new.reshape(db, rows, hd),
      head_bias, own_bias, *([ck] * pages_per_step), *([cv] * pages_per_step))
    return out.reshape(db * t_len, nh * hd)


def _s5_kernel(u_ref, bre_ref, bim_ref, cre_ref, cimn_ref, lre_ref, lim_ref, llre_ref, llim_ref, d_ref,
               h0re_ref, h0im_ref, y_ref, htre_ref, htim_ref, hr_scr, hi_scr, cre_scr, cim_scr,
               *, seg_len, chained, n_chunks):
    c = pl.program_id(1)
    nseg = SUBLANES
    nstate = hr_scr.shape[1]
    n_kb = bre_ref.shape[0]
    ku = bre_ref.shape[1]
    ks = bre_ref.shape[2]

    for kb in range(n_kb):
        ukb = u_ref[:, kb * ku:(kb + 1) * ku].astype(BF16)
        hr_scr[:, kb * ks:(kb + 1) * ks] = jnp.dot(ukb, bre_ref[kb], preferred_element_type=F32)
        hi_scr[:, kb * ks:(kb + 1) * ks] = jnp.dot(ukb, bim_ref[kb], preferred_element_type=F32)

    if chained:
        @pl.when(c == 0)
        def _():
            cre_scr[...] = jnp.zeros_like(cre_scr)
            cim_scr[...] = jnp.zeros_like(cim_scr)
            cre_scr[0:1, :] = h0re_ref[...]
            cim_scr[0:1, :] = h0im_ref[...]

    for lc in range(nstate // S5_SCAN_LANES):
        cols = slice(lc * S5_SCAN_LANES, (lc + 1) * S5_SCAN_LANES)
        lr = jnp.broadcast_to(lre_ref[:, cols], (nseg, S5_SCAN_LANES))
        li = jnp.broadcast_to(lim_ref[:, cols], (nseg, S5_SCAN_LANES))

        def advance(i, hr, hi):
            r = pl.multiple_of(i * nseg, nseg)
            nr = lr * hr - li * hi + hr_scr[pl.ds(r, nseg), cols]
            ni = lr * hi + li * hr + hi_scr[pl.ds(r, nseg), cols]
            return r, nr, ni

        def scan_only(i, carry):
            _, nr, ni = advance(i, *carry)
            return nr, ni

        def scan_store(i, carry):
            r, nr, ni = advance(i, *carry)
            hr_scr[pl.ds(r, nseg), cols] = nr
            hi_scr[pl.ds(r, nseg), cols] = ni
            return nr, ni

        if chained:
            er, ei = lax.fori_loop(0, seg_len, scan_only, (cre_scr[:, cols], cim_scr[:, cols]),
                                   unroll=S5_SCAN_UNROLL)
            llr, lli = llre_ref[:, cols], llim_ref[:, cols]
            fr, fi = er[0:1, :], ei[0:1, :]
            for j in range(1, nseg):
                cre_scr[j:j + 1, cols] = fr
                cim_scr[j:j + 1, cols] = fi
                if j < nseg - 1:
                    fr, fi = (er[j:j + 1, :] + (llr * fr - lli * fi), ei[j:j + 1, :] + (llr * fi + lli * fr))
            init = (cre_scr[:, cols], cim_scr[:, cols])
        else:
            init = (h0re_ref[:, cols], h0im_ref[:, cols])

        hr, hi = lax.fori_loop(0, seg_len, scan_store, init, unroll=S5_SCAN_UNROLL)

        if chained:
            last_r, last_i = hr[nseg - 1:nseg, :], hi[nseg - 1:nseg, :]
            cre_scr[:, cols] = jnp.zeros((nseg, S5_SCAN_LANES), F32)
            cim_scr[:, cols] = jnp.zeros((nseg, S5_SCAN_LANES), F32)
            cre_scr[0:1, cols] = last_r
            cim_scr[0:1, cols] = last_i

            @pl.when(c == n_chunks - 1)
            def _():
                htre_ref[:, cols] = last_r
                htim_ref[:, cols] = last_i
        else:
            htre_ref[:, cols] = hr
            htim_ref[:, cols] = hi

    for kb in range(n_kb):
        hre = hr_scr[:, kb * ks:(kb + 1) * ks].astype(BF16)
        him = hi_scr[:, kb * ks:(kb + 1) * ks].astype(BF16)
        ykb = (jnp.dot(hre, cre_ref[kb], preferred_element_type=F32)
               + jnp.dot(him, cimn_ref[kb], preferred_element_type=F32))
        cols = slice(kb * ku, (kb + 1) * ku)
        y_ref[:, cols] = jax.nn.gelu(ykb + d_ref[:, cols] * u_ref[:, cols])


def _s5_discretise(a_re, a_im, log_dt, b_re, b_im, c_re, c_im, d_skip, seg_len):
    g, p = a_re.shape
    ch = b_re.shape[-1]
    lam = lax.complex(a_re.astype(F32), a_im.astype(F32))
    dt = jnp.exp(log_dt.astype(F32))[:, None]
    lam_bar = jnp.exp(lam * dt)
    b_bar = ((lam_bar - 1.0) / lam)[..., None] * lax.complex(b_re.astype(F32), b_im.astype(F32))
    lam_seg = lam_bar
    assert seg_len & (seg_len - 1) == 0
    for _ in range(seg_len.bit_length() - 1):
        lam_seg = lam_seg * lam_seg
    gl = S5_LANE_GROUPS
    n_kb = g // gl
    eye = jnp.eye(gl, dtype=F32)

    def pack_b(x):
        x = x.reshape(n_kb, gl, p, ch).transpose(0, 1, 3, 2)
        return jnp.einsum("kgcp,gh->kgchp", x, eye).reshape(n_kb, gl * ch, gl * p)

    def pack_c(x):
        x = x.reshape(n_kb, gl, ch, p).transpose(0, 1, 3, 2)
        return jnp.einsum("kgpc,gh->kgphc", x, eye).reshape(n_kb, gl * p, gl * ch)

    flat = lambda x: x.reshape(1, g * p)
    return dict(
        bre=pack_b(jnp.real(b_bar)).astype(BF16), bim=pack_b(jnp.imag(b_bar)).astype(BF16),
        cre=pack_c(c_re.astype(F32)).astype(BF16), cimn=pack_c(-c_im.astype(F32)).astype(BF16),
        lre=flat(jnp.real(lam_bar)), lim=flat(jnp.imag(lam_bar)),
        llre=flat(jnp.real(lam_seg)), llim=flat(jnp.imag(lam_seg)),
        d=d_skip.astype(F32).reshape(1, g * ch),
    )


def _s5(u, h0_re, h0_im, prm, seg_len, chained):
    m, w = u.shape
    gp = prm["lre"].shape[1]
    n_kb, ku, ks = prm["bre"].shape
    chunk = SUBLANES * seg_len
    if chained:
        nseq = h0_re.shape[0]
        n_chunks = m // nseq // chunk
        grid = (nseq, n_chunks)
        u_spec = pl.BlockSpec((chunk, w), lambda b, c: (b * n_chunks + c, 0))
        h_spec = pl.BlockSpec((None, 1, gp), lambda b, c: (b, 0, 0))
        h_shape = jax.ShapeDtypeStruct((nseq, 1, gp), F32)
    else:
        assert m == chunk
        n_chunks = 1
        grid = (1, 1)
        u_spec = pl.BlockSpec((chunk, w), lambda b, c: (0, 0))
        h_spec = pl.BlockSpec((SUBLANES, gp), lambda b, c: (0, 0))
        h_shape = jax.ShapeDtypeStruct((SUBLANES, gp), F32)
    full = lambda a: pl.BlockSpec(a.shape, lambda b, c: (0,) * a.ndim)
    names = ("bre", "bim", "cre", "cimn", "lre", "lim", "llre", "llim", "d")

    def swap_rows(a, n_outer, n_inner):
        return a.reshape(m // chunk, n_outer, n_inner, w).transpose(0, 2, 1, 3).reshape(m, w)

    y, ht_re, ht_im = pl.pallas_call(
        functools.partial(_s5_kernel, seg_len=seg_len, chained=chained, n_chunks=n_chunks),
        grid=grid,
        in_specs=[u_spec] + [full(prm[n]) for n in names] + [h_spec, h_spec],
        out_specs=[u_spec, h_spec, h_spec],
        out_shape=[jax.ShapeDtypeStruct((m, w), F32), h_shape, h_shape],
        scratch_shapes=[pltpu.VMEM((chunk, gp), F32), pltpu.VMEM((chunk, gp), F32),
                        pltpu.VMEM((SUBLANES, gp), F32), pltpu.VMEM((SUBLANES, gp), F32)],
        compiler_params=_cparams(("parallel", "arbitrary")),
        name="s5_scan",
    )(swap_rows(u, SUBLANES, seg_len), *[prm[n] for n in names], h0_re, h0_im)
    return swap_rows(y, seg_len, SUBLANES), ht_re, ht_im


def _pick_tile(n, target):
    t = min(n, target)
    while n % t:
        t //= 2
    return t


def _row_tile(m):
    return min(range(LANES, MAX_ROW_TILE + 1, LANES), key=lambda t: (-(-m // t) * t, -t))


def _stack_rows(prompt_rows, sample_rows, m_all):
    pad = m_all - prompt_rows.shape[0] - sample_rows.shape[0]
    return jnp.concatenate([prompt_rows, sample_rows, jnp.zeros((pad, prompt_rows.shape[1]), prompt_rows.dtype)], axis=0)


def _run_trunk(x_prompt, x_sample, prm, s5_prompt, s5_sample, kv, s5_re0, s5_im0, conv0_prompt, conv0_sample):
    bp, tp, d = x_prompt.shape
    bs, ts, _ = x_sample.shape
    mp, ms = bp * tp, bs * ts
    tm = _row_tile(mp + ms)
    m_all = -(-(mp + ms) // tm) * tm
    depth = prm["ffn1_wgu"].shape[0]
    alpha = (2.0 * depth) ** 0.25
    nh, hd = kv["nh"], kv["hd"]
    moba_w = nh * hd
    g_cnt, p_cnt = prm["s5_a_re"].shape[1:]
    gp = g_cnt * p_cnt
    tf = _pick_tile(prm["ffn1_wd"].shape[1], 512)
    sample = slice(mp, mp + ms)

    x32 = _stack_rows(x_prompt.reshape(mp, d).astype(F32), x_sample.reshape(ms, d).astype(F32), m_all)
    xb = x32.astype(BF16)
    row = lambda a: a.reshape(1, -1).astype(F32)
    out = dict(kp=[], vp=[], hrp=[], hip=[], cvp=[], ks=[], vs=[], hrs=[], his=[], cvs=[])
    for layer in range(depth):
        i = layer // 2
        g, b = prm["ln_g"][layer], prm["ln_b"][layer]
        x32, xb = _ffn(x32, xb, prm["ffn1_wgu"], prm["ffn1_wd"], layer, row(g[0]), row(b[0]), alpha, tm, tf)
        if layer % 2 == 0:
            assert d - moba_w == moba_w
            q, k, v, u = _mm_split(xb, prm["hyb_w_in"], i, 4, tm, 2)
            attn_p = _moba_prompt(q, k, v, bp, tp, nh, hd)
            attn_s = _sample_attn(q[sample], k[sample], v[sample], kv["cache_k"], kv["cache_v"], i,
                                  kv["page_table"], bs, ts, nh, hd)
            zero = jnp.zeros((bp, 1, gp), F32)
            y_p, hr_p, hi_p = _s5(u[:mp], zero, zero, s5_prompt[i], s5_prompt[i]["seg_len"], True)
            y_s, hr_s, hi_s = _s5(u[sample], s5_re0[i].reshape(bs, gp), s5_im0[i].reshape(bs, gp), s5_sample[i],
                                  s5_sample[i]["seg_len"], False)
            ssm = _glu(_stack_rows(y_p, y_s, m_all), prm["s5_glu_w"], i, row(prm["s5_glu_b"][i]), tm)
            x32, xb = _mm_res_ln([_stack_rows(attn_p, attn_s, m_all), ssm], prm["hyb_w_out"], i, x32, row(g[1]),
                                 row(b[1]), alpha, tm)
            out["kp"].append(k[:mp].reshape(bp, tp, nh, hd))
            out["vp"].append(v[:mp].reshape(bp, tp, nh, hd))
            out["ks"].append(k[sample].reshape(bs, ts, nh, hd))
            out["vs"].append(v[sample].reshape(bs, ts, nh, hd))
            out["hrp"].append(hr_p.reshape(bp, g_cnt, p_cnt))
            out["hip"].append(hi_p.reshape(bp, g_cnt, p_cnt))
            out["hrs"].append(hr_s.reshape(bs, g_cnt, p_cnt))
            out["his"].append(hi_s.reshape(bs, g_cnt, p_cnt))
        else:
            c_p, cv_p = _conv_in_core(xb, prm["conv_w_in"], conv0_prompt[i], prm["conv_w"], i, bp, tp,
                                      _pick_tile(tp, 512), _pick_tile(d, 512))
            z = _mm(xb[sample], prm["conv_w_in"], i, 0, 3 * d, ms, _pick_tile(d, 2048))
            c_s, cv_s = _conv_core(z, conv0_sample[i], prm["conv_w"], i, bs, ts, ts, _pick_tile(d, 512))
            x32, xb = _mm_res_ln([_stack_rows(c_p, c_s, m_all)], prm["conv_w_out"], i, x32, row(g[1]), row(b[1]),
                                 alpha, tm)
            out["cvp"].append(cv_p)
            out["cvs"].append(cv_s)
        x32, xb = _ffn(x32, xb, prm["ffn2_wgu"], prm["ffn2_wd"], layer, row(g[2]), row(b[2]), alpha, tm, tf)
    out = {n: jnp.stack(a) for n, a in out.items()}
    return x32[:mp].reshape(bp, tp, d), x32[sample].reshape(bs, ts, d), out


def kernel(x_prompt, x_sample, cache_k, cache_v, state_s5_re, state_s5_im, state_conv, page_table, ffn1_wgu, ffn1_wd, ffn2_wgu, ffn2_wd, ln_g, ln_b, hyb_w_in, hyb_w_out, s5_a_re, s5_a_im, s5_log_dt, s5_b_re, s5_b_im, s5_c_re, s5_c_im, s5_d, s5_glu_w, s5_glu_b, conv_w_in, conv_w, conv_w_out):
    n_hyb, _, _, nh, hd = cache_k.shape
    bp, t_prompt, d = x_prompt.shape
    db, t_sample, _ = x_sample.shape
    n_conv, cw = conv_w.shape[:2]
    g_cnt, p_cnt = s5_a_re.shape[1:]
    assert db == SUBLANES, "the sample scan lays the decode batch along the vreg sublanes"
    assert (s5_b_re.shape[-1] * S5_LANE_GROUPS) == 128

    prm = dict(
        ffn1_wgu=ffn1_wgu.astype(BF16), ffn1_wd=ffn1_wd.astype(BF16),
        ffn2_wgu=ffn2_wgu.astype(BF16), ffn2_wd=ffn2_wd.astype(BF16),
        ln_g=ln_g, ln_b=ln_b, hyb_w_in=hyb_w_in.astype(BF16), hyb_w_out=hyb_w_out.astype(BF16),
        s5_a_re=s5_a_re, s5_glu_w=s5_glu_w.astype(BF16), s5_glu_b=s5_glu_b,
        conv_w_in=conv_w_in.astype(BF16), conv_w=conv_w.astype(F32), conv_w_out=conv_w_out.astype(BF16),
    )

    def s5_prms(seg_len):
        out = []
        for i in range(n_hyb):
            sp = _s5_discretise(s5_a_re[i], s5_a_im[i], s5_log_dt[i], s5_b_re[i], s5_b_im[i], s5_c_re[i],
                                s5_c_im[i], s5_d[i], seg_len)
            sp["seg_len"] = seg_len
            out.append(sp)
        return out

    prompt_seg = _pick_tile(t_prompt // SUBLANES, 32)
    kv = dict(nh=nh, hd=hd, cache_k=cache_k, cache_v=cache_v, page_table=page_table)
    conv_zero = jnp.zeros((n_conv, bp, cw - 1, d), x_prompt.dtype)
    y_p, y_s, o = _run_trunk(x_prompt, x_sample, prm, s5_prms(prompt_seg), s5_prms(t_sample), kv, state_s5_re,
                             state_s5_im, conv_zero, state_conv)
    return (y_p, y_s, o["kp"], o["vp"], o["hrp"], o["hip"], o["cvp"], o["ks"], o["vs"], o["hrs"], o["his"], o["cvs"])
```

```python
import functools

import jax
import jax.numpy as jnp
from jax import lax
from jax.experimental import pallas as pl
from jax.experimental.pallas import tpu as pltpu

MOBA_BLOCK = 256
MOBA_TOPK = 3
LN_EPS = 1e-5
LANES = 128
SUBLANES = 8
S5_LANE_GROUPS = 8
S5_SCAN_LANES = 512
S5_SCAN_UNROLL = 4
MAX_ROW_TILE = 640
VMEM_LIMIT_BYTES = 56 * 1024 * 1024

F32 = jnp.float32
BF16 = jnp.bfloat16
NEG_INF = float("-inf")


def _cparams(semantics):
    return pltpu.CompilerParams(dimension_semantics=semantics, vmem_limit_bytes=VMEM_LIMIT_BYTES)


def _layer_norm(y, g, b):
    mu = jnp.mean(y, axis=-1, keepdims=True)
    d = y - mu
    var = jnp.mean(d * d, axis=-1, keepdims=True)
    return d * lax.rsqrt(var + LN_EPS) * g + b


def _ffn_kernel(x32_ref, xb_ref, wg_ref, wu_ref, wd_ref, g_ref, b_ref, o32_ref, aux_ref, *, alpha, nf, rest):
    i, j = pl.program_id(0), pl.program_id(1)

    @pl.when(j == 0)
    def _():
        o32_ref[...] = jnp.zeros_like(o32_ref)

    xb = xb_ref[...]
    gate = jnp.dot(xb, wg_ref[...], preferred_element_type=F32)
    up = jnp.dot(xb, wu_ref[...], preferred_element_type=F32)
    h = (gate * jax.nn.sigmoid(gate)) * up
    o32_ref[...] += jnp.dot(h.astype(BF16), wd_ref[...], preferred_element_type=F32)

    @pl.when(j == nf - 1)
    def _():
        z = _layer_norm(alpha * x32_ref[...] + 0.5 * o32_ref[...], g_ref[...], b_ref[...])
        o32_ref[...] = z
        if rest is None:
            aux_ref[...] = z.astype(BF16)
        else:
            @pl.when(i == pl.num_programs(0) - 1)
            def _():
                aux_ref[...] = z[rest[0]:rest[0] + rest[1], :]


def _ffn(x32, xb, wgu, wd, layer, g, b, alpha, tm, tf, mp=None):
    m, d = x32.shape
    f = wd.shape[1]
    nf = f // tf
    if mp is None:
        rest = None
        out_specs = [pl.BlockSpec((tm, d), lambda i, j: (i, 0)), pl.BlockSpec((tm, d), lambda i, j: (i, 0))]
        out_shape = [jax.ShapeDtypeStruct((m, d), F32), jax.ShapeDtypeStruct((m, d), BF16)]
    else:
        rest = _rest_rows(m, mp, tm)
        out_specs = [pl.BlockSpec((tm, d), lambda i, j: (i, 0)), pl.BlockSpec((rest[1], d), lambda i, j: (0, 0))]
        out_shape = [jax.ShapeDtypeStruct((mp, d), F32), jax.ShapeDtypeStruct((rest[1], d), F32)]
    return pl.pallas_call(
        functools.partial(_ffn_kernel, alpha=alpha, nf=nf, rest=rest),
        grid=(m // tm, nf),
        in_specs=[
            pl.BlockSpec((tm, d), lambda i, j: (i, 0)),
            pl.BlockSpec((tm, d), lambda i, j: (i, 0)),
            pl.BlockSpec((None, d, tf), lambda i, j: (layer, 0, j)),
            pl.BlockSpec((None, d, tf), lambda i, j: (layer, 0, j + nf)),
            pl.BlockSpec((None, tf, d), lambda i, j: (layer, j, 0)),
            pl.BlockSpec((1, d), lambda i, j: (0, 0)),
            pl.BlockSpec((1, d), lambda i, j: (0, 0)),
        ],
        out_specs=out_specs,
        out_shape=out_shape,
        compiler_params=_cparams(("parallel", "arbitrary")),
        name="ffn_ln",
    )(x32, xb, wgu, wgu, wd, g, b)


def _mm_kernel(x_ref, w_ref, o_ref):
    o_ref[...] = jnp.dot(x_ref[...], w_ref[...], preferred_element_type=F32)


def _mm(xb, w, layer, col0, n, tm, tn):
    m, k = xb.shape
    assert col0 % tn == 0 and n % tn == 0
    jb0 = col0 // tn
    return pl.pallas_call(
        _mm_kernel,
        grid=(m // tm, n // tn),
        in_specs=[
            pl.BlockSpec((tm, k), lambda i, j: (i, 0)),
            pl.BlockSpec((None, k, tn), lambda i, j: (layer, 0, jb0 + j)),
        ],
        out_specs=pl.BlockSpec((tm, tn), lambda i, j: (i, j)),
        out_shape=jax.ShapeDtypeStruct((m, n), F32),
        compiler_params=_cparams(("parallel", "arbitrary")),
        name="proj",
    )(xb, w)


def _rest_rows(m_all, mp, tm):
    first = mp - (m_all // tm - 1) * tm
    assert m_all % tm == 0 and 0 <= first and first + (m_all - mp) == tm
    return first, m_all - mp


def _hyb_in_kernel(x_ref, xr_ref, w_ref, *refs, n_alias):
    qp_ref, qr_ref, kp_ref, kr_ref, vp_ref, vr_ref, up_ref, ur_ref = refs[n_alias:]
    i, j = pl.program_id(0), pl.program_id(1)
    last = i == pl.num_programs(0) - 1
    w = w_ref[...]
    width = w.shape[1] // 2
    y = jnp.dot(x_ref[...], w, preferred_element_type=F32)

    def emit(lo_ref, hi_ref, lo_rest_ref, hi_rest_ref):
        lo_ref[...] = y[:, :width]
        hi_ref[...] = y[:, width:]

        @pl.when(last)
        def _():
            yr = jnp.dot(xr_ref[...], w, preferred_element_type=F32)
            lo_rest_ref[...] = yr[:, :width]
            hi_rest_ref[...] = yr[:, width:]

    @pl.when(j == 0)
    def _():
        emit(qp_ref, kp_ref, qr_ref, kr_ref)

    @pl.when(j == 1)
    def _():
        emit(vp_ref, up_ref, vr_ref, ur_ref)


def _hyb_in(xb, w, layer, mp, k_stack=None, v_stack=None):
    m, k = xb.shape
    n_layers = w.shape[0]
    width = w.shape[2] // 4
    n_rest = m - mp
    assert mp % n_rest == 0
    tm = _pick_tile(mp, 512)
    aliased = [] if k_stack is None else [k_stack, v_stack]
    prompt_spec = pl.BlockSpec((tm, width), lambda i, j: (i, 0))
    stack_spec = pl.BlockSpec((None, tm, width), lambda i, j: (layer, i, 0))
    rest_spec = pl.BlockSpec((n_rest, width), lambda i, j: (0, 0))
    prompt_shape = jax.ShapeDtypeStruct((mp, width), F32)
    stack_shape = jax.ShapeDtypeStruct((n_layers, mp, width), F32)
    rest_shape = jax.ShapeDtypeStruct((n_rest, width), F32)
    qp, qr, kp, kr, vp, vr, up, ur = pl.pallas_call(
        functools.partial(_hyb_in_kernel, n_alias=len(aliased)),
        grid=(mp // tm, 2),
        in_specs=[
            pl.BlockSpec((tm, k), lambda i, j: (i, 0)),
            pl.BlockSpec((n_rest, k), lambda i, j: (mp // n_rest, 0)),
            pl.BlockSpec((None, k, 2 * width), lambda i, j: (layer, 0, j)),
        ] + [pl.BlockSpec(memory_space=pl.ANY)] * len(aliased),
        out_specs=[prompt_spec, rest_spec, stack_spec, rest_spec, stack_spec, rest_spec, prompt_spec, rest_spec],
        out_shape=[prompt_shape, rest_shape, stack_shape, rest_shape, stack_shape, rest_shape, prompt_shape,
                   rest_shape],
        input_output_aliases={3: 2, 4: 4} if aliased else {},
        compiler_params=_cparams(("parallel", "arbitrary")),
        name="hyb_in_proj",
    )(xb, xb, w, *aliased)
    return (qp, qr), (kp, kr), (vp, vr), (up, ur)


def _mm_res_ln_kernel(*refs, nparts, alpha):
    a_refs = refs[:nparts]
    w_refs = refs[nparts:2 * nparts]
    x32_ref, g_ref, b_ref, o32_ref, ob_ref = refs[2 * nparts:]
    y = alpha * x32_ref[...]
    for a_ref, w_ref in zip(a_refs, w_refs):
        y = y + jnp.dot(a_ref[...], w_ref[...], preferred_element_type=F32)
    z = _layer_norm(y, g_ref[...], b_ref[...])
    o32_ref[...] = z
    ob_ref[...] = z.astype(BF16)


def _mm_res_ln(parts, w, layer, x32, g, b, alpha, tm):
    m, d = x32.shape
    kp = parts[0].shape[1]
    assert all(p.shape[1] == kp for p in parts)
    in_specs = [pl.BlockSpec((tm, kp), lambda i: (i, 0)) for _ in parts]
    in_specs += [pl.BlockSpec((None, kp, d), lambda i, r=r: (layer, r, 0)) for r in range(len(parts))]
    in_specs += [
        pl.BlockSpec((tm, d), lambda i: (i, 0)),
        pl.BlockSpec((1, d), lambda i: (0, 0)),
        pl.BlockSpec((1, d), lambda i: (0, 0)),
    ]
    return pl.pallas_call(
        functools.partial(_mm_res_ln_kernel, nparts=len(parts), alpha=alpha),
        grid=(m // tm,),
        in_specs=in_specs,
        out_specs=[pl.BlockSpec((tm, d), lambda i: (i, 0)), pl.BlockSpec((tm, d), lambda i: (i, 0))],
        out_shape=[jax.ShapeDtypeStruct((m, d), F32), jax.ShapeDtypeStruct((m, d), BF16)],
        compiler_params=_cparams(("parallel",)),
        name="out_proj_ln",
    )(*parts, *([w] * len(parts)), x32, g, b)


def _glu_kernel(y_ref, w_ref, b_ref, o_ref):
    y = y_ref[...]
    z = jnp.dot(y.astype(BF16), w_ref[...], preferred_element_type=F32) + b_ref[...]
    o_ref[...] = (y * jax.nn.sigmoid(z)).astype(o_ref.dtype)


def _glu(y, w, layer, b, tm):
    m, n = y.shape
    return pl.pallas_call(
        _glu_kernel,
        grid=(m // tm,),
        in_specs=[
            pl.BlockSpec((tm, n), lambda i: (i, 0)),
            pl.BlockSpec((None, n, n), lambda i: (layer, 0, 0)),
            pl.BlockSpec((1, n), lambda i: (0, 0)),
        ],
        out_specs=pl.BlockSpec((tm, n), lambda i: (i, 0)),
        out_shape=jax.ShapeDtypeStruct((m, n), BF16),
        compiler_params=_cparams(("parallel",)),
        name="s5_glu",
    )(y, w, b)


CONV_PAD = 8


def _conv_tile(bg, cg, v, st_ref, w_ref, c_ref, ns_ref, hbuf, *, tt, nt, cw):
    t = pl.program_id(2)
    lo = CONV_PAD - (cw - 1)

    @pl.when(t == 0)
    def _():
        hbuf[lo:CONV_PAD, :] = st_ref[...]

    hbuf[CONV_PAD:CONV_PAD + tt, :] = cg * v
    w = w_ref[...]
    conv = w[0:1, :] * hbuf[lo:lo + tt, :]
    for j in range(1, cw):
        conv = conv + w[j:j + 1, :] * hbuf[lo + j:lo + j + tt, :]
    c_ref[...] = (bg * conv).astype(c_ref.dtype)
    tail = hbuf[lo + tt:CONV_PAD + tt, :]

    @pl.when(t == nt - 1)
    def _():
        ns_ref[...] = tail

    hbuf[lo:CONV_PAD, :] = tail


def _conv_kernel(bg_ref, cg_ref, v_ref, st_ref, w_ref, c_ref, ns_ref, hbuf, **kw):
    _conv_tile(bg_ref[...], cg_ref[...], v_ref[...], st_ref, w_ref, c_ref, ns_ref, hbuf, **kw)


def _conv_in_kernel(x_ref, wb_ref, wc_ref, wv_ref, st_ref, w_ref, c_ref, ns_ref, hbuf, **kw):
    x = x_ref[...]
    bg, cg, v = (jnp.dot(x, w[...], preferred_element_type=F32) for w in (wb_ref, wc_ref, wv_ref))
    _conv_tile(bg, cg, v, st_ref, w_ref, c_ref, ns_ref, hbuf, **kw)


def _conv_core(z, state, conv_w, layer, bsz, t_len, tt, tn):
    m, d3 = z.shape
    d = d3 // 3
    cw = conv_w.shape[1]
    nt, nj = t_len // tt, d // tn
    return pl.pallas_call(
        functools.partial(_conv_kernel, tt=tt, nt=nt, cw=cw),
        grid=(bsz, nj, nt),
        in_specs=[
            pl.BlockSpec((tt, tn), lambda b, j, t: (b * nt + t, j)),
            pl.BlockSpec((tt, tn), lambda b, j, t: (b * nt + t, j + nj)),
            pl.BlockSpec((tt, tn), lambda b, j, t: (b * nt + t, j + 2 * nj)),
            pl.BlockSpec((None, cw - 1, tn), lambda b, j, t: (b, 0, j)),
            pl.BlockSpec((None, cw, tn), lambda b, j, t: (layer, 0, j)),
        ],
        out_specs=[
            pl.BlockSpec((tt, tn), lambda b, j, t: (b * nt + t, j)),
            pl.BlockSpec((None, cw - 1, tn), lambda b, j, t: (b, 0, j)),
        ],
        out_shape=[jax.ShapeDtypeStruct((m, d), BF16), jax.ShapeDtypeStruct((bsz, cw - 1, d), F32)],
        scratch_shapes=[pltpu.VMEM((CONV_PAD + tt, tn), F32)],
        compiler_params=_cparams(("parallel", "parallel", "arbitrary")),
        name="conv_core",
    )(z, z, z, state, conv_w)


def _conv_in_core(xb, w_in, state, conv_w, layer, bsz, t_len, tt, tn):
    m, d = xb.shape
    cw = conv_w.shape[1]
    nt, nj = t_len // tt, d // tn
    return pl.pallas_call(
        functools.partial(_conv_in_kernel, tt=tt, nt=nt, cw=cw),
        grid=(bsz, nj, nt),
        in_specs=[
            pl.BlockSpec((tt, d), lambda b, j, t: (b * nt + t, 0)),
            pl.BlockSpec((None, d, tn), lambda b, j, t: (layer, 0, j)),
            pl.BlockSpec((None, d, tn), lambda b, j, t: (layer, 0, j + nj)),
            pl.BlockSpec((None, d, tn), lambda b, j, t: (layer, 0, j + 2 * nj)),
            pl.BlockSpec((None, cw - 1, tn), lambda b, j, t: (b, 0, j)),
            pl.BlockSpec((None, cw, tn), lambda b, j, t: (layer, 0, j)),
        ],
        out_specs=[
            pl.BlockSpec((tt, tn), lambda b, j, t: (b * nt + t, j)),
            pl.BlockSpec((None, cw - 1, tn), lambda b, j, t: (b, 0, j)),
        ],
        out_shape=[jax.ShapeDtypeStruct((m, d), BF16), jax.ShapeDtypeStruct((bsz, cw - 1, d), F32)],
        scratch_shapes=[pltpu.VMEM((CONV_PAD + tt, tn), F32)],
        compiler_params=_cparams(("parallel", "parallel", "arbitrary")),
        name="conv_in_core",
    )(xb, w_in, w_in, w_in, state, conv_w)


def _moba_select_rows(gate_t, n_valid):
    nb, nq = gate_t.shape
    sub = lax.broadcasted_iota(jnp.int32, (nb, nq), 0)
    valid = sub < n_valid
    gate_t = jnp.where(valid, gate_t, NEG_INF)
    beaten_by = jnp.zeros((nb, nq), jnp.int32)
    for jp in range(nb):
        row = gate_t[jp:jp + 1, :]
        beats = (row > gate_t) | ((row == gate_t) & (sub > jp))
        beaten_by = beaten_by + jnp.where(beats, 1, 0)
    return valid & (beaten_by < MOBA_TOPK)


def _moba_prompt_kernel(q_ref, k_ref, v_ref, o_ref, kb_scr, vt_scr, km_scr, *, nb, scale):
    blk = MOBA_BLOCK
    kb_scr[...] = k_ref[...].astype(BF16)
    for j in range(nb):
        rows = slice(j * blk, (j + 1) * blk)
        km_scr[j:j + 1, :] = jnp.mean(k_ref[rows, :], axis=0, keepdims=True)
        vt_scr[:, rows] = v_ref[rows, :].T.astype(BF16)
    causal = lax.broadcasted_iota(jnp.int32, (blk, blk), 0) <= lax.broadcasted_iota(jnp.int32, (blk, blk), 1)
    causal_bias = jnp.where(causal, 0.0, NEG_INF).astype(F32)

    for qi in range(nb):
        n_keys = (qi + 1) * blk
        q = q_ref[qi * blk:(qi + 1) * blk, :]
        s_all = lax.dot_general(kb_scr[0:n_keys, :], q.astype(BF16), (((1,), (1,)), ((), ())),
                                preferred_element_type=F32) * scale
        if qi > 0:
            gate_t = lax.dot_general(km_scr[...], q, (((1,), (1,)), ((), ())),
                                     precision=lax.Precision.HIGHEST, preferred_element_type=F32)
            sel_bias = jnp.where(_moba_select_rows(gate_t, qi), 0.0, NEG_INF).astype(F32)
        s_blocks = [s_all[j * blk:(j + 1) * blk, :] + sel_bias[j:j + 1, :] for j in range(qi)]
        s_blocks.append(s_all[qi * blk:n_keys, :] + causal_bias)
        top = s_blocks[0]
        for sb in s_blocks[1:]:
            top = jnp.maximum(top, sb)
        m = jnp.max(top, axis=0, keepdims=True)
        p_blocks = [jnp.exp(sb - m) for sb in s_blocks]
        tot = p_blocks[0]
        for pb in p_blocks[1:]:
            tot = tot + pb
        l = jnp.sum(tot, axis=0, keepdims=True)
        p_all = jnp.concatenate([pb.astype(BF16) for pb in p_blocks], axis=0)
        acc = jnp.dot(vt_scr[:, 0:n_keys], p_all, preferred_element_type=F32)
        o_ref[qi * blk:(qi + 1) * blk, :] = (acc / l).T.astype(o_ref.dtype)


def _moba_prompt(q, k_stack, v_stack, layer, bsz, t_len, nh, hd, m_out):
    m = m_out
    assert t_len % MOBA_BLOCK == 0
    nb = t_len // MOBA_BLOCK
    seq_spec = pl.BlockSpec((t_len, hd), lambda b, h: (b, h))
    kv_spec = pl.BlockSpec((None, t_len, hd), lambda b, h: (layer, b, h))
    return pl.pallas_call(
        functools.partial(_moba_prompt_kernel, nb=nb, scale=hd ** -0.5),
        grid=(bsz, nh),
        in_specs=[seq_spec, kv_spec, kv_spec],
        out_specs=seq_spec,
        out_shape=jax.ShapeDtypeStruct((m, nh * hd), BF16),
        scratch_shapes=[pltpu.VMEM((t_len, hd), BF16), pltpu.VMEM((hd, t_len), BF16),
                        pltpu.VMEM((nb, hd), F32)],
        compiler_params=_cparams(("parallel", "parallel")),
        name="moba_prompt",
    )(q, k_stack, v_stack)


def _moba_select_grouped(gate, n_valid_blocks, group):
    rows, nl = gate.shape
    lane = lax.broadcasted_iota(jnp.int32, (rows, nl), 1)
    valid = lane < n_valid_blocks * group
    gate = jnp.where(valid, gate, NEG_INF)
    beaten_by = jnp.zeros((rows, nl), jnp.int32)
    for jp in range(0, nl, group):
        col = gate[:, jp:jp + 1]
        beats = (col > gate) | ((col == gate) & (lane >= jp + group))
        beaten_by = beaten_by + jnp.where(beats, 1, 0)
    return valid & (beaten_by < MOBA_TOPK)


def _fold_lanes(x, op):
    tiles = [x[:, i:i + LANES] for i in range(0, x.shape[1], LANES)]
    out = tiles[0]
    for t in tiles[1:]:
        out = op(out, t)
    return out


def _sample_attn_kernel(pt_ref, q_ref, kn_ref, vn_ref, hbias_ref, obias_ref, *refs, nh, n_past_blocks,
                        pages_per_step, pages_per_block, scale):
    del pt_ref
    kp_refs = refs[:pages_per_step]
    vp_refs = refs[pages_per_step:2 * pages_per_step]
    o_ref, gate_scr, m_scr, l_scr, acc_scr = refs[2 * pages_per_step:]
    step = pl.program_id(1)
    n_steps = pl.num_programs(1)
    rows, hd = q_ref.shape
    n_pages = m_scr.shape[1]
    q = q_ref[...]
    qb = q.astype(BF16)
    q3 = q.reshape(rows // nh, nh, hd)
    lane = lax.broadcasted_iota(jnp.int32, (rows, n_pages), 1)

    def put_column(scr, pg, col):
        scr[...] = jnp.where(lane == pg, col, scr[...])

    for blk0 in range(0, pages_per_step, pages_per_block):
        ksum = jnp.zeros((nh, hd), F32)
        for r in range(blk0, blk0 + pages_per_block):
            pg = step * pages_per_step + r
            k = kp_refs[r][...]
            ksum = ksum + jnp.sum(k.reshape(k.shape[0] // nh, nh, hd), axis=0)
            s = lax.dot_general(qb, k.astype(BF16), (((1,), (1,)), ((), ())), preferred_element_type=F32)
            s = s * scale + hbias_ref[...]
            m = jnp.max(_fold_lanes(s, jnp.maximum), axis=1, keepdims=True)
            p = jnp.exp(s - m)
            put_column(m_scr, pg, m)
            put_column(l_scr, pg, jnp.sum(_fold_lanes(p, jnp.add), axis=1, keepdims=True))
            acc_scr[pg] = jnp.dot(p.astype(BF16), vp_refs[r][...].astype(BF16), preferred_element_type=F32)
        kmean = ksum / MOBA_BLOCK
        gcol = jnp.sum(q3 * kmean[None, :, :], axis=2, keepdims=True).reshape(rows, 1)
        for r in range(blk0, blk0 + pages_per_block):
            put_column(gate_scr, step * pages_per_step + r, gcol)

    @pl.when(step == n_steps - 1)
    def _():
        keep = _moba_select_grouped(gate_scr[...], n_past_blocks, pages_per_block)
        m_pages = m_scr[...] + jnp.where(keep, 0.0, NEG_INF)
        s = lax.dot_general(qb, kn_ref[...].astype(BF16), (((1,), (1,)), ((), ())), preferred_element_type=F32)
        s = s * scale + obias_ref[...]
        m_own = jnp.max(s, axis=1, keepdims=True)
        m_tot = jnp.maximum(m_own, jnp.max(m_pages, axis=1, keepdims=True))
        p = jnp.exp(s - m_tot)
        w = jnp.exp(m_pages - m_tot)
        l = jnp.sum(p, axis=1, keepdims=True) + jnp.sum(w * l_scr[...], axis=1, keepdims=True)
        acc = jnp.dot(p.astype(BF16), vn_ref[...].astype(BF16), preferred_element_type=F32)
        for pg in range(n_pages):
            acc = acc + w[:, pg:pg + 1] * acc_scr[pg]
        o_ref[...] = (acc / l).astype(o_ref.dtype)


def _sample_attn(q, k_new, v_new, cache_k, cache_v, layer, page_table, db, t_len, nh, hd, pages_per_step=4):
    n_hyb, n_pool, page, _, _ = cache_k.shape
    n_pages = page_table.shape[1]
    past_len = n_pages * page
    assert past_len % MOBA_BLOCK == 0 and t_len <= MOBA_BLOCK and MOBA_BLOCK % page == 0
    assert nh & (nh - 1) == 0, "head matching uses bit masks"
    ppb = MOBA_BLOCK // page
    assert pages_per_step % ppb == 0 and n_pages % pages_per_step == 0
    rows = t_len * nh
    ck = cache_k.reshape(n_hyb, n_pool, page * nh, hd)
    cv = cache_v.reshape(n_hyb, n_pool, page * nh, hd)
    r = jnp.arange(rows, dtype=jnp.int32)[:, None]
    c = jnp.arange(page * nh, dtype=jnp.int32)[None, :]
    same_head = (r & (nh - 1)) == (c & (nh - 1))
    head_bias = jnp.where(same_head, 0.0, NEG_INF).astype(F32)
    own_bias = jnp.where(same_head[:, :rows] & (c[:, :rows] // nh <= r // nh), 0.0, NEG_INF).astype(F32)

    def page_spec(i):
        return pl.BlockSpec((None, None, page * nh, hd),
                            lambda b, s, pt: (layer, pt[b, s * pages_per_step + i], 0, 0))

    def row_spec(nrows):
        return pl.BlockSpec((None, nrows, hd), lambda b, s, pt: (b, 0, 0))

    full = lambda a: pl.BlockSpec(a.shape, lambda b, s, pt: (0,) * a.ndim)
    out = pl.pallas_call(
        functools.partial(_sample_attn_kernel, nh=nh, n_past_blocks=past_len // MOBA_BLOCK,
                          pages_per_step=pages_per_step, pages_per_block=ppb, scale=hd ** -0.5),
        grid_spec=pltpu.PrefetchScalarGridSpec(
            num_scalar_prefetch=1,
            grid=(db, n_pages // pages_per_step),
            in_specs=[row_spec(rows), row_spec(rows), row_spec(rows), full(head_bias), full(own_bias)]
            + [page_spec(i) for i in range(pages_per_step)] * 2,
            out_specs=row_spec(rows),
            scratch_shapes=[pltpu.VMEM((rows, n_pages), F32), pltpu.VMEM((rows, n_pages), F32),
                            pltpu.VMEM((rows, n_pages), F32), pltpu.VMEM((n_pages, rows, hd), F32)],
        ),
        out_shape=jax.ShapeDtypeStruct((db, rows, hd), BF16),
        compiler_params=_cparams(("parallel", "arbitrary")),
        name="sample_attn",
    )(page_table, q.reshape(db, rows, hd), k_new.reshape(db, rows, hd), v_new.reshape(db, rows, hd),
      head_bias, own_bias, *([ck] * pages_per_step), *([cv] * pages_per_step))
    return out.reshape(db * t_len, nh * hd)


def _s5_kernel(u_ref, perm_ref, unperm_ref, bre_ref, bim_ref, cre_ref, cimn_ref, lre_ref, lim_ref, llre_ref, llim_ref,
               d_ref, h0re_ref, h0im_ref, y_ref, htre_ref, htim_ref, hr_scr, hi_scr, cre_scr, cim_scr, yc_scr,
               *, seg_len, chained, n_chunks):
    c = pl.program_id(1)
    nseg = SUBLANES
    nstate = hr_scr.shape[1]
    n_kb = bre_ref.shape[0]
    ku = bre_ref.shape[1]
    ks = bre_ref.shape[2]

    up = jnp.dot(perm_ref[...], u_ref[...].astype(BF16), preferred_element_type=F32).astype(BF16)
    for kb in range(n_kb):
        ukb = up[:, kb * ku:(kb + 1) * ku]
        hr_scr[:, kb * ks:(kb + 1) * ks] = jnp.dot(ukb, bre_ref[kb], preferred_element_type=F32)
        hi_scr[:, kb * ks:(kb + 1) * ks] = jnp.dot(ukb, bim_ref[kb], preferred_element_type=F32)

    if chained:
        @pl.when(c == 0)
        def _():
            cre_scr[...] = jnp.zeros_like(cre_scr)
            cim_scr[...] = jnp.zeros_like(cim_scr)
            cre_scr[0:1, :] = h0re_ref[...]
            cim_scr[0:1, :] = h0im_ref[...]

    for lc in range(nstate // S5_SCAN_LANES):
        cols = slice(lc * S5_SCAN_LANES, (lc + 1) * S5_SCAN_LANES)
        lr = jnp.broadcast_to(lre_ref[:, cols], (nseg, S5_SCAN_LANES))
        li = jnp.broadcast_to(lim_ref[:, cols], (nseg, S5_SCAN_LANES))

        def advance(i, hr, hi):
            r = pl.multiple_of(i * nseg, nseg)
            nr = lr * hr - li * hi + hr_scr[pl.ds(r, nseg), cols]
            ni = lr * hi + li * hr + hi_scr[pl.ds(r, nseg), cols]
            return r, nr, ni

        def scan_only(i, carry):
            _, nr, ni = advance(i, *carry)
            return nr, ni

        def scan_store(i, carry):
            r, nr, ni = advance(i, *carry)
            hr_scr[pl.ds(r, nseg), cols] = nr
            hi_scr[pl.ds(r, nseg), cols] = ni
            return nr, ni

        if chained:
            er, ei = lax.fori_loop(0, seg_len, scan_only, (cre_scr[:, cols], cim_scr[:, cols]),
                                   unroll=S5_SCAN_UNROLL)
            llr, lli = llre_ref[:, cols], llim_ref[:, cols]
            fr, fi = er[0:1, :], ei[0:1, :]
            for j in range(1, nseg):
                cre_scr[j:j + 1, cols] = fr
                cim_scr[j:j + 1, cols] = fi
                if j < nseg - 1:
                    fr, fi = (er[j:j + 1, :] + (llr * fr - lli * fi), ei[j:j + 1, :] + (llr * fi + lli * fr))
            init = (cre_scr[:, cols], cim_scr[:, cols])
        else:
            init = (h0re_ref[:, cols], h0im_ref[:, cols])

        hr, hi = lax.fori_loop(0, seg_len, scan_store, init, unroll=S5_SCAN_UNROLL)

        if chained:
            last_r, last_i = hr[nseg - 1:nseg, :], hi[nseg - 1:nseg, :]
            cre_scr[:, cols] = jnp.zeros((nseg, S5_SCAN_LANES), F32)
            cim_scr[:, cols] = jnp.zeros((nseg, S5_SCAN_LANES), F32)
            cre_scr[0:1, cols] = last_r
            cim_scr[0:1, cols] = last_i

            @pl.when(c == n_chunks - 1)
            def _():
                htre_ref[:, cols] = last_r
                htim_ref[:, cols] = last_i
        else:
            htre_ref[:, cols] = hr
            htim_ref[:, cols] = hi

    for kb in range(n_kb):
        hre = hr_scr[:, kb * ks:(kb + 1) * ks].astype(BF16)
        him = hi_scr[:, kb * ks:(kb + 1) * ks].astype(BF16)
        ykb = (jnp.dot(hre, cre_ref[kb], preferred_element_type=F32)
               + jnp.dot(him, cimn_ref[kb], preferred_element_type=F32))
        yc_scr[:, kb * ku:(kb + 1) * ku] = ykb

    yc = yc_scr[...]
    hi = yc.astype(BF16)
    r1 = yc - hi.astype(F32)
    mid = r1.astype(BF16)
    lo = (r1 - mid.astype(F32)).astype(BF16)
    unperm = unperm_ref[...]
    y = (jnp.dot(unperm, hi, preferred_element_type=F32) + jnp.dot(unperm, mid, preferred_element_type=F32)
         + jnp.dot(unperm, lo, preferred_element_type=F32))
    y_ref[...] = jax.nn.gelu(y + d_ref[...] * u_ref[...])


def _s5_discretise(a_re, a_im, log_dt, b_re, b_im, c_re, c_im, d_skip, seg_len):
    g, p = a_re.shape
    ch = b_re.shape[-1]
    lam = lax.complex(a_re.astype(F32), a_im.astype(F32))
    dt = jnp.exp(log_dt.astype(F32))[:, None]
    lam_bar = jnp.exp(lam * dt)
    b_bar = ((lam_bar - 1.0) / lam)[..., None] * lax.complex(b_re.astype(F32), b_im.astype(F32))
    lam_seg = lam_bar
    assert seg_len & (seg_len - 1) == 0
    for _ in range(seg_len.bit_length() - 1):
        lam_seg = lam_seg * lam_seg
    gl = S5_LANE_GROUPS
    n_kb = g // gl
    eye = jnp.eye(gl, dtype=F32)

    def pack_b(x):
        x = x.reshape(n_kb, gl, p, ch).transpose(0, 1, 3, 2)
        return jnp.einsum("kgcp,gh->kgchp", x, eye).reshape(n_kb, gl * ch, gl * p)

    def pack_c(x):
        x = x.reshape(n_kb, gl, ch, p).transpose(0, 1, 3, 2)
        return jnp.einsum("kgpc,gh->kgphc", x, eye).reshape(n_kb, gl * p, gl * ch)

    flat = lambda x: x.reshape(1, g * p)
    return dict(
        bre=pack_b(jnp.real(b_bar)).astype(BF16), bim=pack_b(jnp.imag(b_bar)).astype(BF16),
        cre=pack_c(c_re.astype(F32)).astype(BF16), cimn=pack_c(-c_im.astype(F32)).astype(BF16),
        lre=flat(jnp.real(lam_bar)), lim=flat(jnp.imag(lam_bar)),
        llre=flat(jnp.real(lam_seg)), llim=flat(jnp.imag(lam_seg)),
        d=d_skip.astype(F32).reshape(1, g * ch),
    )


def _s5(u, h0_re, h0_im, prm, seg_len, chained, m_out=None):
    n_rows, w = u.shape
    m = n_rows if m_out is None else m_out
    gp = prm["lre"].shape[1]
    chunk = SUBLANES * seg_len
    if chained:
        nseq = h0_re.shape[0]
        n_chunks = n_rows // nseq // chunk
        grid = (nseq, n_chunks)
        u_spec = pl.BlockSpec((chunk, w), lambda b, c: (b * n_chunks + c, 0))
        h_spec = pl.BlockSpec((None, 1, gp), lambda b, c: (b, 0, 0))
        h_shape = jax.ShapeDtypeStruct((nseq, 1, gp), F32)
    else:
        assert n_rows == chunk
        n_chunks = 1
        grid = (1, 1)
        u_spec = pl.BlockSpec((chunk, w), lambda b, c: (0, 0))
        h_spec = pl.BlockSpec((SUBLANES, gp), lambda b, c: (0, 0))
        h_shape = jax.ShapeDtypeStruct((SUBLANES, gp), F32)
    full = lambda a: pl.BlockSpec(a.shape, lambda b, c: (0,) * a.ndim)
    names = ("bre", "bim", "cre", "cimn", "lre", "lim", "llre", "llim", "d")
    r = jnp.arange(chunk)
    perm = (((r % SUBLANES) * seg_len + r // SUBLANES)[:, None] == r[None, :]).astype(BF16)
    consts = [perm, perm.T]
    return pl.pallas_call(
        functools.partial(_s5_kernel, seg_len=seg_len, chained=chained, n_chunks=n_chunks),
        grid=grid,
        in_specs=[u_spec] + [full(a) for a in consts] + [full(prm[n]) for n in names] + [h_spec, h_spec],
        out_specs=[u_spec, h_spec, h_spec],
        out_shape=[jax.ShapeDtypeStruct((m, w), F32), h_shape, h_shape],
        scratch_shapes=[pltpu.VMEM((chunk, gp), F32), pltpu.VMEM((chunk, gp), F32),
                        pltpu.VMEM((SUBLANES, gp), F32), pltpu.VMEM((SUBLANES, gp), F32),
                        pltpu.VMEM((chunk, w), F32)],
        compiler_params=_cparams(("parallel", "arbitrary")),
        name="s5_scan",
    )(u, *consts, *[prm[n] for n in names], h0_re, h0_im)


def _pick_tile(n, target):
    t = min(n, target)
    while n % t:
        t //= 2
    return t


def _row_tile(m):
    return min(range(LANES, MAX_ROW_TILE + 1, LANES), key=lambda t: (-(-m // t) * t, -t))


def _fill_rest(full, sample_rows, mp):
    pad = full.shape[0] - mp - sample_rows.shape[0]
    rest = jnp.concatenate([sample_rows.astype(full.dtype), jnp.zeros((pad, full.shape[1]), full.dtype)], axis=0)
    return lax.dynamic_update_slice(full, rest, (mp, 0))


def _run_trunk(x_prompt, x_sample, prm, s5_prompt, s5_sample, kv, s5_re0, s5_im0, conv0_prompt, conv0_sample):
    bp, tp, d = x_prompt.shape
    bs, ts, _ = x_sample.shape
    mp, ms = bp * tp, bs * ts
    tm = _row_tile(mp + ms)
    m_all = -(-(mp + ms) // tm) * tm
    depth = prm["ffn1_wgu"].shape[0]
    alpha = (2.0 * depth) ** 0.25
    nh, hd = kv["nh"], kv["hd"]
    moba_w = nh * hd
    g_cnt, p_cnt = prm["s5_a_re"].shape[1:]
    gp = g_cnt * p_cnt
    tf = _pick_tile(prm["ffn1_wd"].shape[1], 512)
    sample = slice(mp, mp + ms)

    x32 = jnp.concatenate([x_prompt.reshape(mp, d).astype(F32), x_sample.reshape(ms, d).astype(F32),
                           jnp.zeros((m_all - mp - ms, d), F32)], axis=0)
    xb = x32.astype(BF16)
    row = lambda a: a.reshape(1, -1).astype(F32)
    out = dict(hrp=[], hip=[], cvp=[], ks=[], vs=[], hrs=[], his=[], cvs=[])
    k_stack = v_stack = None
    for layer in range(depth):
        i = layer // 2
        g, b = prm["ln_g"][layer], prm["ln_b"][layer]
        x32, xb = _ffn(x32, xb, prm["ffn1_wgu"], prm["ffn1_wd"], layer, row(g[0]), row(b[0]), alpha, tm, tf)
        if layer % 2 == 0:
            assert d - moba_w == moba_w
            (q, q_r), (k_stack, k_rest), (v_stack, v_rest), (u, u_r) = _hyb_in(xb, prm["hyb_w_in"], i, mp, k_stack,
                                                                               v_stack)
            attn = _moba_prompt(q, k_stack, v_stack, i, bp, tp, nh, hd, m_all)
            attn_s = _sample_attn(q_r[:ms], k_rest[:ms], v_rest[:ms], kv["cache_k"], kv["cache_v"], i,
                                  kv["page_table"], bs, ts, nh, hd)
            zero = jnp.zeros((bp, 1, gp), F32)
            y, hr_p, hi_p = _s5(u, zero, zero, s5_prompt[i], s5_prompt[i]["seg_len"], True, m_out=m_all)
            y_s, hr_s, hi_s = _s5(u_r[:ms], s5_re0[i].reshape(bs, gp), s5_im0[i].reshape(bs, gp), s5_sample[i],
                                  s5_sample[i]["seg_len"], False)
            ssm = _glu(_fill_rest(y, y_s, mp), prm["s5_glu_w"], i, row(prm["s5_glu_b"][i]), tm)
            x32, xb = _mm_res_ln([_fill_rest(attn, attn_s, mp), ssm], prm["hyb_w_out"], i, x32, row(g[1]),
                                 row(b[1]), alpha, tm)
            out["ks"].append(k_rest[:ms].reshape(bs, ts, nh, hd))
            out["vs"].append(v_rest[:ms].reshape(bs, ts, nh, hd))
            out["hrp"].append(hr_p.reshape(bp, g_cnt, p_cnt))
            out["hip"].append(hi_p.reshape(bp, g_cnt, p_cnt))
            out["hrs"].append(hr_s.reshape(bs, g_cnt, p_cnt))
            out["his"].append(hi_s.reshape(bs, g_cnt, p_cnt))
        else:
            c, cv_p = _conv_in_core(xb, prm["conv_w_in"], conv0_prompt[i], prm["conv_w"], i, bp, tp,
                                    _pick_tile(tp, 512), _pick_tile(d, 512))
            z = _mm(xb[sample], prm["conv_w_in"], i, 0, 3 * d, ms, _pick_tile(d, 2048))
            c_s, cv_s = _conv_core(z, conv0_sample[i], prm["conv_w"], i, bs, ts, ts, _pick_tile(d, 512))
            x32, xb = _mm_res_ln([_fill_rest(c, c_s, mp)], prm["conv_w_out"], i, x32, row(g[1]), row(b[1]),
                                 alpha, tm)
            out["cvp"].append(cv_p)
            out["cvs"].append(cv_s)
        if layer < depth - 1:
            x32, xb = _ffn(x32, xb, prm["ffn2_wgu"], prm["ffn2_wd"], layer, row(g[2]), row(b[2]), alpha, tm, tf)
        else:
            y_prompt, y_rest = _ffn(x32, xb, prm["ffn2_wgu"], prm["ffn2_wd"], layer, row(g[2]), row(b[2]), alpha,
                                    tm, tf, mp=mp)
    out = {n: jnp.stack(a) for n, a in out.items()}
    n_hyb = k_stack.shape[0]
    out["kp"] = k_stack.reshape(n_hyb, bp, tp, nh, hd)
    out["vp"] = v_stack.reshape(n_hyb, bp, tp, nh, hd)
    return y_prompt.reshape(bp, tp, d), y_rest[:ms].reshape(bs, ts, d), out


def kernel(x_prompt, x_sample, cache_k, cache_v, state_s5_re, state_s5_im, state_conv, page_table, ffn1_wgu, ffn1_wd, ffn2_wgu, ffn2_wd, ln_g, ln_b, hyb_w_in, hyb_w_out, s5_a_re, s5_a_im, s5_log_dt, s5_b_re, s5_b_im, s5_c_re, s5_c_im, s5_d, s5_glu_w, s5_glu_b, conv_w_in, conv_w, conv_w_out):
    n_hyb, _, _, nh, hd = cache_k.shape
    bp, t_prompt, d = x_prompt.shape
    db, t_sample, _ = x_sample.shape
    n_conv, cw = conv_w.shape[:2]
    g_cnt, p_cnt = s5_a_re.shape[1:]
    assert db == SUBLANES, "the sample scan lays the decode batch along the vreg sublanes"
    assert (s5_b_re.shape[-1] * S5_LANE_GROUPS) == 128

    prm = dict(
        ffn1_wgu=ffn1_wgu.astype(BF16), ffn1_wd=ffn1_wd.astype(BF16),
        ffn2_wgu=ffn2_wgu.astype(BF16), ffn2_wd=ffn2_wd.astype(BF16),
        ln_g=ln_g, ln_b=ln_b, hyb_w_in=hyb_w_in.astype(BF16), hyb_w_out=hyb_w_out.astype(BF16),
        s5_a_re=s5_a_re, s5_glu_w=s5_glu_w.astype(BF16), s5_glu_b=s5_glu_b,
        conv_w_in=conv_w_in.astype(BF16), conv_w=conv_w.astype(F32), conv_w_out=conv_w_out.astype(BF16),
    )

    def s5_prms(seg_len):
        out = []
        for i in range(n_hyb):
            sp = _s5_discretise(s5_a_re[i], s5_a_im[i], s5_log_dt[i], s5_b_re[i], s5_b_im[i], s5_c_re[i],
                                s5_c_im[i], s5_d[i], seg_len)
            sp["seg_len"] = seg_len
            out.append(sp)
        return out

    prompt_seg = _pick_tile(t_prompt // SUBLANES, 32)
    kv = dict(nh=nh, hd=hd, cache_k=cache_k, cache_v=cache_v, page_table=page_table)
    conv_zero = jnp.zeros((n_conv, bp, cw - 1, d), x_prompt.dtype)
    y_p, y_s, o = _run_trunk(x_prompt, x_sample, prm, s5_prms(prompt_seg), s5_prms(t_sample), kv, state_s5_re,
                             state_s5_im, conv_zero, state_conv)
    return (y_p, y_s, o["kp"], o["vp"], o["hrp"], o["hip"], o["cvp"], o["ks"], o["vs"], o["hrs"], o["his"], o["cvs"])
```

```python
import functools

import jax
import jax.numpy as jnp
from jax import lax
from jax.experimental import pallas as pl
from jax.experimental.pallas import tpu as pltpu

MOBA_BLOCK = 256
MOBA_TOPK = 3
LN_EPS = 1e-5
LANES = 128
SUBLANES = 8
S5_LANE_GROUPS = 8
S5_SCAN_LANES = 512
S5_SCAN_UNROLL = 4
MAX_ROW_TILE = 640
VMEM_LIMIT_BYTES = 56 * 1024 * 1024

F32 = jnp.float32
BF16 = jnp.bfloat16
NEG_INF = float("-inf")


def _cparams(semantics):
    return pltpu.CompilerParams(dimension_semantics=semantics, vmem_limit_bytes=VMEM_LIMIT_BYTES)


def _layer_norm(y, g, b):
    mu = jnp.mean(y, axis=-1, keepdims=True)
    d = y - mu
    var = jnp.mean(d * d, axis=-1, keepdims=True)
    return d * lax.rsqrt(var + LN_EPS) * g + b


def _ffn_kernel(x32_ref, xb_ref, wg_ref, wu_ref, wd_ref, g_ref, b_ref, o32_ref, aux_ref, *, alpha, nf, rest):
    i, j = pl.program_id(0), pl.program_id(1)

    @pl.when(j == 0)
    def _():
        o32_ref[...] = jnp.zeros_like(o32_ref)

    xb = xb_ref[...]
    gate = jnp.dot(xb, wg_ref[...], preferred_element_type=F32)
    up = jnp.dot(xb, wu_ref[...], preferred_element_type=F32)
    h = (gate * jax.nn.sigmoid(gate)) * up
    o32_ref[...] += jnp.dot(h.astype(BF16), wd_ref[...], preferred_element_type=F32)

    @pl.when(j == nf - 1)
    def _():
        z = _layer_norm(alpha * x32_ref[...] + 0.5 * o32_ref[...], g_ref[...], b_ref[...])
        o32_ref[...] = z
        if rest is None:
            aux_ref[...] = z.astype(BF16)
        else:
            @pl.when(i == pl.num_programs(0) - 1)
            def _():
                aux_ref[...] = z[rest[0]:rest[0] + rest[1], :]


def _ffn(x32, xb, wgu, wd, layer, g, b, alpha, tm, tf, mp=None):
    m, d = x32.shape
    f = wd.shape[1]
    nf = f // tf
    if mp is None:
        rest = None
        out_specs = [pl.BlockSpec((tm, d), lambda i, j: (i, 0)), pl.BlockSpec((tm, d), lambda i, j: (i, 0))]
        out_shape = [jax.ShapeDtypeStruct((m, d), F32), jax.ShapeDtypeStruct((m, d), BF16)]
    else:
        rest = _rest_rows(m, mp, tm)
        out_specs = [pl.BlockSpec((tm, d), lambda i, j: (i, 0)), pl.BlockSpec((rest[1], d), lambda i, j: (0, 0))]
        out_shape = [jax.ShapeDtypeStruct((mp, d), F32), jax.ShapeDtypeStruct((rest[1], d), F32)]
    return pl.pallas_call(
        functools.partial(_ffn_kernel, alpha=alpha, nf=nf, rest=rest),
        grid=(m // tm, nf),
        in_specs=[
            pl.BlockSpec((tm, d), lambda i, j: (i, 0)),
            pl.BlockSpec((tm, d), lambda i, j: (i, 0)),
            pl.BlockSpec((None, d, tf), lambda i, j: (layer, 0, j)),
            pl.BlockSpec((None, d, tf), lambda i, j: (layer, 0, j + nf)),
            pl.BlockSpec((None, tf, d), lambda i, j: (layer, j, 0)),
            pl.BlockSpec((1, d), lambda i, j: (0, 0)),
            pl.BlockSpec((1, d), lambda i, j: (0, 0)),
        ],
        out_specs=out_specs,
        out_shape=out_shape,
        compiler_params=_cparams(("parallel", "arbitrary")),
        name="ffn_ln",
    )(x32, xb, wgu, wgu, wd, g, b)


def _mm_kernel(x_ref, w_ref, o_ref):
    o_ref[...] = jnp.dot(x_ref[...], w_ref[...], preferred_element_type=F32)


def _mm(xb, w, layer, col0, n, tm, tn):
    m, k = xb.shape
    assert col0 % tn == 0 and n % tn == 0
    jb0 = col0 // tn
    return pl.pallas_call(
        _mm_kernel,
        grid=(m // tm, n // tn),
        in_specs=[
            pl.BlockSpec((tm, k), lambda i, j: (i, 0)),
            pl.BlockSpec((None, k, tn), lambda i, j: (layer, 0, jb0 + j)),
        ],
        out_specs=pl.BlockSpec((tm, tn), lambda i, j: (i, j)),
        out_shape=jax.ShapeDtypeStruct((m, n), F32),
        compiler_params=_cparams(("parallel", "arbitrary")),
        name="proj",
    )(xb, w)


def _rest_rows(m_all, mp, tm):
    first = mp - (m_all // tm - 1) * tm
    assert m_all % tm == 0 and 0 <= first and first + (m_all - mp) == tm
    return first, m_all - mp


def _hyb_in_kernel(x_ref, xr_ref, w_ref, *refs, n_alias):
    qp_ref, qr_ref, kp_ref, kr_ref, vp_ref, vr_ref, up_ref, ur_ref = refs[n_alias:]
    i, j = pl.program_id(0), pl.program_id(1)
    last = i == pl.num_programs(0) - 1
    w = w_ref[...]
    width = w.shape[1] // 2
    y = jnp.dot(x_ref[...], w, preferred_element_type=F32)

    def emit(lo_ref, hi_ref, lo_rest_ref, hi_rest_ref):
        lo_ref[...] = y[:, :width]
        hi_ref[...] = y[:, width:]

        @pl.when(last)
        def _():
            yr = jnp.dot(xr_ref[...], w, preferred_element_type=F32)
            lo_rest_ref[...] = yr[:, :width]
            hi_rest_ref[...] = yr[:, width:]

    @pl.when(j == 0)
    def _():
        emit(qp_ref, kp_ref, qr_ref, kr_ref)

    @pl.when(j == 1)
    def _():
        emit(vp_ref, up_ref, vr_ref, ur_ref)


def _hyb_in(xb, w, layer, mp, k_stack=None, v_stack=None):
    m, k = xb.shape
    n_layers = w.shape[0]
    width = w.shape[2] // 4
    n_rest = m - mp
    assert mp % n_rest == 0
    tm = _pick_tile(mp, 512)
    aliased = [] if k_stack is None else [k_stack, v_stack]
    prompt_spec = pl.BlockSpec((tm, width), lambda i, j: (i, 0))
    stack_spec = pl.BlockSpec((None, tm, width), lambda i, j: (layer, i, 0))
    rest_spec = pl.BlockSpec((n_rest, width), lambda i, j: (0, 0))
    prompt_shape = jax.ShapeDtypeStruct((mp, width), F32)
    stack_shape = jax.ShapeDtypeStruct((n_layers, mp, width), F32)
    rest_shape = jax.ShapeDtypeStruct((n_rest, width), F32)
    qp, qr, kp, kr, vp, vr, up, ur = pl.pallas_call(
        functools.partial(_hyb_in_kernel, n_alias=len(aliased)),
        grid=(mp // tm, 2),
        in_specs=[
            pl.BlockSpec((tm, k), lambda i, j: (i, 0)),
            pl.BlockSpec((n_rest, k), lambda i, j: (mp // n_rest, 0)),
            pl.BlockSpec((None, k, 2 * width), lambda i, j: (layer, 0, j)),
        ] + [pl.BlockSpec(memory_space=pl.ANY)] * len(aliased),
        out_specs=[prompt_spec, rest_spec, stack_spec, rest_spec, stack_spec, rest_spec, prompt_spec, rest_spec],
        out_shape=[prompt_shape, rest_shape, stack_shape, rest_shape, stack_shape, rest_shape, prompt_shape,
                   rest_shape],
        input_output_aliases={3: 2, 4: 4} if aliased else {},
        compiler_params=_cparams(("parallel", "arbitrary")),
        name="hyb_in_proj",
    )(xb, xb, w, *aliased)
    return (qp, qr), (kp, kr), (vp, vr), (up, ur)


def _mm_res_ln_kernel(*refs, nparts, gated, alpha):
    a_refs = refs[:nparts]
    w_refs = refs[nparts:2 * nparts]
    rest = refs[2 * nparts:]
    if gated:
        gw_ref, gb_ref = rest[:2]
        rest = rest[2:]
    x32_ref, g_ref, b_ref, o32_ref, ob_ref = rest
    y = alpha * x32_ref[...]
    for r, (a_ref, w_ref) in enumerate(zip(a_refs, w_refs)):
        a = a_ref[...]
        if gated and r == nparts - 1:
            gate = jnp.dot(a.astype(BF16), gw_ref[...], preferred_element_type=F32) + gb_ref[...]
            a = a * jax.nn.sigmoid(gate)
        y = y + jnp.dot(a.astype(BF16), w_ref[...], preferred_element_type=F32)
    z = _layer_norm(y, g_ref[...], b_ref[...])
    o32_ref[...] = z
    ob_ref[...] = z.astype(BF16)


def _mm_res_ln(parts, w, layer, x32, g, b, alpha, tm, glu=None):
    m, d = x32.shape
    kp = parts[0].shape[1]
    assert all(p.shape[1] == kp for p in parts)
    once = dict(pipeline_mode=pl.Buffered(1))
    in_specs = [pl.BlockSpec((tm, kp), lambda i: (i, 0)) for _ in parts]
    in_specs += [pl.BlockSpec((None, kp, d), lambda i, r=r: (layer, r, 0), **once) for r in range(len(parts))]
    operands = [*parts, *([w] * len(parts))]
    if glu is not None:
        in_specs += [pl.BlockSpec((None, kp, kp), lambda i: (layer, 0, 0), **once), pl.BlockSpec((1, kp), lambda i: (0, 0))]
        operands += list(glu)
    in_specs += [
        pl.BlockSpec((tm, d), lambda i: (i, 0)),
        pl.BlockSpec((1, d), lambda i: (0, 0)),
        pl.BlockSpec((1, d), lambda i: (0, 0)),
    ]
    return pl.pallas_call(
        functools.partial(_mm_res_ln_kernel, nparts=len(parts), gated=glu is not None, alpha=alpha),
        grid=(m // tm,),
        in_specs=in_specs,
        out_specs=[pl.BlockSpec((tm, d), lambda i: (i, 0)), pl.BlockSpec((tm, d), lambda i: (i, 0))],
        out_shape=[jax.ShapeDtypeStruct((m, d), F32), jax.ShapeDtypeStruct((m, d), BF16)],
        compiler_params=_cparams(("parallel",)),
        name="out_proj_ln",
    )(*operands, x32, g, b)


CONV_PAD = 8
CONV_COL_SLAB = 256


def _conv_tile(project, st_ref, w_ref, c_ref, ns_ref, hbuf, *, tt, nt, cw):
    t = pl.program_id(2)
    lo = CONV_PAD - (cw - 1)
    tn = c_ref.shape[1]
    slab = min(tn, CONV_COL_SLAB)

    @pl.when(t == 0)
    def _():
        hbuf[lo:CONV_PAD, :] = st_ref[...]

    for c0 in range(0, tn, slab):
        cols = slice(c0, c0 + slab)
        bg, cg, v = project(cols)
        hbuf[CONV_PAD:CONV_PAD + tt, cols] = cg * v
        w = w_ref[:, cols]
        conv = w[0:1, :] * hbuf[lo:lo + tt, cols]
        for j in range(1, cw):
            conv = conv + w[j:j + 1, :] * hbuf[lo + j:lo + j + tt, cols]
        c_ref[:, cols] = (bg * conv).astype(c_ref.dtype)
    tail = hbuf[lo + tt:CONV_PAD + tt, :]

    @pl.when(t == nt - 1)
    def _():
        ns_ref[...] = tail

    hbuf[lo:CONV_PAD, :] = tail


def _conv_kernel(bg_ref, cg_ref, v_ref, st_ref, w_ref, c_ref, ns_ref, hbuf, **kw):
    _conv_tile(lambda cols: (bg_ref[:, cols], cg_ref[:, cols], v_ref[:, cols]), st_ref, w_ref, c_ref, ns_ref, hbuf,
               **kw)


def _conv_in_kernel(x_ref, wb_ref, wc_ref, wv_ref, st_ref, w_ref, c_ref, ns_ref, hbuf, **kw):
    def project(cols):
        x = x_ref[...]
        return tuple(jnp.dot(x, w[:, cols], preferred_element_type=F32) for w in (wb_ref, wc_ref, wv_ref))

    _conv_tile(project, st_ref, w_ref, c_ref, ns_ref, hbuf, **kw)


def _conv_core(z, state, conv_w, layer, bsz, t_len, tt, tn):
    m, d3 = z.shape
    d = d3 // 3
    cw = conv_w.shape[1]
    nt, nj = t_len // tt, d // tn
    return pl.pallas_call(
        functools.partial(_conv_kernel, tt=tt, nt=nt, cw=cw),
        grid=(bsz, nj, nt),
        in_specs=[
            pl.BlockSpec((tt, tn), lambda b, j, t: (b * nt + t, j)),
            pl.BlockSpec((tt, tn), lambda b, j, t: (b * nt + t, j + nj)),
            pl.BlockSpec((tt, tn), lambda b, j, t: (b * nt + t, j + 2 * nj)),
            pl.BlockSpec((None, cw - 1, tn), lambda b, j, t: (b, 0, j)),
            pl.BlockSpec((None, cw, tn), lambda b, j, t: (layer, 0, j)),
        ],
        out_specs=[
            pl.BlockSpec((tt, tn), lambda b, j, t: (b * nt + t, j)),
            pl.BlockSpec((None, cw - 1, tn), lambda b, j, t: (b, 0, j)),
        ],
        out_shape=[jax.ShapeDtypeStruct((m, d), BF16), jax.ShapeDtypeStruct((bsz, cw - 1, d), F32)],
        scratch_shapes=[pltpu.VMEM((CONV_PAD + tt, tn), F32)],
        compiler_params=_cparams(("parallel", "parallel", "arbitrary")),
        name="conv_core",
    )(z, z, z, state, conv_w)


def _conv_in_core(xb, w_in, state, conv_w, layer, bsz, t_len, tt, tn):
    m, d = xb.shape
    cw = conv_w.shape[1]
    nt, nj = t_len // tt, d // tn
    return pl.pallas_call(
        functools.partial(_conv_in_kernel, tt=tt, nt=nt, cw=cw),
        grid=(bsz, nj, nt),
        in_specs=[
            pl.BlockSpec((tt, d), lambda b, j, t: (b * nt + t, 0)),
            pl.BlockSpec((None, d, tn), lambda b, j, t: (layer, 0, j)),
            pl.BlockSpec((None, d, tn), lambda b, j, t: (layer, 0, j + nj)),
            pl.BlockSpec((None, d, tn), lambda b, j, t: (layer, 0, j + 2 * nj)),
            pl.BlockSpec((None, cw - 1, tn), lambda b, j, t: (b, 0, j)),
            pl.BlockSpec((None, cw, tn), lambda b, j, t: (layer, 0, j)),
        ],
        out_specs=[
            pl.BlockSpec((tt, tn), lambda b, j, t: (b * nt + t, j)),
            pl.BlockSpec((None, cw - 1, tn), lambda b, j, t: (b, 0, j)),
        ],
        out_shape=[jax.ShapeDtypeStruct((m, d), BF16), jax.ShapeDtypeStruct((bsz, cw - 1, d), F32)],
        scratch_shapes=[pltpu.VMEM((CONV_PAD + tt, tn), F32)],
        compiler_params=_cparams(("parallel", "parallel", "arbitrary")),
        name="conv_in_core",
    )(xb, w_in, w_in, w_in, state, conv_w)


def _moba_select_rows(gate_t, n_valid):
    nb, nq = gate_t.shape
    sub = lax.broadcasted_iota(jnp.int32, (nb, nq), 0)
    valid = sub < n_valid
    gate_t = jnp.where(valid, gate_t, NEG_INF)
    beaten_by = jnp.zeros((nb, nq), jnp.int32)
    for jp in range(nb):
        row = gate_t[jp:jp + 1, :]
        beats = (row > gate_t) | ((row == gate_t) & (sub > jp))
        beaten_by = beaten_by + jnp.where(beats, 1, 0)
    return valid & (beaten_by < MOBA_TOPK)


def _moba_prompt_kernel(q_ref, k_ref, v_ref, o_ref, kb_scr, vt_scr, km_scr, *, nb, scale):
    blk = MOBA_BLOCK
    kb_scr[...] = k_ref[...].astype(BF16)
    for j in range(nb):
        rows = slice(j * blk, (j + 1) * blk)
        km_scr[j:j + 1, :] = jnp.mean(k_ref[rows, :], axis=0, keepdims=True)
        vt_scr[:, rows] = v_ref[rows, :].T.astype(BF16)
    causal = lax.broadcasted_iota(jnp.int32, (blk, blk), 0) <= lax.broadcasted_iota(jnp.int32, (blk, blk), 1)
    causal_bias = jnp.where(causal, 0.0, NEG_INF).astype(F32)

    for qi in range(nb):
        n_keys = (qi + 1) * blk
        q = q_ref[qi * blk:(qi + 1) * blk, :]
        s_all = lax.dot_general(kb_scr[0:n_keys, :], q.astype(BF16), (((1,), (1,)), ((), ())),
                                preferred_element_type=F32) * scale
        if qi > 0:
            gate_t = lax.dot_general(km_scr[...], q, (((1,), (1,)), ((), ())),
                                     precision=lax.Precision.HIGHEST, preferred_element_type=F32)
            sel_bias = jnp.where(_moba_select_rows(gate_t, qi), 0.0, NEG_INF).astype(F32)
        s_blocks = [s_all[j * blk:(j + 1) * blk, :] + sel_bias[j:j + 1, :] for j in range(qi)]
        s_blocks.append(s_all[qi * blk:n_keys, :] + causal_bias)
        top = s_blocks[0]
        for sb in s_blocks[1:]:
            top = jnp.maximum(top, sb)
        m = jnp.max(top, axis=0, keepdims=True)
        p_blocks = [jnp.exp(sb - m) for sb in s_blocks]
        tot = p_blocks[0]
        for pb in p_blocks[1:]:
            tot = tot + pb
        l = jnp.sum(tot, axis=0, keepdims=True)
        p_all = jnp.concatenate([pb.astype(BF16) for pb in p_blocks], axis=0)
        acc = jnp.dot(vt_scr[:, 0:n_keys], p_all, preferred_element_type=F32)
        o_ref[qi * blk:(qi + 1) * blk, :] = (acc / l).T.astype(o_ref.dtype)


def _moba_prompt(q, k_stack, v_stack, layer, bsz, t_len, nh, hd, m_out):
    m = m_out
    assert t_len % MOBA_BLOCK == 0
    nb = t_len // MOBA_BLOCK
    seq_spec = pl.BlockSpec((t_len, hd), lambda b, h: (b, h))
    kv_spec = pl.BlockSpec((None, t_len, hd), lambda b, h: (layer, b, h))
    return pl.pallas_call(
        functools.partial(_moba_prompt_kernel, nb=nb, scale=hd ** -0.5),
        grid=(bsz, nh),
        in_specs=[seq_spec, kv_spec, kv_spec],
        out_specs=seq_spec,
        out_shape=jax.ShapeDtypeStruct((m, nh * hd), BF16),
        scratch_shapes=[pltpu.VMEM((t_len, hd), BF16), pltpu.VMEM((hd, t_len), BF16),
                        pltpu.VMEM((nb, hd), F32)],
        compiler_params=_cparams(("parallel", "parallel")),
        name="moba_prompt",
    )(q, k_stack, v_stack)


def _moba_select_grouped(gate, n_valid_blocks, group):
    rows, nl = gate.shape
    lane = lax.broadcasted_iota(jnp.int32, (rows, nl), 1)
    valid = lane < n_valid_blocks * group
    gate = jnp.where(valid, gate, NEG_INF)
    beaten_by = jnp.zeros((rows, nl), jnp.int32)
    for jp in range(0, nl, group):
        col = gate[:, jp:jp + 1]
        beats = (col > gate) | ((col == gate) & (lane >= jp + group))
        beaten_by = beaten_by + jnp.where(beats, 1, 0)
    return valid & (beaten_by < MOBA_TOPK)


def _fold_lanes(x, op):
    tiles = [x[:, i:i + LANES] for i in range(0, x.shape[1], LANES)]
    out = tiles[0]
    for t in tiles[1:]:
        out = op(out, t)
    return out


def _sample_attn_kernel(pt_ref, q_ref, kn_ref, vn_ref, hbias_ref, obias_ref, *refs, nh, n_past_blocks,
                        pages_per_step, pages_per_block, scale):
    del pt_ref
    kp_refs = refs[:pages_per_step]
    vp_refs = refs[pages_per_step:2 * pages_per_step]
    o_ref, gate_scr, m_scr, l_scr, acc_scr = refs[2 * pages_per_step:]
    step = pl.program_id(1)
    n_steps = pl.num_programs(1)
    rows, hd = q_ref.shape
    n_pages = m_scr.shape[1]
    q = q_ref[...]
    qb = q.astype(BF16)
    q3 = q.reshape(rows // nh, nh, hd)
    lane = lax.broadcasted_iota(jnp.int32, (rows, n_pages), 1)

    def put_column(scr, pg, col):
        scr[...] = jnp.where(lane == pg, col, scr[...])

    for blk0 in range(0, pages_per_step, pages_per_block):
        ksum = jnp.zeros((nh, hd), F32)
        for r in range(blk0, blk0 + pages_per_block):
            pg = step * pages_per_step + r
            k = kp_refs[r][...]
            ksum = ksum + jnp.sum(k.reshape(k.shape[0] // nh, nh, hd), axis=0)
            s = lax.dot_general(qb, k.astype(BF16), (((1,), (1,)), ((), ())), preferred_element_type=F32)
            s = s * scale + hbias_ref[...]
            m = jnp.max(_fold_lanes(s, jnp.maximum), axis=1, keepdims=True)
            p = jnp.exp(s - m)
            put_column(m_scr, pg, m)
            put_column(l_scr, pg, jnp.sum(_fold_lanes(p, jnp.add), axis=1, keepdims=True))
            acc_scr[pg] = jnp.dot(p.astype(BF16), vp_refs[r][...].astype(BF16), preferred_element_type=F32)
        kmean = ksum / MOBA_BLOCK
        gcol = jnp.sum(q3 * kmean[None, :, :], axis=2, keepdims=True).reshape(rows, 1)
        for r in range(blk0, blk0 + pages_per_block):
            put_column(gate_scr, step * pages_per_step + r, gcol)

    @pl.when(step == n_steps - 1)
    def _():
        keep = _moba_select_grouped(gate_scr[...], n_past_blocks, pages_per_block)
        m_pages = m_scr[...] + jnp.where(keep, 0.0, NEG_INF)
        s = lax.dot_general(qb, kn_ref[...].astype(BF16), (((1,), (1,)), ((), ())), preferred_element_type=F32)
        s = s * scale + obias_ref[...]
        m_own = jnp.max(s, axis=1, keepdims=True)
        m_tot = jnp.maximum(m_own, jnp.max(m_pages, axis=1, keepdims=True))
        p = jnp.exp(s - m_tot)
        w = jnp.exp(m_pages - m_tot)
        l = jnp.sum(p, axis=1, keepdims=True) + jnp.sum(w * l_scr[...], axis=1, keepdims=True)
        acc = jnp.dot(p.astype(BF16), vn_ref[...].astype(BF16), preferred_element_type=F32)
        for pg in range(n_pages):
            acc = acc + w[:, pg:pg + 1] * acc_scr[pg]
        o_ref[...] = (acc / l).astype(o_ref.dtype)


def _sample_attn(q, k_new, v_new, cache_k, cache_v, layer, page_table, db, t_len, nh, hd, pages_per_step=8):
    n_hyb, n_pool, page, _, _ = cache_k.shape
    n_pages = page_table.shape[1]
    past_len = n_pages * page
    assert past_len % MOBA_BLOCK == 0 and t_len <= MOBA_BLOCK and MOBA_BLOCK % page == 0
    assert nh & (nh - 1) == 0, "head matching uses bit masks"
    ppb = MOBA_BLOCK // page
    assert pages_per_step % ppb == 0 and n_pages % pages_per_step == 0
    rows = t_len * nh
    ck = cache_k.reshape(n_hyb, n_pool, page * nh, hd)
    cv = cache_v.reshape(n_hyb, n_pool, page * nh, hd)
    r = jnp.arange(rows, dtype=jnp.int32)[:, None]
    c = jnp.arange(page * nh, dtype=jnp.int32)[None, :]
    same_head = (r & (nh - 1)) == (c & (nh - 1))
    head_bias = jnp.where(same_head, 0.0, NEG_INF).astype(F32)
    own_bias = jnp.where(same_head[:, :rows] & (c[:, :rows] // nh <= r // nh), 0.0, NEG_INF).astype(F32)

    def page_spec(i):
        return pl.BlockSpec((None, None, page * nh, hd),
                            lambda b, s, pt: (layer, pt[b, s * pages_per_step + i], 0, 0))

    def row_spec(nrows):
        return pl.BlockSpec((None, nrows, hd), lambda b, s, pt: (b, 0, 0))

    full = lambda a: pl.BlockSpec(a.shape, lambda b, s, pt: (0,) * a.ndim)
    out = pl.pallas_call(
        functools.partial(_sample_attn_kernel, nh=nh, n_past_blocks=past_len // MOBA_BLOCK,
                          pages_per_step=pages_per_step, pages_per_block=ppb, scale=hd ** -0.5),
        grid_spec=pltpu.PrefetchScalarGridSpec(
            num_scalar_prefetch=1,
            grid=(db, n_pages // pages_per_step),
            in_specs=[row_spec(rows), row_spec(rows), row_spec(rows), full(head_bias), full(own_bias)]
            + [page_spec(i) for i in range(pages_per_step)] * 2,
            out_specs=row_spec(rows),
            scratch_shapes=[pltpu.VMEM((rows, n_pages), F32), pltpu.VMEM((rows, n_pages), F32),
                            pltpu.VMEM((rows, n_pages), F32), pltpu.VMEM((n_pages, rows, hd), F32)],
        ),
        out_shape=jax.ShapeDtypeStruct((db, rows, hd), BF16),
        compiler_params=_cparams(("parallel", "arbitrary")),
        name="sample_attn",
    )(page_table, q.reshape(db, rows, hd), k_new.reshape(db, rows, hd), v_new.reshape(db, rows, hd),
      head_bias, own_bias, *([ck] * pages_per_step), *([cv] * pages_per_step))
    return out.reshape(db * t_len, nh * hd)


def _s5_kernel(u_ref, perm_ref, unperm_ref, bre_ref, bim_ref, cre_ref, cimn_ref, lre_ref, lim_ref, llre_ref, llim_ref,
               d_ref, h0re_ref, h0im_ref, y_ref, htre_ref, htim_ref, hr_scr, hi_scr, cre_scr, cim_scr, yc_scr,
               *, seg_len, chained, n_chunks):
    c = pl.program_id(1)
    nseg = SUBLANES
    nstate = hr_scr.shape[1]
    n_kb = bre_ref.shape[0]
    ku = bre_ref.shape[1]
    ks = bre_ref.shape[2]

    up = jnp.dot(perm_ref[...], u_ref[...].astype(BF16), preferred_element_type=F32).astype(BF16)
    for kb in range(n_kb):
        ukb = up[:, kb * ku:(kb + 1) * ku]
        hr_scr[:, kb * ks:(kb + 1) * ks] = jnp.dot(ukb, bre_ref[kb], preferred_element_type=F32)
        hi_scr[:, kb * ks:(kb + 1) * ks] = jnp.dot(ukb, bim_ref[kb], preferred_element_type=F32)

    if chained:
        @pl.when(c == 0)
        def _():
            cre_scr[...] = jnp.zeros_like(cre_scr)
            cim_scr[...] = jnp.zeros_like(cim_scr)
            cre_scr[0:1, :] = h0re_ref[...]
            cim_scr[0:1, :] = h0im_ref[...]

    for lc in range(nstate // S5_SCAN_LANES):
        cols = slice(lc * S5_SCAN_LANES, (lc + 1) * S5_SCAN_LANES)
        lr = jnp.broadcast_to(lre_ref[:, cols], (nseg, S5_SCAN_LANES))
        li = jnp.broadcast_to(lim_ref[:, cols], (nseg, S5_SCAN_LANES))

        def advance(i, hr, hi):
            r = pl.multiple_of(i * nseg, nseg)
            nr = lr * hr - li * hi + hr_scr[pl.ds(r, nseg), cols]
            ni = lr * hi + li * hr + hi_scr[pl.ds(r, nseg), cols]
            return r, nr, ni

        def scan_only(i, carry):
            _, nr, ni = advance(i, *carry)
            return nr, ni

        def scan_store(i, carry):
            r, nr, ni = advance(i, *carry)
            hr_scr[pl.ds(r, nseg), cols] = nr
            hi_scr[pl.ds(r, nseg), cols] = ni
            return nr, ni

        if chained:
            er, ei = lax.fori_loop(0, seg_len, scan_only, (cre_scr[:, cols], cim_scr[:, cols]),
                                   unroll=S5_SCAN_UNROLL)
            llr, lli = llre_ref[:, cols], llim_ref[:, cols]
            fr, fi = er[0:1, :], ei[0:1, :]
            for j in range(1, nseg):
                cre_scr[j:j + 1, cols] = fr
                cim_scr[j:j + 1, cols] = fi
                if j < nseg - 1:
                    fr, fi = (er[j:j + 1, :] + (llr * fr - lli * fi), ei[j:j + 1, :] + (llr * fi + lli * fr))
            init = (cre_scr[:, cols], cim_scr[:, cols])
        else:
            init = (h0re_ref[:, cols], h0im_ref[:, cols])

        hr, hi = lax.fori_loop(0, seg_len, scan_store, init, unroll=S5_SCAN_UNROLL)

        if chained:
            last_r, last_i = hr[nseg - 1:nseg, :], hi[nseg - 1:nseg, :]
            cre_scr[:, cols] = jnp.zeros((nseg, S5_SCAN_LANES), F32)
            cim_scr[:, cols] = jnp.zeros((nseg, S5_SCAN_LANES), F32)
            cre_scr[0:1, cols] = last_r
            cim_scr[0:1, cols] = last_i

            @pl.when(c == n_chunks - 1)
            def _():
                htre_ref[:, cols] = last_r
                htim_ref[:, cols] = last_i
        else:
            htre_ref[:, cols] = hr
            htim_ref[:, cols] = hi

    for kb in range(n_kb):
        hre = hr_scr[:, kb * ks:(kb + 1) * ks].astype(BF16)
        him = hi_scr[:, kb * ks:(kb + 1) * ks].astype(BF16)
        ykb = (jnp.dot(hre, cre_ref[kb], preferred_element_type=F32)
               + jnp.dot(him, cimn_ref[kb], preferred_element_type=F32))
        yc_scr[:, kb * ku:(kb + 1) * ku] = ykb

    yc = yc_scr[...]
    hi = yc.astype(BF16)
    r1 = yc - hi.astype(F32)
    mid = r1.astype(BF16)
    lo = (r1 - mid.astype(F32)).astype(BF16)
    unperm = unperm_ref[...]
    y = (jnp.dot(unperm, hi, preferred_element_type=F32) + jnp.dot(unperm, mid, preferred_element_type=F32)
         + jnp.dot(unperm, lo, preferred_element_type=F32))
    y_ref[...] = jax.nn.gelu(y + d_ref[...] * u_ref[...])


def _s5_discretise(a_re, a_im, log_dt, b_re, b_im, c_re, c_im, d_skip, seg_len):
    g, p = a_re.shape
    ch = b_re.shape[-1]
    lam = lax.complex(a_re.astype(F32), a_im.astype(F32))
    dt = jnp.exp(log_dt.astype(F32))[:, None]
    lam_bar = jnp.exp(lam * dt)
    b_bar = ((lam_bar - 1.0) / lam)[..., None] * lax.complex(b_re.astype(F32), b_im.astype(F32))
    lam_seg = lam_bar
    assert seg_len & (seg_len - 1) == 0
    for _ in range(seg_len.bit_length() - 1):
        lam_seg = lam_seg * lam_seg
    gl = S5_LANE_GROUPS
    n_kb = g // gl
    eye = jnp.eye(gl, dtype=F32)

    def pack_b(x):
        x = x.reshape(n_kb, gl, p, ch).transpose(0, 1, 3, 2)
        return jnp.einsum("kgcp,gh->kgchp", x, eye).reshape(n_kb, gl * ch, gl * p)

    def pack_c(x):
        x = x.reshape(n_kb, gl, ch, p).transpose(0, 1, 3, 2)
        return jnp.einsum("kgpc,gh->kgphc", x, eye).reshape(n_kb, gl * p, gl * ch)

    flat = lambda x: x.reshape(1, g * p)
    return dict(
        bre=pack_b(jnp.real(b_bar)).astype(BF16), bim=pack_b(jnp.imag(b_bar)).astype(BF16),
        cre=pack_c(c_re.astype(F32)).astype(BF16), cimn=pack_c(-c_im.astype(F32)).astype(BF16),
        lre=flat(jnp.real(lam_bar)), lim=flat(jnp.imag(lam_bar)),
        llre=flat(jnp.real(lam_seg)), llim=flat(jnp.imag(lam_seg)),
        d=d_skip.astype(F32).reshape(1, g * ch),
    )


def _s5(u, h0_re, h0_im, prm, seg_len, chained, m_out=None):
    n_rows, w = u.shape
    m = n_rows if m_out is None else m_out
    gp = prm["lre"].shape[1]
    chunk = SUBLANES * seg_len
    if chained:
        nseq = h0_re.shape[0]
        n_chunks = n_rows // nseq // chunk
        grid = (nseq, n_chunks)
        u_spec = pl.BlockSpec((chunk, w), lambda b, c: (b * n_chunks + c, 0))
        h_spec = pl.BlockSpec((None, 1, gp), lambda b, c: (b, 0, 0))
        h_shape = jax.ShapeDtypeStruct((nseq, 1, gp), F32)
    else:
        assert n_rows == chunk
        n_chunks = 1
        grid = (1, 1)
        u_spec = pl.BlockSpec((chunk, w), lambda b, c: (0, 0))
        h_spec = pl.BlockSpec((SUBLANES, gp), lambda b, c: (0, 0))
        h_shape = jax.ShapeDtypeStruct((SUBLANES, gp), F32)
    full = lambda a: pl.BlockSpec(a.shape, lambda b, c: (0,) * a.ndim)
    names = ("bre", "bim", "cre", "cimn", "lre", "lim", "llre", "llim", "d")
    r = jnp.arange(chunk)
    perm = (((r % SUBLANES) * seg_len + r // SUBLANES)[:, None] == r[None, :]).astype(BF16)
    consts = [perm, perm.T]
    return pl.pallas_call(
        functools.partial(_s5_kernel, seg_len=seg_len, chained=chained, n_chunks=n_chunks),
        grid=grid,
        in_specs=[u_spec] + [full(a) for a in consts] + [full(prm[n]) for n in names] + [h_spec, h_spec],
        out_specs=[u_spec, h_spec, h_spec],
        out_shape=[jax.ShapeDtypeStruct((m, w), F32), h_shape, h_shape],
        scratch_shapes=[pltpu.VMEM((chunk, gp), F32), pltpu.VMEM((chunk, gp), F32),
                        pltpu.VMEM((SUBLANES, gp), F32), pltpu.VMEM((SUBLANES, gp), F32),
                        pltpu.VMEM((chunk, w), F32)],
        compiler_params=_cparams(("parallel", "arbitrary")),
        name="s5_scan",
    )(u, *consts, *[prm[n] for n in names], h0_re, h0_im)


def _pick_tile(n, target):
    t = min(n, target)
    while n % t:
        t //= 2
    return t


def _row_tile(m):
    return min(range(LANES, MAX_ROW_TILE + 1, LANES), key=lambda t: (-(-m // t) * t, -t))


def _fill_rest(full, sample_rows, mp):
    pad = full.shape[0] - mp - sample_rows.shape[0]
    rest = jnp.concatenate([sample_rows.astype(full.dtype), jnp.zeros((pad, full.shape[1]), full.dtype)], axis=0)
    return lax.dynamic_update_slice(full, rest, (mp, 0))


def _run_trunk(x_prompt, x_sample, prm, s5_prompt, s5_sample, kv, s5_re0, s5_im0, conv0_prompt, conv0_sample):
    bp, tp, d = x_prompt.shape
    bs, ts, _ = x_sample.shape
    mp, ms = bp * tp, bs * ts
    tm = _row_tile(mp + ms)
    m_all = -(-(mp + ms) // tm) * tm
    depth = prm["ffn1_wgu"].shape[0]
    alpha = (2.0 * depth) ** 0.25
    nh, hd = kv["nh"], kv["hd"]
    moba_w = nh * hd
    g_cnt, p_cnt = prm["s5_a_re"].shape[1:]
    gp = g_cnt * p_cnt
    tf = _pick_tile(prm["ffn1_wd"].shape[1], 512)
    sample = slice(mp, mp + ms)

    x32 = jnp.concatenate([x_prompt.reshape(mp, d).astype(F32), x_sample.reshape(ms, d).astype(F32),
                           jnp.zeros((m_all - mp - ms, d), F32)], axis=0)
    xb = x32.astype(BF16)
    row = lambda a: a.reshape(1, -1).astype(F32)
    out = dict(hrp=[], hip=[], cvp=[], ks=[], vs=[], hrs=[], his=[], cvs=[])
    k_stack = v_stack = None
    for layer in range(depth):
        i = layer // 2
        g, b = prm["ln_g"][layer], prm["ln_b"][layer]
        x32, xb = _ffn(x32, xb, prm["ffn1_wgu"], prm["ffn1_wd"], layer, row(g[0]), row(b[0]), alpha, tm, tf)
        if layer % 2 == 0:
            assert d - moba_w == moba_w
            (q, q_r), (k_stack, k_rest), (v_stack, v_rest), (u, u_r) = _hyb_in(xb, prm["hyb_w_in"], i, mp, k_stack,
                                                                               v_stack)
            attn = _moba_prompt(q, k_stack, v_stack, i, bp, tp, nh, hd, m_all)
            attn_s = _sample_attn(q_r[:ms], k_rest[:ms], v_rest[:ms], kv["cache_k"], kv["cache_v"], i,
                                  kv["page_table"], bs, ts, nh, hd)
            zero = jnp.zeros((bp, 1, gp), F32)
            y, hr_p, hi_p = _s5(u, zero, zero, s5_prompt[i], s5_prompt[i]["seg_len"], True, m_out=m_all)
            y_s, hr_s, hi_s = _s5(u_r[:ms], s5_re0[i].reshape(bs, gp), s5_im0[i].reshape(bs, gp), s5_sample[i],
                                  s5_sample[i]["seg_len"], False)
            x32, xb = _mm_res_ln([_fill_rest(attn, attn_s, mp), _fill_rest(y, y_s, mp)], prm["hyb_w_out"], i, x32,
                                 row(g[1]), row(b[1]), alpha, tm, glu=(prm["s5_glu_w"], row(prm["s5_glu_b"][i])))
            out["ks"].append(k_rest[:ms].reshape(bs, ts, nh, hd))
            out["vs"].append(v_rest[:ms].reshape(bs, ts, nh, hd))
            out["hrp"].append(hr_p.reshape(bp, g_cnt, p_cnt))
            out["hip"].append(hi_p.reshape(bp, g_cnt, p_cnt))
            out["hrs"].append(hr_s.reshape(bs, g_cnt, p_cnt))
            out["his"].append(hi_s.reshape(bs, g_cnt, p_cnt))
        else:
            c, cv_p = _conv_in_core(xb, prm["conv_w_in"], conv0_prompt[i], prm["conv_w"], i, bp, tp,
                                    _pick_tile(tp, 512), _pick_tile(d, 1024))
            z = _mm(xb[sample], prm["conv_w_in"], i, 0, 3 * d, ms, _pick_tile(d, 2048))
            c_s, cv_s = _conv_core(z, conv0_sample[i], prm["conv_w"], i, bs, ts, ts, _pick_tile(d, 512))
            x32, xb = _mm_res_ln([_fill_rest(c, c_s, mp)], prm["conv_w_out"], i, x32, row(g[1]), row(b[1]),
                                 alpha, tm)
            out["cvp"].append(cv_p)
            out["cvs"].append(cv_s)
        if layer < depth - 1:
            x32, xb = _ffn(x32, xb, prm["ffn2_wgu"], prm["ffn2_wd"], layer, row(g[2]), row(b[2]), alpha, tm, tf)
        else:
            y_prompt, y_rest = _ffn(x32, xb, prm["ffn2_wgu"], prm["ffn2_wd"], layer, row(g[2]), row(b[2]), alpha,
                                    tm, tf, mp=mp)
    out = {n: jnp.stack(a) for n, a in out.items()}
    n_hyb = k_stack.shape[0]
    out["kp"] = k_stack.reshape(n_hyb, bp, tp, nh, hd)
    out["vp"] = v_stack.reshape(n_hyb, bp, tp, nh, hd)
    return y_prompt.reshape(bp, tp, d), y_rest[:ms].reshape(bs, ts, d), out


def kernel(x_prompt, x_sample, cache_k, cache_v, state_s5_re, state_s5_im, state_conv, page_table, ffn1_wgu, ffn1_wd, ffn2_wgu, ffn2_wd, ln_g, ln_b, hyb_w_in, hyb_w_out, s5_a_re, s5_a_im, s5_log_dt, s5_b_re, s5_b_im, s5_c_re, s5_c_im, s5_d, s5_glu_w, s5_glu_b, conv_w_in, conv_w, conv_w_out):
    n_hyb, _, _, nh, hd = cache_k.shape
    bp, t_prompt, d = x_prompt.shape
    db, t_sample, _ = x_sample.shape
    n_conv, cw = conv_w.shape[:2]
    g_cnt, p_cnt = s5_a_re.shape[1:]
    assert db == SUBLANES, "the sample scan lays the decode batch along the vreg sublanes"
    assert (s5_b_re.shape[-1] * S5_LANE_GROUPS) == 128

    prm = dict(
        ffn1_wgu=ffn1_wgu.astype(BF16), ffn1_wd=ffn1_wd.astype(BF16),
        ffn2_wgu=ffn2_wgu.astype(BF16), ffn2_wd=ffn2_wd.astype(BF16),
        ln_g=ln_g, ln_b=ln_b, hyb_w_in=hyb_w_in.astype(BF16), hyb_w_out=hyb_w_out.astype(BF16),
        s5_a_re=s5_a_re, s5_glu_w=s5_glu_w.astype(BF16), s5_glu_b=s5_glu_b,
        conv_w_in=conv_w_in.astype(BF16), conv_w=conv_w.astype(F32), conv_w_out=conv_w_out.astype(BF16),
    )

    def s5_prms(seg_len):
        out = []
        for i in range(n_hyb):
            sp = _s5_discretise(s5_a_re[i], s5_a_im[i], s5_log_dt[i], s5_b_re[i], s5_b_im[i], s5_c_re[i],
                                s5_c_im[i], s5_d[i], seg_len)
            sp["seg_len"] = seg_len
            out.append(sp)
        return out

    prompt_seg = _pick_tile(t_prompt // SUBLANES, 32)
    kv = dict(nh=nh, hd=hd, cache_k=cache_k, cache_v=cache_v, page_table=page_table)
    conv_zero = jnp.zeros((n_conv, bp, cw - 1, d), x_prompt.dtype)
    y_p, y_s, o = _run_trunk(x_prompt, x_sample, prm, s5_prms(prompt_seg), s5_prms(t_sample), kv, state_s5_re,
                             state_s5_im, conv_zero, state_conv)
    return (y_p, y_s, o["kp"], o["vp"], o["hrp"], o["hip"], o["cvp"], o["ks"], o["vs"], o["hrs"], o["his"], o["cvs"])
```

```python
import functools

import jax
import jax.numpy as jnp
from jax import lax
from jax.experimental import pallas as pl
from jax.experimental.pallas import tpu as pltpu

MOBA_BLOCK = 256
MOBA_TOPK = 3
LN_EPS = 1e-5
LANES = 128
SUBLANES = 8
S5_LANE_GROUPS = 8
S5_SCAN_LANES = 512
S5_SCAN_UNROLL = 4
BF16_ROWS = 16
MAX_ROW_TILE = 640
VMEM_LIMIT_BYTES = 56 * 1024 * 1024

F32 = jnp.float32
BF16 = jnp.bfloat16
NEG_INF = float("-inf")


def _cparams(semantics):
    return pltpu.CompilerParams(dimension_semantics=semantics, vmem_limit_bytes=VMEM_LIMIT_BYTES)


def _layer_norm(y, g, b):
    mu = jnp.mean(y, axis=-1, keepdims=True)
    d = y - mu
    var = jnp.mean(d * d, axis=-1, keepdims=True)
    return d * lax.rsqrt(var + LN_EPS) * g + b


def _convert_slab(s, n_steps, src_ref, layer, dst_ref, inbuf, outbuf, sem_in, sem_out, rows):
    last_slab = src_ref.shape[1] // rows - 1

    def row0(k):
        return pl.multiple_of(jnp.minimum(k, last_slab) * rows, rows)

    def fetch(k, slot):
        return pltpu.make_async_copy(src_ref.at[layer, pl.ds(row0(k), rows), :], inbuf.at[slot], sem_in.at[slot])

    def send(k):
        return pltpu.make_async_copy(outbuf, dst_ref.at[0, pl.ds(row0(k), rows), :], sem_out)

    slot = lax.rem(s, 2)

    @pl.when(s == 0)
    def _():
        fetch(0, 0).start()
        outbuf[...] = jnp.zeros_like(outbuf)
        send(0).start()

    fetch(s, slot).wait()
    fetch(s + 1, 1 - slot).start()
    send(s).wait()
    outbuf[...] = inbuf[slot].astype(BF16)
    send(s).start()

    @pl.when(s == n_steps - 1)
    def _():
        fetch(s + 1, 1 - slot).wait()
        send(s).wait()


def _convert_rows(n_rows, n_steps):
    return next(r for r in range(BF16_ROWS, n_rows + 1, BF16_ROWS) if n_rows % r == 0 and n_rows // r <= n_steps)


def _ffn_kernel(*refs, alpha, nf, rest, converting):
    x32_ref, xb_ref, wg_ref, wu_ref, wd_ref, g_ref, b_ref = refs[:7]
    if converting:
        (ngu_src, nd_src, o32_ref, aux_ref, ngu_dst, nd_dst, gu_in, gu_out, d_in, d_out, sem_in, sem_out) = refs[7:]
    else:
        o32_ref, aux_ref = refs[7:]
    i, j = pl.program_id(0), pl.program_id(1)
    if converting:
        step, n_steps = i * nf + j, pl.num_programs(0) * nf
        _convert_slab(step, n_steps, ngu_src, converting[0], ngu_dst, gu_in, gu_out, sem_in.at[0], sem_out.at[0],
                      gu_in.shape[1])
        _convert_slab(step, n_steps, nd_src, converting[0], nd_dst, d_in, d_out, sem_in.at[1], sem_out.at[1],
                      d_in.shape[1])

    @pl.when(j == 0)
    def _():
        o32_ref[...] = jnp.zeros_like(o32_ref)

    xb = xb_ref[...]
    gate = jnp.dot(xb, wg_ref[...], preferred_element_type=F32)
    up = jnp.dot(xb, wu_ref[...], preferred_element_type=F32)
    h = (gate * jax.nn.sigmoid(gate)) * up
    o32_ref[...] += jnp.dot(h.astype(BF16), wd_ref[...], preferred_element_type=F32)

    @pl.when(j == nf - 1)
    def _():
        z = _layer_norm(alpha * x32_ref[...] + 0.5 * o32_ref[...], g_ref[...], b_ref[...])
        o32_ref[...] = z
        if rest is None:
            aux_ref[...] = z.astype(BF16)
        else:
            @pl.when(i == pl.num_programs(0) - 1)
            def _():
                aux_ref[...] = z[rest[0]:rest[0] + rest[1], :]


def _ffn(x32, xb, wgu, wd, layer, g, b, alpha, tm, tf, mp=None, convert_next=None):
    m, d = x32.shape
    f = wd.shape[1]
    nf = f // tf
    n_steps = (m // tm) * nf
    if mp is None:
        rest = None
        out_specs = [pl.BlockSpec((tm, d), lambda i, j: (i, 0)), pl.BlockSpec((tm, d), lambda i, j: (i, 0))]
        out_shape = [jax.ShapeDtypeStruct((m, d), F32), jax.ShapeDtypeStruct((m, d), BF16)]
    else:
        rest = _rest_rows(m, mp, tm)
        out_specs = [pl.BlockSpec((tm, d), lambda i, j: (i, 0)), pl.BlockSpec((rest[1], d), lambda i, j: (0, 0))]
        out_shape = [jax.ShapeDtypeStruct((mp, d), F32), jax.ShapeDtypeStruct((rest[1], d), F32)]
    in_specs = [
        pl.BlockSpec((tm, d), lambda i, j: (i, 0)),
        pl.BlockSpec((tm, d), lambda i, j: (i, 0)),
        pl.BlockSpec((None, d, tf), lambda i, j: (layer, 0, j)),
        pl.BlockSpec((None, d, tf), lambda i, j: (layer, 0, j + nf)),
        pl.BlockSpec((None, tf, d), lambda i, j: (layer, j, 0)),
        pl.BlockSpec((1, d), lambda i, j: (0, 0)),
        pl.BlockSpec((1, d), lambda i, j: (0, 0)),
    ]
    operands = [x32, xb, wgu, wgu, wd, g, b]
    scratch, converting = [], None
    if convert_next is not None:
        src_gu, src_d, next_layer = convert_next
        converting = (next_layer,)
        gu_rows, d_rows = _convert_rows(d, n_steps), _convert_rows(f, n_steps)
        any_spec = pl.BlockSpec(memory_space=pl.ANY)
        in_specs += [any_spec, any_spec]
        operands += [src_gu, src_d]
        out_specs += [any_spec, any_spec]
        out_shape += [jax.ShapeDtypeStruct((1, d, 2 * f), BF16), jax.ShapeDtypeStruct((1, f, d), BF16)]
        scratch = [pltpu.VMEM((2, gu_rows, 2 * f), F32), pltpu.VMEM((gu_rows, 2 * f), BF16),
                   pltpu.VMEM((2, d_rows, d), F32), pltpu.VMEM((d_rows, d), BF16),
                   pltpu.SemaphoreType.DMA((2, 2)), pltpu.SemaphoreType.DMA((2,))]
    return pl.pallas_call(
        functools.partial(_ffn_kernel, alpha=alpha, nf=nf, rest=rest, converting=converting),
        grid=(m // tm, nf),
        in_specs=in_specs,
        out_specs=out_specs,
        out_shape=out_shape,
        scratch_shapes=scratch,
        compiler_params=_cparams(("arbitrary", "arbitrary") if converting else ("parallel", "arbitrary")),
        name="ffn_ln",
    )(*operands)


def _mm_kernel(x_ref, w_ref, o_ref):
    o_ref[...] = jnp.dot(x_ref[...], w_ref[...], preferred_element_type=F32)


def _mm(xb, w, layer, col0, n, tm, tn):
    m, k = xb.shape
    assert col0 % tn == 0 and n % tn == 0
    jb0 = col0 // tn
    return pl.pallas_call(
        _mm_kernel,
        grid=(m // tm, n // tn),
        in_specs=[
            pl.BlockSpec((tm, k), lambda i, j: (i, 0)),
            pl.BlockSpec((None, k, tn), lambda i, j: (layer, 0, jb0 + j)),
        ],
        out_specs=pl.BlockSpec((tm, tn), lambda i, j: (i, j)),
        out_shape=jax.ShapeDtypeStruct((m, n), F32),
        compiler_params=_cparams(("parallel", "arbitrary")),
        name="proj",
    )(xb, w)


def _rest_rows(m_all, mp, tm):
    first = mp - (m_all // tm - 1) * tm
    assert m_all % tm == 0 and 0 <= first and first + (m_all - mp) == tm
    return first, m_all - mp


def _hyb_in_kernel(x_ref, xr_ref, w_ref, *refs, n_alias):
    qp_ref, qr_ref, kp_ref, kr_ref, vp_ref, vr_ref, up_ref, ur_ref = refs[n_alias:]
    i, j = pl.program_id(0), pl.program_id(1)
    last = i == pl.num_programs(0) - 1
    w = w_ref[...]
    width = w.shape[1] // 2
    y = jnp.dot(x_ref[...], w, preferred_element_type=F32)

    def emit(lo_ref, hi_ref, lo_rest_ref, hi_rest_ref):
        lo_ref[...] = y[:, :width]
        hi_ref[...] = y[:, width:]

        @pl.when(last)
        def _():
            yr = jnp.dot(xr_ref[...], w, preferred_element_type=F32)
            lo_rest_ref[...] = yr[:, :width]
            hi_rest_ref[...] = yr[:, width:]

    @pl.when(j == 0)
    def _():
        emit(qp_ref, kp_ref, qr_ref, kr_ref)

    @pl.when(j == 1)
    def _():
        emit(vp_ref, up_ref, vr_ref, ur_ref)


def _hyb_in(xb, w, layer, mp, k_stack=None, v_stack=None):
    m, k = xb.shape
    n_layers = w.shape[0]
    width = w.shape[2] // 4
    n_rest = m - mp
    assert mp % n_rest == 0
    tm = _pick_tile(mp, 512)
    aliased = [] if k_stack is None else [k_stack, v_stack]
    prompt_spec = pl.BlockSpec((tm, width), lambda i, j: (i, 0))
    stack_spec = pl.BlockSpec((None, tm, width), lambda i, j: (layer, i, 0))
    rest_spec = pl.BlockSpec((n_rest, width), lambda i, j: (0, 0))
    prompt_shape = jax.ShapeDtypeStruct((mp, width), F32)
    stack_shape = jax.ShapeDtypeStruct((n_layers, mp, width), F32)
    rest_shape = jax.ShapeDtypeStruct((n_rest, width), F32)
    qp, qr, kp, kr, vp, vr, up, ur = pl.pallas_call(
        functools.partial(_hyb_in_kernel, n_alias=len(aliased)),
        grid=(mp // tm, 2),
        in_specs=[
            pl.BlockSpec((tm, k), lambda i, j: (i, 0)),
            pl.BlockSpec((n_rest, k), lambda i, j: (mp // n_rest, 0)),
            pl.BlockSpec((None, k, 2 * width), lambda i, j: (layer, 0, j)),
        ] + [pl.BlockSpec(memory_space=pl.ANY)] * len(aliased),
        out_specs=[prompt_spec, rest_spec, stack_spec, rest_spec, stack_spec, rest_spec, prompt_spec, rest_spec],
        out_shape=[prompt_shape, rest_shape, stack_shape, rest_shape, stack_shape, rest_shape, prompt_shape,
                   rest_shape],
        input_output_aliases={3: 2, 4: 4} if aliased else {},
        compiler_params=_cparams(("parallel", "arbitrary")),
        name="hyb_in_proj",
    )(xb, xb, w, *aliased)
    return (qp, qr), (kp, kr), (vp, vr), (up, ur)


def _mm_res_ln_kernel(*refs, nparts, gated, alpha):
    a_refs = refs[:nparts]
    w_refs = refs[nparts:2 * nparts]
    rest = refs[2 * nparts:]
    if gated:
        gw_ref, gb_ref = rest[:2]
        rest = rest[2:]
    x32_ref, g_ref, b_ref, o32_ref, ob_ref = rest
    y = alpha * x32_ref[...]
    for r, (a_ref, w_ref) in enumerate(zip(a_refs, w_refs)):
        a = a_ref[...]
        if gated and r == nparts - 1:
            gate = jnp.dot(a.astype(BF16), gw_ref[...], preferred_element_type=F32) + gb_ref[...]
            a = a * jax.nn.sigmoid(gate)
        y = y + jnp.dot(a.astype(BF16), w_ref[...], preferred_element_type=F32)
    z = _layer_norm(y, g_ref[...], b_ref[...])
    o32_ref[...] = z
    ob_ref[...] = z.astype(BF16)


def _mm_res_ln(parts, w, layer, x32, g, b, alpha, tm, glu=None):
    m, d = x32.shape
    kp = parts[0].shape[1]
    assert all(p.shape[1] == kp for p in parts)
    once = dict(pipeline_mode=pl.Buffered(1))
    in_specs = [pl.BlockSpec((tm, kp), lambda i: (i, 0)) for _ in parts]
    in_specs += [pl.BlockSpec((None, kp, d), lambda i, r=r: (layer, r, 0), **once) for r in range(len(parts))]
    operands = [*parts, *([w] * len(parts))]
    if glu is not None:
        in_specs += [pl.BlockSpec((None, kp, kp), lambda i: (layer, 0, 0), **once), pl.BlockSpec((1, kp), lambda i: (0, 0))]
        operands += list(glu)
    in_specs += [
        pl.BlockSpec((tm, d), lambda i: (i, 0)),
        pl.BlockSpec((1, d), lambda i: (0, 0)),
        pl.BlockSpec((1, d), lambda i: (0, 0)),
    ]
    return pl.pallas_call(
        functools.partial(_mm_res_ln_kernel, nparts=len(parts), gated=glu is not None, alpha=alpha),
        grid=(m // tm,),
        in_specs=in_specs,
        out_specs=[pl.BlockSpec((tm, d), lambda i: (i, 0)), pl.BlockSpec((tm, d), lambda i: (i, 0))],
        out_shape=[jax.ShapeDtypeStruct((m, d), F32), jax.ShapeDtypeStruct((m, d), BF16)],
        compiler_params=_cparams(("parallel",)),
        name="out_proj_ln",
    )(*operands, x32, g, b)


CONV_PAD = 8
CONV_COL_SLAB = 256


def _conv_tile(project, st_ref, w_ref, c_ref, ns_ref, hbuf, *, tt, nt, cw):
    t = pl.program_id(2)
    lo = CONV_PAD - (cw - 1)
    tn = c_ref.shape[1]
    slab = min(tn, CONV_COL_SLAB)

    @pl.when(t == 0)
    def _():
        hbuf[lo:CONV_PAD, :] = st_ref[...]

    for c0 in range(0, tn, slab):
        cols = slice(c0, c0 + slab)
        bg, cg, v = project(cols)
        hbuf[CONV_PAD:CONV_PAD + tt, cols] = cg * v
        w = w_ref[:, cols]
        conv = w[0:1, :] * hbuf[lo:lo + tt, cols]
        for j in range(1, cw):
            conv = conv + w[j:j + 1, :] * hbuf[lo + j:lo + j + tt, cols]
        c_ref[:, cols] = (bg * conv).astype(c_ref.dtype)
    tail = hbuf[lo + tt:CONV_PAD + tt, :]

    @pl.when(t == nt - 1)
    def _():
        ns_ref[...] = tail

    hbuf[lo:CONV_PAD, :] = tail


def _conv_kernel(bg_ref, cg_ref, v_ref, st_ref, w_ref, c_ref, ns_ref, hbuf, **kw):
    _conv_tile(lambda cols: (bg_ref[:, cols], cg_ref[:, cols], v_ref[:, cols]), st_ref, w_ref, c_ref, ns_ref, hbuf,
               **kw)


def _conv_in_kernel(x_ref, wb_ref, wc_ref, wv_ref, st_ref, w_ref, c_ref, ns_ref, hbuf, **kw):
    def project(cols):
        x = x_ref[...]
        return tuple(jnp.dot(x, w[:, cols], preferred_element_type=F32) for w in (wb_ref, wc_ref, wv_ref))

    _conv_tile(project, st_ref, w_ref, c_ref, ns_ref, hbuf, **kw)


def _conv_core(z, state, conv_w, layer, bsz, t_len, tt, tn):
    m, d3 = z.shape
    d = d3 // 3
    cw = conv_w.shape[1]
    nt, nj = t_len // tt, d // tn
    return pl.pallas_call(
        functools.partial(_conv_kernel, tt=tt, nt=nt, cw=cw),
        grid=(bsz, nj, nt),
        in_specs=[
            pl.BlockSpec((tt, tn), lambda b, j, t: (b * nt + t, j)),
            pl.BlockSpec((tt, tn), lambda b, j, t: (b * nt + t, j + nj)),
            pl.BlockSpec((tt, tn), lambda b, j, t: (b * nt + t, j + 2 * nj)),
            pl.BlockSpec((None, cw - 1, tn), lambda b, j, t: (b, 0, j)),
            pl.BlockSpec((None, cw, tn), lambda b, j, t: (layer, 0, j)),
        ],
        out_specs=[
            pl.BlockSpec((tt, tn), lambda b, j, t: (b * nt + t, j)),
            pl.BlockSpec((None, cw - 1, tn), lambda b, j, t: (b, 0, j)),
        ],
        out_shape=[jax.ShapeDtypeStruct((m, d), BF16), jax.ShapeDtypeStruct((bsz, cw - 1, d), F32)],
        scratch_shapes=[pltpu.VMEM((CONV_PAD + tt, tn), F32)],
        compiler_params=_cparams(("parallel", "parallel", "arbitrary")),
        name="conv_core",
    )(z, z, z, state, conv_w)


def _conv_in_core(xb, w_in, state, conv_w, layer, bsz, t_len, tt, tn):
    m, d = xb.shape
    cw = conv_w.shape[1]
    nt, nj = t_len // tt, d // tn
    return pl.pallas_call(
        functools.partial(_conv_in_kernel, tt=tt, nt=nt, cw=cw),
        grid=(bsz, nj, nt),
        in_specs=[
            pl.BlockSpec((tt, d), lambda b, j, t: (b * nt + t, 0)),
            pl.BlockSpec((None, d, tn), lambda b, j, t: (layer, 0, j)),
            pl.BlockSpec((None, d, tn), lambda b, j, t: (layer, 0, j + nj)),
            pl.BlockSpec((None, d, tn), lambda b, j, t: (layer, 0, j + 2 * nj)),
            pl.BlockSpec((None, cw - 1, tn), lambda b, j, t: (b, 0, j)),
            pl.BlockSpec((None, cw, tn), lambda b, j, t: (layer, 0, j)),
        ],
        out_specs=[
            pl.BlockSpec((tt, tn), lambda b, j, t: (b * nt + t, j)),
            pl.BlockSpec((None, cw - 1, tn), lambda b, j, t: (b, 0, j)),
        ],
        out_shape=[jax.ShapeDtypeStruct((m, d), BF16), jax.ShapeDtypeStruct((bsz, cw - 1, d), F32)],
        scratch_shapes=[pltpu.VMEM((CONV_PAD + tt, tn), F32)],
        compiler_params=_cparams(("parallel", "parallel", "arbitrary")),
        name="conv_in_core",
    )(xb, w_in, w_in, w_in, state, conv_w)


def _moba_select_rows(gate_t, n_valid):
    nb, nq = gate_t.shape
    sub = lax.broadcasted_iota(jnp.int32, (nb, nq), 0)
    valid = sub < n_valid
    gate_t = jnp.where(valid, gate_t, NEG_INF)
    beaten_by = jnp.zeros((nb, nq), jnp.int32)
    for jp in range(nb):
        row = gate_t[jp:jp + 1, :]
        beats = (row > gate_t) | ((row == gate_t) & (sub > jp))
        beaten_by = beaten_by + jnp.where(beats, 1, 0)
    return valid & (beaten_by < MOBA_TOPK)


def _moba_prompt_kernel(q_ref, k_ref, v_ref, o_ref, kb_scr, vt_scr, km_scr, *, nb, scale):
    blk = MOBA_BLOCK
    kb_scr[...] = k_ref[...].astype(BF16)
    for j in range(nb):
        rows = slice(j * blk, (j + 1) * blk)
        km_scr[j:j + 1, :] = jnp.mean(k_ref[rows, :], axis=0, keepdims=True)
        vt_scr[:, rows] = v_ref[rows, :].T.astype(BF16)
    causal = lax.broadcasted_iota(jnp.int32, (blk, blk), 0) <= lax.broadcasted_iota(jnp.int32, (blk, blk), 1)
    causal_bias = jnp.where(causal, 0.0, NEG_INF).astype(F32)

    for qi in range(nb):
        n_keys = (qi + 1) * blk
        q = q_ref[qi * blk:(qi + 1) * blk, :]
        s_all = lax.dot_general(kb_scr[0:n_keys, :], q.astype(BF16), (((1,), (1,)), ((), ())),
                                preferred_element_type=F32) * scale
        if qi > 0:
            gate_t = lax.dot_general(km_scr[...], q, (((1,), (1,)), ((), ())),
                                     precision=lax.Precision.HIGHEST, preferred_element_type=F32)
            sel_bias = jnp.where(_moba_select_rows(gate_t, qi), 0.0, NEG_INF).astype(F32)
        s_blocks = [s_all[j * blk:(j + 1) * blk, :] + sel_bias[j:j + 1, :] for j in range(qi)]
        s_blocks.append(s_all[qi * blk:n_keys, :] + causal_bias)
        top = s_blocks[0]
        for sb in s_blocks[1:]:
            top = jnp.maximum(top, sb)
        m = jnp.max(top, axis=0, keepdims=True)
        p_blocks = [jnp.exp(sb - m) for sb in s_blocks]
        tot = p_blocks[0]
        for pb in p_blocks[1:]:
            tot = tot + pb
        l = jnp.sum(tot, axis=0, keepdims=True)
        p_all = jnp.concatenate([pb.astype(BF16) for pb in p_blocks], axis=0)
        acc = jnp.dot(vt_scr[:, 0:n_keys], p_all, preferred_element_type=F32)
        o_ref[qi * blk:(qi + 1) * blk, :] = (acc / l).T.astype(o_ref.dtype)


def _moba_prompt(q, k_stack, v_stack, layer, bsz, t_len, nh, hd, m_out):
    m = m_out
    assert t_len % MOBA_BLOCK == 0
    nb = t_len // MOBA_BLOCK
    seq_spec = pl.BlockSpec((t_len, hd), lambda b, h: (b, h))
    kv_spec = pl.BlockSpec((None, t_len, hd), lambda b, h: (layer, b, h))
    return pl.pallas_call(
        functools.partial(_moba_prompt_kernel, nb=nb, scale=hd ** -0.5),
        grid=(bsz, nh),
        in_specs=[seq_spec, kv_spec, kv_spec],
        out_specs=seq_spec,
        out_shape=jax.ShapeDtypeStruct((m, nh * hd), BF16),
        scratch_shapes=[pltpu.VMEM((t_len, hd), BF16), pltpu.VMEM((hd, t_len), BF16),
                        pltpu.VMEM((nb, hd), F32)],
        compiler_params=_cparams(("parallel", "parallel")),
        name="moba_prompt",
    )(q, k_stack, v_stack)


def _moba_select_grouped(gate, n_valid_blocks, group):
    rows, nl = gate.shape
    lane = lax.broadcasted_iota(jnp.int32, (rows, nl), 1)
    valid = lane < n_valid_blocks * group
    gate = jnp.where(valid, gate, NEG_INF)
    beaten_by = jnp.zeros((rows, nl), jnp.int32)
    for jp in range(0, nl, group):
        col = gate[:, jp:jp + 1]
        beats = (col > gate) | ((col == gate) & (lane >= jp + group))
        beaten_by = beaten_by + jnp.where(beats, 1, 0)
    return valid & (beaten_by < MOBA_TOPK)


def _fold_lanes(x, op):
    tiles = [x[:, i:i + LANES] for i in range(0, x.shape[1], LANES)]
    out = tiles[0]
    for t in tiles[1:]:
        out = op(out, t)
    return out


def _sample_attn_kernel(pt_ref, q_ref, kn_ref, vn_ref, hbias_ref, obias_ref, *refs, nh, n_past_blocks,
                        pages_per_step, pages_per_block, scale):
    del pt_ref
    kp_refs = refs[:pages_per_step]
    vp_refs = refs[pages_per_step:2 * pages_per_step]
    o_ref, gate_scr, m_scr, l_scr, acc_scr = refs[2 * pages_per_step:]
    step = pl.program_id(1)
    n_steps = pl.num_programs(1)
    rows, hd = q_ref.shape
    n_pages = m_scr.shape[1]
    q = q_ref[...]
    qb = q.astype(BF16)
    q3 = q.reshape(rows // nh, nh, hd)
    lane = lax.broadcasted_iota(jnp.int32, (rows, n_pages), 1)

    def put_column(scr, pg, col):
        scr[...] = jnp.where(lane == pg, col, scr[...])

    for blk0 in range(0, pages_per_step, pages_per_block):
        ksum = jnp.zeros((nh, hd), F32)
        for r in range(blk0, blk0 + pages_per_block):
            pg = step * pages_per_step + r
            k = kp_refs[r][...]
            ksum = ksum + jnp.sum(k.reshape(k.shape[0] // nh, nh, hd), axis=0)
            s = lax.dot_general(qb, k.astype(BF16), (((1,), (1,)), ((), ())), preferred_element_type=F32)
            s = s * scale + hbias_ref[...]
            m = jnp.max(_fold_lanes(s, jnp.maximum), axis=1, keepdims=True)
            p = jnp.exp(s - m)
            put_column(m_scr, pg, m)
            put_column(l_scr, pg, jnp.sum(_fold_lanes(p, jnp.add), axis=1, keepdims=True))
            acc_scr[pg] = jnp.dot(p.astype(BF16), vp_refs[r][...].astype(BF16), preferred_element_type=F32)
        kmean = ksum / MOBA_BLOCK
        gcol = jnp.sum(q3 * kmean[None, :, :], axis=2, keepdims=True).reshape(rows, 1)
        for r in range(blk0, blk0 + pages_per_block):
            put_column(gate_scr, step * pages_per_step + r, gcol)

    @pl.when(step == n_steps - 1)
    def _():
        keep = _moba_select_grouped(gate_scr[...], n_past_blocks, pages_per_block)
        m_pages = m_scr[...] + jnp.where(keep, 0.0, NEG_INF)
        s = lax.dot_general(qb, kn_ref[...].astype(BF16), (((1,), (1,)), ((), ())), preferred_element_type=F32)
        s = s * scale + obias_ref[...]
        m_own = jnp.max(s, axis=1, keepdims=True)
        m_tot = jnp.maximum(m_own, jnp.max(m_pages, axis=1, keepdims=True))
        p = jnp.exp(s - m_tot)
        w = jnp.exp(m_pages - m_tot)
        l = jnp.sum(p, axis=1, keepdims=True) + jnp.sum(w * l_scr[...], axis=1, keepdims=True)
        acc = jnp.dot(p.astype(BF16), vn_ref[...].astype(BF16), preferred_element_type=F32)
        for pg in range(n_pages):
            acc = acc + w[:, pg:pg + 1] * acc_scr[pg]
        o_ref[...] = (acc / l).astype(o_ref.dtype)


def _sample_attn(q, k_new, v_new, cache_k, cache_v, layer, page_table, db, t_len, nh, hd, pages_per_step=8):
    n_hyb, n_pool, page, _, _ = cache_k.shape
    n_pages = page_table.shape[1]
    past_len = n_pages * page
    assert past_len % MOBA_BLOCK == 0 and t_len <= MOBA_BLOCK and MOBA_BLOCK % page == 0
    assert nh & (nh - 1) == 0, "head matching uses bit masks"
    ppb = MOBA_BLOCK // page
    assert pages_per_step % ppb == 0 and n_pages % pages_per_step == 0
    rows = t_len * nh
    ck = cache_k.reshape(n_hyb, n_pool, page * nh, hd)
    cv = cache_v.reshape(n_hyb, n_pool, page * nh, hd)
    r = jnp.arange(rows, dtype=jnp.int32)[:, None]
    c = jnp.arange(page * nh, dtype=jnp.int32)[None, :]
    same_head = (r & (nh - 1)) == (c & (nh - 1))
    head_bias = jnp.where(same_head, 0.0, NEG_INF).astype(F32)
    own_bias = jnp.where(same_head[:, :rows] & (c[:, :rows] // nh <= r // nh), 0.0, NEG_INF).astype(F32)

    def page_spec(i):
        return pl.BlockSpec((None, None, page * nh, hd),
                            lambda b, s, pt: (layer, pt[b, s * pages_per_step + i], 0, 0))

    def row_spec(nrows):
        return pl.BlockSpec((None, nrows, hd), lambda b, s, pt: (b, 0, 0))

    full = lambda a: pl.BlockSpec(a.shape, lambda b, s, pt: (0,) * a.ndim)
    out = pl.pallas_call(
        functools.partial(_sample_attn_kernel, nh=nh, n_past_blocks=past_len // MOBA_BLOCK,
                          pages_per_step=pages_per_step, pages_per_block=ppb, scale=hd ** -0.5),
        grid_spec=pltpu.PrefetchScalarGridSpec(
            num_scalar_prefetch=1,
            grid=(db, n_pages // pages_per_step),
            in_specs=[row_spec(rows), row_spec(rows), row_spec(rows), full(head_bias), full(own_bias)]
            + [page_spec(i) for i in range(pages_per_step)] * 2,
            out_specs=row_spec(rows),
            scratch_shapes=[pltpu.VMEM((rows, n_pages), F32), pltpu.VMEM((rows, n_pages), F32),
                            pltpu.VMEM((rows, n_pages), F32), pltpu.VMEM((n_pages, rows, hd), F32)],
        ),
        out_shape=jax.ShapeDtypeStruct((db, rows, hd), BF16),
        compiler_params=_cparams(("parallel", "arbitrary")),
        name="sample_attn",
    )(page_table, q.reshape(db, rows, hd), k_new.reshape(db, rows, hd), v_new.reshape(db, rows, hd),
      head_bias, own_bias, *([ck] * pages_per_step), *([cv] * pages_per_step))
    return out.reshape(db * t_len, nh * hd)


def _s5_kernel(u_ref, perm_ref, unperm_ref, bre_ref, bim_ref, cre_ref, cimn_ref, lre_ref, lim_ref, llre_ref, llim_ref,
               d_ref, h0re_ref, h0im_ref, y_ref, htre_ref, htim_ref, hr_scr, hi_scr, cre_scr, cim_scr, yc_scr,
               *, seg_len, chained, n_chunks):
    c = pl.program_id(1)
    nseg = SUBLANES
    nstate = hr_scr.shape[1]
    n_kb = bre_ref.shape[0]
    ku = bre_ref.shape[1]
    ks = bre_ref.shape[2]

    up = jnp.dot(perm_ref[...], u_ref[...].astype(BF16), preferred_element_type=F32).astype(BF16)
    for kb in range(n_kb):
        ukb = up[:, kb * ku:(kb + 1) * ku]
        hr_scr[:, kb * ks:(kb + 1) * ks] = jnp.dot(ukb, bre_ref[kb], preferred_element_type=F32)
        hi_scr[:, kb * ks:(kb + 1) * ks] = jnp.dot(ukb, bim_ref[kb], preferred_element_type=F32)

    if chained:
        @pl.when(c == 0)
        def _():
            cre_scr[...] = jnp.zeros_like(cre_scr)
            cim_scr[...] = jnp.zeros_like(cim_scr)
            cre_scr[0:1, :] = h0re_ref[...]
            cim_scr[0:1, :] = h0im_ref[...]

    for lc in range(nstate // S5_SCAN_LANES):
        cols = slice(lc * S5_SCAN_LANES, (lc + 1) * S5_SCAN_LANES)
        lr = jnp.broadcast_to(lre_ref[:, cols], (nseg, S5_SCAN_LANES))
        li = jnp.broadcast_to(lim_ref[:, cols], (nseg, S5_SCAN_LANES))

        def advance(i, hr, hi):
            r = pl.multiple_of(i * nseg, nseg)
            nr = lr * hr - li * hi + hr_scr[pl.ds(r, nseg), cols]
            ni = lr * hi + li * hr + hi_scr[pl.ds(r, nseg), cols]
            return r, nr, ni

        def scan_only(i, carry):
            _, nr, ni = advance(i, *carry)
            return nr, ni

        def scan_store(i, carry):
            r, nr, ni = advance(i, *carry)
            hr_scr[pl.ds(r, nseg), cols] = nr
            hi_scr[pl.ds(r, nseg), cols] = ni
            return nr, ni

        if chained:
            er, ei = lax.fori_loop(0, seg_len, scan_only, (cre_scr[:, cols], cim_scr[:, cols]),
                                   unroll=S5_SCAN_UNROLL)
            llr, lli = llre_ref[:, cols], llim_ref[:, cols]
            fr, fi = er[0:1, :], ei[0:1, :]
            for j in range(1, nseg):
                cre_scr[j:j + 1, cols] = fr
                cim_scr[j:j + 1, cols] = fi
                if j < nseg - 1:
                    fr, fi = (er[j:j + 1, :] + (llr * fr - lli * fi), ei[j:j + 1, :] + (llr * fi + lli * fr))
            init = (cre_scr[:, cols], cim_scr[:, cols])
        else:
            init = (h0re_ref[:, cols], h0im_ref[:, cols])

        hr, hi = lax.fori_loop(0, seg_len, scan_store, init, unroll=S5_SCAN_UNROLL)

        if chained:
            last_r, last_i = hr[nseg - 1:nseg, :], hi[nseg - 1:nseg, :]
            cre_scr[:, cols] = jnp.zeros((nseg, S5_SCAN_LANES), F32)
            cim_scr[:, cols] = jnp.zeros((nseg, S5_SCAN_LANES), F32)
            cre_scr[0:1, cols] = last_r
            cim_scr[0:1, cols] = last_i

            @pl.when(c == n_chunks - 1)
            def _():
                htre_ref[:, cols] = last_r
                htim_ref[:, cols] = last_i
        else:
            htre_ref[:, cols] = hr
            htim_ref[:, cols] = hi

    for kb in range(n_kb):
        hre = hr_scr[:, kb * ks:(kb + 1) * ks].astype(BF16)
        him = hi_scr[:, kb * ks:(kb + 1) * ks].astype(BF16)
        ykb = (jnp.dot(hre, cre_ref[kb], preferred_element_type=F32)
               + jnp.dot(him, cimn_ref[kb], preferred_element_type=F32))
        yc_scr[:, kb * ku:(kb + 1) * ku] = ykb

    yc = yc_scr[...]
    hi = yc.astype(BF16)
    r1 = yc - hi.astype(F32)
    mid = r1.astype(BF16)
    lo = (r1 - mid.astype(F32)).astype(BF16)
    unperm = unperm_ref[...]
    y = (jnp.dot(unperm, hi, preferred_element_type=F32) + jnp.dot(unperm, mid, preferred_element_type=F32)
         + jnp.dot(unperm, lo, preferred_element_type=F32))
    y_ref[...] = jax.nn.gelu(y + d_ref[...] * u_ref[...])


def _s5_discretise(a_re, a_im, log_dt, b_re, b_im, c_re, c_im, d_skip, seg_len):
    g, p = a_re.shape
    ch = b_re.shape[-1]
    lam = lax.complex(a_re.astype(F32), a_im.astype(F32))
    dt = jnp.exp(log_dt.astype(F32))[:, None]
    lam_bar = jnp.exp(lam * dt)
    b_bar = ((lam_bar - 1.0) / lam)[..., None] * lax.complex(b_re.astype(F32), b_im.astype(F32))
    lam_seg = lam_bar
    assert seg_len & (seg_len - 1) == 0
    for _ in range(seg_len.bit_length() - 1):
        lam_seg = lam_seg * lam_seg
    gl = S5_LANE_GROUPS
    n_kb = g // gl
    eye = jnp.eye(gl, dtype=F32)

    def pack_b(x):
        x = x.reshape(n_kb, gl, p, ch).transpose(0, 1, 3, 2)
        return jnp.einsum("kgcp,gh->kgchp", x, eye).reshape(n_kb, gl * ch, gl * p)

    def pack_c(x):
        x = x.reshape(n_kb, gl, ch, p).transpose(0, 1, 3, 2)
        return jnp.einsum("kgpc,gh->kgphc", x, eye).reshape(n_kb, gl * p, gl * ch)

    flat = lambda x: x.reshape(1, g * p)
    return dict(
        bre=pack_b(jnp.real(b_bar)).astype(BF16), bim=pack_b(jnp.imag(b_bar)).astype(BF16),
        cre=pack_c(c_re.astype(F32)).astype(BF16), cimn=pack_c(-c_im.astype(F32)).astype(BF16),
        lre=flat(jnp.real(lam_bar)), lim=flat(jnp.imag(lam_bar)),
        llre=flat(jnp.real(lam_seg)), llim=flat(jnp.imag(lam_seg)),
        d=d_skip.astype(F32).reshape(1, g * ch),
    )


def _s5(u, h0_re, h0_im, prm, seg_len, chained, m_out=None):
    n_rows, w = u.shape
    m = n_rows if m_out is None else m_out
    gp = prm["lre"].shape[1]
    chunk = SUBLANES * seg_len
    if chained:
        nseq = h0_re.shape[0]
        n_chunks = n_rows // nseq // chunk
        grid = (nseq, n_chunks)
        u_spec = pl.BlockSpec((chunk, w), lambda b, c: (b * n_chunks + c, 0))
        h_spec = pl.BlockSpec((None, 1, gp), lambda b, c: (b, 0, 0))
        h_shape = jax.ShapeDtypeStruct((nseq, 1, gp), F32)
    else:
        assert n_rows == chunk
        n_chunks = 1
        grid = (1, 1)
        u_spec = pl.BlockSpec((chunk, w), lambda b, c: (0, 0))
        h_spec = pl.BlockSpec((SUBLANES, gp), lambda b, c: (0, 0))
        h_shape = jax.ShapeDtypeStruct((SUBLANES, gp), F32)
    full = lambda a: pl.BlockSpec(a.shape, lambda b, c: (0,) * a.ndim)
    names = ("bre", "bim", "cre", "cimn", "lre", "lim", "llre", "llim", "d")
    r = jnp.arange(chunk)
    perm = (((r % SUBLANES) * seg_len + r // SUBLANES)[:, None] == r[None, :]).astype(BF16)
    consts = [perm, perm.T]
    return pl.pallas_call(
        functools.partial(_s5_kernel, seg_len=seg_len, chained=chained, n_chunks=n_chunks),
        grid=grid,
        in_specs=[u_spec] + [full(a) for a in consts] + [full(prm[n]) for n in names] + [h_spec, h_spec],
        out_specs=[u_spec, h_spec, h_spec],
        out_shape=[jax.ShapeDtypeStruct((m, w), F32), h_shape, h_shape],
        scratch_shapes=[pltpu.VMEM((chunk, gp), F32), pltpu.VMEM((chunk, gp), F32),
                        pltpu.VMEM((SUBLANES, gp), F32), pltpu.VMEM((SUBLANES, gp), F32),
                        pltpu.VMEM((chunk, w), F32)],
        compiler_params=_cparams(("parallel", "arbitrary")),
        name="s5_scan",
    )(u, *consts, *[prm[n] for n in names], h0_re, h0_im)


def _pick_tile(n, target):
    t = min(n, target)
    while n % t:
        t //= 2
    return t


def _row_tile(m):
    return min(range(LANES, MAX_ROW_TILE + 1, LANES), key=lambda t: (-(-m // t) * t, -t))


def _fill_rest(full, sample_rows, mp):
    pad = full.shape[0] - mp - sample_rows.shape[0]
    rest = jnp.concatenate([sample_rows.astype(full.dtype), jnp.zeros((pad, full.shape[1]), full.dtype)], axis=0)
    return lax.dynamic_update_slice(full, rest, (mp, 0))


def _run_trunk(x_prompt, x_sample, prm, s5_prompt, s5_sample, kv, s5_re0, s5_im0, conv0_prompt, conv0_sample):
    bp, tp, d = x_prompt.shape
    bs, ts, _ = x_sample.shape
    mp, ms = bp * tp, bs * ts
    tm = _row_tile(mp + ms)
    m_all = -(-(mp + ms) // tm) * tm
    depth = prm["ffn1_wgu"].shape[0]
    alpha = (2.0 * depth) ** 0.25
    nh, hd = kv["nh"], kv["hd"]
    moba_w = nh * hd
    g_cnt, p_cnt = prm["s5_a_re"].shape[1:]
    gp = g_cnt * p_cnt
    tf = _pick_tile(prm["ffn1_wd"].shape[1], 512)
    sample = slice(mp, mp + ms)

    x32 = jnp.concatenate([x_prompt.reshape(mp, d).astype(F32), x_sample.reshape(ms, d).astype(F32),
                           jnp.zeros((m_all - mp - ms, d), F32)], axis=0)
    xb = x32.astype(BF16)
    row = lambda a: a.reshape(1, -1).astype(F32)
    out = dict(hrp=[], hip=[], cvp=[], ks=[], vs=[], hrs=[], his=[], cvs=[])
    k_stack = v_stack = None
    wgu_b, wd_b = prm["ffn1_wgu"][:1].astype(BF16), prm["ffn1_wd"][:1].astype(BF16)
    for layer in range(depth):
        i = layer // 2
        g, b = prm["ln_g"][layer], prm["ln_b"][layer]
        x32, xb, wgu_b, wd_b = _ffn(x32, xb, wgu_b, wd_b, 0, row(g[0]), row(b[0]), alpha, tm, tf,
                                    convert_next=(prm["ffn2_wgu"], prm["ffn2_wd"], layer))
        if layer % 2 == 0:
            assert d - moba_w == moba_w
            (q, q_r), (k_stack, k_rest), (v_stack, v_rest), (u, u_r) = _hyb_in(xb, prm["hyb_w_in"], i, mp, k_stack,
                                                                               v_stack)
            attn = _moba_prompt(q, k_stack, v_stack, i, bp, tp, nh, hd, m_all)
            attn_s = _sample_attn(q_r[:ms], k_rest[:ms], v_rest[:ms], kv["cache_k"], kv["cache_v"], i,
                                  kv["page_table"], bs, ts, nh, hd)
            zero = jnp.zeros((bp, 1, gp), F32)
            y, hr_p, hi_p = _s5(u, zero, zero, s5_prompt[i], s5_prompt[i]["seg_len"], True, m_out=m_all)
            y_s, hr_s, hi_s = _s5(u_r[:ms], s5_re0[i].reshape(bs, gp), s5_im0[i].reshape(bs, gp), s5_sample[i],
                                  s5_sample[i]["seg_len"], False)
            x32, xb = _mm_res_ln([_fill_rest(attn, attn_s, mp), _fill_rest(y, y_s, mp)], prm["hyb_w_out"], i, x32,
                                 row(g[1]), row(b[1]), alpha, tm, glu=(prm["s5_glu_w"], row(prm["s5_glu_b"][i])))
            out["ks"].append(k_rest[:ms].reshape(bs, ts, nh, hd))
            out["vs"].append(v_rest[:ms].reshape(bs, ts, nh, hd))
            out["hrp"].append(hr_p.reshape(bp, g_cnt, p_cnt))
            out["hip"].append(hi_p.reshape(bp, g_cnt, p_cnt))
            out["hrs"].append(hr_s.reshape(bs, g_cnt, p_cnt))
            out["his"].append(hi_s.reshape(bs, g_cnt, p_cnt))
        else:
            c, cv_p = _conv_in_core(xb, prm["conv_w_in"], conv0_prompt[i], prm["conv_w"], i, bp, tp,
                                    _pick_tile(tp, 512), _pick_tile(d, 1024))
            z = _mm(xb[sample], prm["conv_w_in"], i, 0, 3 * d, ms, _pick_tile(d, 2048))
            c_s, cv_s = _conv_core(z, conv0_sample[i], prm["conv_w"], i, bs, ts, ts, _pick_tile(d, 512))
            x32, xb = _mm_res_ln([_fill_rest(c, c_s, mp)], prm["conv_w_out"], i, x32, row(g[1]), row(b[1]),
                                 alpha, tm)
            out["cvp"].append(cv_p)
            out["cvs"].append(cv_s)
        if layer < depth - 1:
            x32, xb, wgu_b, wd_b = _ffn(x32, xb, wgu_b, wd_b, 0, row(g[2]), row(b[2]), alpha, tm, tf,
                                        convert_next=(prm["ffn1_wgu"], prm["ffn1_wd"], layer + 1))
        else:
            y_prompt, y_rest = _ffn(x32, xb, wgu_b, wd_b, 0, row(g[2]), row(b[2]), alpha, tm, tf, mp=mp)
    out = {n: jnp.stack(a) for n, a in out.items()}
    n_hyb = k_stack.shape[0]
    out["kp"] = k_stack.reshape(n_hyb, bp, tp, nh, hd)
    out["vp"] = v_stack.reshape(n_hyb, bp, tp, nh, hd)
    return y_prompt.reshape(bp, tp, d), y_rest[:ms].reshape(bs, ts, d), out


def kernel(x_prompt, x_sample, cache_k, cache_v, state_s5_re, state_s5_im, state_conv, page_table, ffn1_wgu, ffn1_wd, ffn2_wgu, ffn2_wd, ln_g, ln_b, hyb_w_in, hyb_w_out, s5_a_re, s5_a_im, s5_log_dt, s5_b_re, s5_b_im, s5_c_re, s5_c_im, s5_d, s5_glu_w, s5_glu_b, conv_w_in, conv_w, conv_w_out):
    n_hyb, _, _, nh, hd = cache_k.shape
    bp, t_prompt, d = x_prompt.shape
    db, t_sample, _ = x_sample.shape
    n_conv, cw = conv_w.shape[:2]
    g_cnt, p_cnt = s5_a_re.shape[1:]
    assert db == SUBLANES, "the sample scan lays the decode batch along the vreg sublanes"
    assert (s5_b_re.shape[-1] * S5_LANE_GROUPS) == 128

    prm = dict(
        ffn1_wgu=ffn1_wgu, ffn1_wd=ffn1_wd, ffn2_wgu=ffn2_wgu, ffn2_wd=ffn2_wd,
        ln_g=ln_g, ln_b=ln_b, hyb_w_in=hyb_w_in.astype(BF16), hyb_w_out=hyb_w_out.astype(BF16),
        s5_a_re=s5_a_re, s5_glu_w=s5_glu_w.astype(BF16), s5_glu_b=s5_glu_b,
        conv_w_in=conv_w_in.astype(BF16), conv_w=conv_w.astype(F32), conv_w_out=conv_w_out.astype(BF16),
    )

    def s5_prms(seg_len):
        out = []
        for i in range(n_hyb):
            sp = _s5_discretise(s5_a_re[i], s5_a_im[i], s5_log_dt[i], s5_b_re[i], s5_b_im[i], s5_c_re[i],
                                s5_c_im[i], s5_d[i], seg_len)
            sp["seg_len"] = seg_len
            out.append(sp)
        return out

    prompt_seg = _pick_tile(t_prompt // SUBLANES, 32)
    kv = dict(nh=nh, hd=hd, cache_k=cache_k, cache_v=cache_v, page_table=page_table)
    conv_zero = jnp.zeros((n_conv, bp, cw - 1, d), x_prompt.dtype)
    y_p, y_s, o = _run_trunk(x_prompt, x_sample, prm, s5_prms(prompt_seg), s5_prms(t_sample), kv, state_s5_re,
                             state_s5_im, conv_zero, state_conv)
    return (y_p, y_s, o["kp"], o["vp"], o["hrp"], o["hip"], o["cvp"], o["ks"], o["vs"], o["hrs"], o["his"], o["cvs"])
```

```python
import functools

import jax
import jax.numpy as jnp
from jax import lax
from jax.experimental import pallas as pl
from jax.experimental.pallas import tpu as pltpu

MOBA_BLOCK = 256
MOBA_TOPK = 3
LN_EPS = 1e-5
LANES = 128
SUBLANES = 8
S5_LANE_GROUPS = 8
S5_SCAN_LANES = 512
S5_SCAN_UNROLL = 4
BF16_ROWS = 16
MAX_ROW_TILE = 640
VMEM_LIMIT_BYTES = 56 * 1024 * 1024

F32 = jnp.float32
BF16 = jnp.bfloat16
NEG_INF = float("-inf")


def _cparams(semantics):
    return pltpu.CompilerParams(dimension_semantics=semantics, vmem_limit_bytes=VMEM_LIMIT_BYTES)


def _layer_norm(y, g, b):
    mu = jnp.mean(y, axis=-1, keepdims=True)
    d = y - mu
    var = jnp.mean(d * d, axis=-1, keepdims=True)
    return d * lax.rsqrt(var + LN_EPS) * g + b


def _convert_slab(s, n_steps, src_ref, layer, dst_ref, inbuf, outbuf, sem_in, sem_out, rows):
    last_slab = src_ref.shape[1] // rows - 1

    def row0(k):
        return pl.multiple_of(jnp.minimum(k, last_slab) * rows, rows)

    def fetch(k, slot):
        return pltpu.make_async_copy(src_ref.at[layer, pl.ds(row0(k), rows), :], inbuf.at[slot], sem_in.at[slot])

    def send(k):
        return pltpu.make_async_copy(outbuf, dst_ref.at[0, pl.ds(row0(k), rows), :], sem_out)

    slot = lax.rem(s, 2)

    @pl.when(s == 0)
    def _():
        fetch(0, 0).start()
        outbuf[...] = jnp.zeros_like(outbuf)
        send(0).start()

    fetch(s, slot).wait()
    fetch(s + 1, 1 - slot).start()
    send(s).wait()
    outbuf[...] = inbuf[slot].astype(BF16)
    send(s).start()

    @pl.when(s == n_steps - 1)
    def _():
        fetch(s + 1, 1 - slot).wait()
        send(s).wait()


def _convert_rows(n_rows, n_steps):
    return next(r for r in range(BF16_ROWS, n_rows + 1, BF16_ROWS) if n_rows % r == 0 and n_rows // r <= n_steps)


def _ffn_kernel(*refs, alpha, nf, rest, converting):
    x32_ref, xb_ref, wg_ref, wu_ref, wd_ref, g_ref, b_ref = refs[:7]
    if converting:
        (ngu_src, nd_src, o32_ref, aux_ref, ngu_dst, nd_dst, gu_in, gu_out, d_in, d_out, sem_in, sem_out) = refs[7:]
    else:
        o32_ref, aux_ref = refs[7:]
    i, j = pl.program_id(0), pl.program_id(1)
    if converting:
        step, n_steps = i * nf + j, pl.num_programs(0) * nf
        _convert_slab(step, n_steps, ngu_src, converting[0], ngu_dst, gu_in, gu_out, sem_in.at[0], sem_out.at[0],
                      gu_in.shape[1])
        _convert_slab(step, n_steps, nd_src, converting[0], nd_dst, d_in, d_out, sem_in.at[1], sem_out.at[1],
                      d_in.shape[1])

    @pl.when(j == 0)
    def _():
        o32_ref[...] = jnp.zeros_like(o32_ref)

    xb = xb_ref[...]
    gate = jnp.dot(xb, wg_ref[...], preferred_element_type=F32)
    up = jnp.dot(xb, wu_ref[...], preferred_element_type=F32)
    h = (gate * jax.nn.sigmoid(gate)) * up
    o32_ref[...] += jnp.dot(h.astype(BF16), wd_ref[...], preferred_element_type=F32)

    @pl.when(j == nf - 1)
    def _():
        z = _layer_norm(alpha * x32_ref[...] + 0.5 * o32_ref[...], g_ref[...], b_ref[...])
        o32_ref[...] = z
        if rest is None:
            aux_ref[...] = z.astype(BF16)
        else:
            @pl.when(i == pl.num_programs(0) - 1)
            def _():
                aux_ref[...] = z[rest[0]:rest[0] + rest[1], :]


def _ffn(x32, xb, wgu, wd, layer, g, b, alpha, tm, tf, mp=None, convert_next=None):
    m, d = x32.shape
    f = wd.shape[1]
    nf = f // tf
    n_steps = (m // tm) * nf
    if mp is None:
        rest = None
        out_specs = [pl.BlockSpec((tm, d), lambda i, j: (i, 0)), pl.BlockSpec((tm, d), lambda i, j: (i, 0))]
        out_shape = [jax.ShapeDtypeStruct((m, d), F32), jax.ShapeDtypeStruct((m, d), BF16)]
    else:
        rest = _rest_rows(m, mp, tm)
        out_specs = [pl.BlockSpec((tm, d), lambda i, j: (i, 0)), pl.BlockSpec((rest[1], d), lambda i, j: (0, 0))]
        out_shape = [jax.ShapeDtypeStruct((mp, d), F32), jax.ShapeDtypeStruct((rest[1], d), F32)]
    in_specs = [
        pl.BlockSpec((tm, d), lambda i, j: (i, 0)),
        pl.BlockSpec((tm, d), lambda i, j: (i, 0)),
        pl.BlockSpec((None, d, tf), lambda i, j: (layer, 0, j)),
        pl.BlockSpec((None, d, tf), lambda i, j: (layer, 0, j + nf)),
        pl.BlockSpec((None, tf, d), lambda i, j: (layer, j, 0)),
        pl.BlockSpec((1, d), lambda i, j: (0, 0)),
        pl.BlockSpec((1, d), lambda i, j: (0, 0)),
    ]
    operands = [x32, xb, wgu, wgu, wd, g, b]
    scratch, converting = [], None
    if convert_next is not None:
        src_gu, src_d, next_layer = convert_next
        converting = (next_layer,)
        gu_rows, d_rows = _convert_rows(d, n_steps), _convert_rows(f, n_steps)
        any_spec = pl.BlockSpec(memory_space=pl.ANY)
        in_specs += [any_spec, any_spec]
        operands += [src_gu, src_d]
        out_specs += [any_spec, any_spec]
        out_shape += [jax.ShapeDtypeStruct((1, d, 2 * f), BF16), jax.ShapeDtypeStruct((1, f, d), BF16)]
        scratch = [pltpu.VMEM((2, gu_rows, 2 * f), F32), pltpu.VMEM((gu_rows, 2 * f), BF16),
                   pltpu.VMEM((2, d_rows, d), F32), pltpu.VMEM((d_rows, d), BF16),
                   pltpu.SemaphoreType.DMA((2, 2)), pltpu.SemaphoreType.DMA((2,))]
    return pl.pallas_call(
        functools.partial(_ffn_kernel, alpha=alpha, nf=nf, rest=rest, converting=converting),
        grid=(m // tm, nf),
        in_specs=in_specs,
        out_specs=out_specs,
        out_shape=out_shape,
        scratch_shapes=scratch,
        compiler_params=_cparams(("arbitrary", "arbitrary") if converting else ("parallel", "arbitrary")),
        name="ffn_ln",
    )(*operands)


def _mm_kernel(x_ref, w_ref, o_ref):
    o_ref[...] = jnp.dot(x_ref[...], w_ref[...], preferred_element_type=F32)


def _mm(xb, w, layer, col0, n, tm, tn):
    m, k = xb.shape
    assert col0 % tn == 0 and n % tn == 0
    jb0 = col0 // tn
    return pl.pallas_call(
        _mm_kernel,
        grid=(m // tm, n // tn),
        in_specs=[
            pl.BlockSpec((tm, k), lambda i, j: (i, 0)),
            pl.BlockSpec((None, k, tn), lambda i, j: (layer, 0, jb0 + j)),
        ],
        out_specs=pl.BlockSpec((tm, tn), lambda i, j: (i, j)),
        out_shape=jax.ShapeDtypeStruct((m, n), F32),
        compiler_params=_cparams(("parallel", "arbitrary")),
        name="proj",
    )(xb, w)


def _rest_rows(m_all, mp, tm):
    first = mp - (m_all // tm - 1) * tm
    assert m_all % tm == 0 and 0 <= first and first + (m_all - mp) == tm
    return first, m_all - mp


def _hyb_in_kernel(x_ref, xr_ref, w_ref, *refs, n_alias):
    qp_ref, qr_ref, kp_ref, kr_ref, vp_ref, vr_ref, up_ref, ur_ref = refs[n_alias:]
    j, i = pl.program_id(0), pl.program_id(1)
    last = i == pl.num_programs(1) - 1
    w = w_ref[...]
    width = w.shape[1] // 2
    y = jnp.dot(x_ref[...], w, preferred_element_type=F32)

    def emit(lo_ref, hi_ref, lo_rest_ref, hi_rest_ref):
        lo_ref[...] = y[:, :width]
        hi_ref[...] = y[:, width:]

        @pl.when(last)
        def _():
            yr = jnp.dot(xr_ref[...], w, preferred_element_type=F32)
            lo_rest_ref[...] = yr[:, :width]
            hi_rest_ref[...] = yr[:, width:]

    @pl.when(j == 0)
    def _():
        emit(qp_ref, kp_ref, qr_ref, kr_ref)

    @pl.when(j == 1)
    def _():
        emit(vp_ref, up_ref, vr_ref, ur_ref)


def _hyb_in(xb, w, layer, mp, k_stack=None, v_stack=None):
    m, k = xb.shape
    n_layers = w.shape[0]
    width = w.shape[2] // 4
    n_rest = m - mp
    assert mp % n_rest == 0
    tm = _pick_tile(mp, 512)
    aliased = [] if k_stack is None else [k_stack, v_stack]
    n_tiles = mp // tm
    tile = (lambda j, i: i * (1 - j) + (n_tiles - 1) * j, lambda j, i: i * j)
    prompt_spec = [pl.BlockSpec((tm, width), lambda j, i, p=p: (tile[p](j, i), 0)) for p in (0, 1)]
    stack_spec = [pl.BlockSpec((None, tm, width), lambda j, i, p=p: (layer, tile[p](j, i), 0)) for p in (0, 1)]
    rest_spec = pl.BlockSpec((n_rest, width), lambda j, i: (0, 0))
    prompt_shape = jax.ShapeDtypeStruct((mp, width), F32)
    stack_shape = jax.ShapeDtypeStruct((n_layers, mp, width), F32)
    rest_shape = jax.ShapeDtypeStruct((n_rest, width), F32)
    qp, qr, kp, kr, vp, vr, up, ur = pl.pallas_call(
        functools.partial(_hyb_in_kernel, n_alias=len(aliased)),
        grid=(2, n_tiles),
        in_specs=[
            pl.BlockSpec((tm, k), lambda j, i: (i, 0)),
            pl.BlockSpec((n_rest, k), lambda j, i: (mp // n_rest, 0)),
            pl.BlockSpec((None, k, 2 * width), lambda j, i: (layer, 0, j)),
        ] + [pl.BlockSpec(memory_space=pl.ANY)] * len(aliased),
        out_specs=[prompt_spec[0], rest_spec, stack_spec[0], rest_spec, stack_spec[1], rest_spec, prompt_spec[1],
                   rest_spec],
        out_shape=[prompt_shape, rest_shape, stack_shape, rest_shape, stack_shape, rest_shape, prompt_shape,
                   rest_shape],
        input_output_aliases={3: 2, 4: 4} if aliased else {},
        compiler_params=_cparams(("arbitrary", "arbitrary")),
        name="hyb_in_proj",
    )(xb, xb, w, *aliased)
    return (qp, qr), (kp, kr), (vp, vr), (up, ur)


def _mm_res_ln_kernel(*refs, nparts, gated, alpha):
    a_refs = refs[:nparts]
    w_refs = refs[nparts:2 * nparts]
    rest = refs[2 * nparts:]
    if gated:
        gw_ref, gb_ref = rest[:2]
        rest = rest[2:]
    x32_ref, g_ref, b_ref, o32_ref, ob_ref = rest
    y = alpha * x32_ref[...]
    for r, (a_ref, w_ref) in enumerate(zip(a_refs, w_refs)):
        a = a_ref[...]
        if gated and r == nparts - 1:
            gate = jnp.dot(a.astype(BF16), gw_ref[...], preferred_element_type=F32) + gb_ref[...]
            a = a * jax.nn.sigmoid(gate)
        y = y + jnp.dot(a.astype(BF16), w_ref[...], preferred_element_type=F32)
    z = _layer_norm(y, g_ref[...], b_ref[...])
    o32_ref[...] = z
    ob_ref[...] = z.astype(BF16)


def _mm_res_ln(parts, w, layer, x32, g, b, alpha, tm, glu=None):
    m, d = x32.shape
    kp = parts[0].shape[1]
    assert all(p.shape[1] == kp for p in parts)
    once = dict(pipeline_mode=pl.Buffered(1))
    in_specs = [pl.BlockSpec((tm, kp), lambda i: (i, 0)) for _ in parts]
    in_specs += [pl.BlockSpec((None, kp, d), lambda i, r=r: (layer, r, 0), **once) for r in range(len(parts))]
    operands = [*parts, *([w] * len(parts))]
    if glu is not None:
        in_specs += [pl.BlockSpec((None, kp, kp), lambda i: (layer, 0, 0), **once), pl.BlockSpec((1, kp), lambda i: (0, 0))]
        operands += list(glu)
    in_specs += [
        pl.BlockSpec((tm, d), lambda i: (i, 0)),
        pl.BlockSpec((1, d), lambda i: (0, 0)),
        pl.BlockSpec((1, d), lambda i: (0, 0)),
    ]
    return pl.pallas_call(
        functools.partial(_mm_res_ln_kernel, nparts=len(parts), gated=glu is not None, alpha=alpha),
        grid=(m // tm,),
        in_specs=in_specs,
        out_specs=[pl.BlockSpec((tm, d), lambda i: (i, 0)), pl.BlockSpec((tm, d), lambda i: (i, 0))],
        out_shape=[jax.ShapeDtypeStruct((m, d), F32), jax.ShapeDtypeStruct((m, d), BF16)],
        compiler_params=_cparams(("parallel",)),
        name="out_proj_ln",
    )(*operands, x32, g, b)


CONV_PAD = 8
CONV_COL_SLAB = 256


def _conv_tile(project, st_ref, w_ref, c_ref, ns_ref, hbuf, *, tt, nt, cw):
    t = pl.program_id(2)
    lo = CONV_PAD - (cw - 1)
    tn = c_ref.shape[1]
    slab = min(tn, CONV_COL_SLAB)

    @pl.when(t == 0)
    def _():
        hbuf[lo:CONV_PAD, :] = st_ref[...]

    for c0 in range(0, tn, slab):
        cols = slice(c0, c0 + slab)
        bg, cg, v = project(cols)
        hbuf[CONV_PAD:CONV_PAD + tt, cols] = cg * v
        w = w_ref[:, cols]
        conv = w[0:1, :] * hbuf[lo:lo + tt, cols]
        for j in range(1, cw):
            conv = conv + w[j:j + 1, :] * hbuf[lo + j:lo + j + tt, cols]
        c_ref[:, cols] = (bg * conv).astype(c_ref.dtype)
    tail = hbuf[lo + tt:CONV_PAD + tt, :]

    @pl.when(t == nt - 1)
    def _():
        ns_ref[...] = tail

    hbuf[lo:CONV_PAD, :] = tail


def _conv_kernel(bg_ref, cg_ref, v_ref, st_ref, w_ref, c_ref, ns_ref, hbuf, **kw):
    _conv_tile(lambda cols: (bg_ref[:, cols], cg_ref[:, cols], v_ref[:, cols]), st_ref, w_ref, c_ref, ns_ref, hbuf,
               **kw)


def _conv_in_kernel(x_ref, wb_ref, wc_ref, wv_ref, st_ref, w_ref, c_ref, ns_ref, hbuf, **kw):
    def project(cols):
        x = x_ref[...]
        return tuple(jnp.dot(x, w[:, cols], preferred_element_type=F32) for w in (wb_ref, wc_ref, wv_ref))

    _conv_tile(project, st_ref, w_ref, c_ref, ns_ref, hbuf, **kw)


def _conv_core(z, state, conv_w, layer, bsz, t_len, tt, tn):
    m, d3 = z.shape
    d = d3 // 3
    cw = conv_w.shape[1]
    nt, nj = t_len // tt, d // tn
    return pl.pallas_call(
        functools.partial(_conv_kernel, tt=tt, nt=nt, cw=cw),
        grid=(bsz, nj, nt),
        in_specs=[
            pl.BlockSpec((tt, tn), lambda b, j, t: (b * nt + t, j)),
            pl.BlockSpec((tt, tn), lambda b, j, t: (b * nt + t, j + nj)),
            pl.BlockSpec((tt, tn), lambda b, j, t: (b * nt + t, j + 2 * nj)),
            pl.BlockSpec((None, cw - 1, tn), lambda b, j, t: (b, 0, j)),
            pl.BlockSpec((None, cw, tn), lambda b, j, t: (layer, 0, j)),
        ],
        out_specs=[
            pl.BlockSpec((tt, tn), lambda b, j, t: (b * nt + t, j)),
            pl.BlockSpec((None, cw - 1, tn), lambda b, j, t: (b, 0, j)),
        ],
        out_shape=[jax.ShapeDtypeStruct((m, d), BF16), jax.ShapeDtypeStruct((bsz, cw - 1, d), F32)],
        scratch_shapes=[pltpu.VMEM((CONV_PAD + tt, tn), F32)],
        compiler_params=_cparams(("parallel", "parallel", "arbitrary")),
        name="conv_core",
    )(z, z, z, state, conv_w)


def _conv_in_core(xb, w_in, state, conv_w, layer, bsz, t_len, tt, tn):
    m, d = xb.shape
    cw = conv_w.shape[1]
    nt, nj = t_len // tt, d // tn
    return pl.pallas_call(
        functools.partial(_conv_in_kernel, tt=tt, nt=nt, cw=cw),
        grid=(bsz, nj, nt),
        in_specs=[
            pl.BlockSpec((tt, d), lambda b, j, t: (b * nt + t, 0)),
            pl.BlockSpec((None, d, tn), lambda b, j, t: (layer, 0, j)),
            pl.BlockSpec((None, d, tn), lambda b, j, t: (layer, 0, j + nj)),
            pl.BlockSpec((None, d, tn), lambda b, j, t: (layer, 0, j + 2 * nj)),
            pl.BlockSpec((None, cw - 1, tn), lambda b, j, t: (b, 0, j)),
            pl.BlockSpec((None, cw, tn), lambda b, j, t: (layer, 0, j)),
        ],
        out_specs=[
            pl.BlockSpec((tt, tn), lambda b, j, t: (b * nt + t, j)),
            pl.BlockSpec((None, cw - 1, tn), lambda b, j, t: (b, 0, j)),
        ],
        out_shape=[jax.ShapeDtypeStruct((m, d), BF16), jax.ShapeDtypeStruct((bsz, cw - 1, d), F32)],
        scratch_shapes=[pltpu.VMEM((CONV_PAD + tt, tn), F32)],
        compiler_params=_cparams(("parallel", "parallel", "arbitrary")),
        name="conv_in_core",
    )(xb, w_in, w_in, w_in, state, conv_w)


def _moba_select_rows(gate_t, n_valid):
    nb, nq = gate_t.shape
    sub = lax.broadcasted_iota(jnp.int32, (nb, nq), 0)
    valid = sub < n_valid
    gate_t = jnp.where(valid, gate_t, NEG_INF)
    beaten_by = jnp.zeros((nb, nq), jnp.int32)
    for jp in range(nb):
        row = gate_t[jp:jp + 1, :]
        beats = (row > gate_t) | ((row == gate_t) & (sub > jp))
        beaten_by = beaten_by + jnp.where(beats, 1, 0)
    return valid & (beaten_by < MOBA_TOPK)


def _moba_prompt_kernel(q_ref, k_ref, v_ref, o_ref, kb_scr, vt_scr, km_scr, *, nb, scale):
    blk = MOBA_BLOCK
    kb_scr[...] = k_ref[...].astype(BF16)
    for j in range(nb):
        rows = slice(j * blk, (j + 1) * blk)
        km_scr[j:j + 1, :] = jnp.mean(k_ref[rows, :], axis=0, keepdims=True)
        vt_scr[:, rows] = v_ref[rows, :].T.astype(BF16)
    causal = lax.broadcasted_iota(jnp.int32, (blk, blk), 0) <= lax.broadcasted_iota(jnp.int32, (blk, blk), 1)
    causal_bias = jnp.where(causal, 0.0, NEG_INF).astype(F32)

    for qi in range(nb):
        n_keys = (qi + 1) * blk
        q = q_ref[qi * blk:(qi + 1) * blk, :]
        s_all = lax.dot_general(kb_scr[0:n_keys, :], q.astype(BF16), (((1,), (1,)), ((), ())),
                                preferred_element_type=F32) * scale
        if qi > 0:
            gate_t = lax.dot_general(km_scr[...], q, (((1,), (1,)), ((), ())),
                                     precision=lax.Precision.HIGHEST, preferred_element_type=F32)
            sel_bias = jnp.where(_moba_select_rows(gate_t, qi), 0.0, NEG_INF).astype(F32)
        s_blocks = [s_all[j * blk:(j + 1) * blk, :] + sel_bias[j:j + 1, :] for j in range(qi)]
        s_blocks.append(s_all[qi * blk:n_keys, :] + causal_bias)
        top = s_blocks[0]
        for sb in s_blocks[1:]:
            top = jnp.maximum(top, sb)
        m = jnp.max(top, axis=0, keepdims=True)
        p_blocks = [jnp.exp(sb - m) for sb in s_blocks]
        tot = p_blocks[0]
        for pb in p_blocks[1:]:
            tot = tot + pb
        l = jnp.sum(tot, axis=0, keepdims=True)
        p_all = jnp.concatenate([pb.astype(BF16) for pb in p_blocks], axis=0)
        acc = jnp.dot(vt_scr[:, 0:n_keys], p_all, preferred_element_type=F32)
        o_ref[qi * blk:(qi + 1) * blk, :] = (acc / l).T.astype(o_ref.dtype)


def _moba_prompt(q, k_stack, v_stack, layer, bsz, t_len, nh, hd, m_out):
    m = m_out
    assert t_len % MOBA_BLOCK == 0
    nb = t_len // MOBA_BLOCK
    seq_spec = pl.BlockSpec((t_len, hd), lambda b, h: (b, h))
    kv_spec = pl.BlockSpec((None, t_len, hd), lambda b, h: (layer, b, h))
    return pl.pallas_call(
        functools.partial(_moba_prompt_kernel, nb=nb, scale=hd ** -0.5),
        grid=(bsz, nh),
        in_specs=[seq_spec, kv_spec, kv_spec],
        out_specs=seq_spec,
        out_shape=jax.ShapeDtypeStruct((m, nh * hd), BF16),
        scratch_shapes=[pltpu.VMEM((t_len, hd), BF16), pltpu.VMEM((hd, t_len), BF16),
                        pltpu.VMEM((nb, hd), F32)],
        compiler_params=_cparams(("parallel", "parallel")),
        name="moba_prompt",
    )(q, k_stack, v_stack)


def _moba_select_grouped(gate, n_valid_blocks, group):
    rows, nl = gate.shape
    lane = lax.broadcasted_iota(jnp.int32, (rows, nl), 1)
    valid = lane < n_valid_blocks * group
    gate = jnp.where(valid, gate, NEG_INF)
    beaten_by = jnp.zeros((rows, nl), jnp.int32)
    for jp in range(0, nl, group):
        col = gate[:, jp:jp + 1]
        beats = (col > gate) | ((col == gate) & (lane >= jp + group))
        beaten_by = beaten_by + jnp.where(beats, 1, 0)
    return valid & (beaten_by < MOBA_TOPK)


def _fold_lanes(x, op):
    tiles = [x[:, i:i + LANES] for i in range(0, x.shape[1], LANES)]
    out = tiles[0]
    for t in tiles[1:]:
        out = op(out, t)
    return out


def _sample_attn_kernel(pt_ref, q_ref, kn_ref, vn_ref, hbias_ref, obias_ref, *refs, nh, n_past_blocks,
                        pages_per_step, pages_per_block, scale):
    del pt_ref
    kp_refs = refs[:pages_per_step]
    vp_refs = refs[pages_per_step:2 * pages_per_step]
    o_ref, gate_scr, m_scr, l_scr, acc_scr = refs[2 * pages_per_step:]
    step = pl.program_id(1)
    n_steps = pl.num_programs(1)
    rows, hd = q_ref.shape
    n_pages = m_scr.shape[1]
    q = q_ref[...]
    qb = q.astype(BF16)
    q3 = q.reshape(rows // nh, nh, hd)
    lane = lax.broadcasted_iota(jnp.int32, (rows, n_pages), 1)

    def put_column(scr, pg, col):
        scr[...] = jnp.where(lane == pg, col, scr[...])

    for blk0 in range(0, pages_per_step, pages_per_block):
        ksum = jnp.zeros((nh, hd), F32)
        for r in range(blk0, blk0 + pages_per_block):
            pg = step * pages_per_step + r
            k = kp_refs[r][...]
            ksum = ksum + jnp.sum(k.reshape(k.shape[0] // nh, nh, hd), axis=0)
            s = lax.dot_general(qb, k.astype(BF16), (((1,), (1,)), ((), ())), preferred_element_type=F32)
            s = s * scale + hbias_ref[...]
            m = jnp.max(_fold_lanes(s, jnp.maximum), axis=1, keepdims=True)
            p = jnp.exp(s - m)
            put_column(m_scr, pg, m)
            put_column(l_scr, pg, jnp.sum(_fold_lanes(p, jnp.add), axis=1, keepdims=True))
            acc_scr[pg] = jnp.dot(p.astype(BF16), vp_refs[r][...].astype(BF16), preferred_element_type=F32)
        kmean = ksum / MOBA_BLOCK
        gcol = jnp.sum(q3 * kmean[None, :, :], axis=2, keepdims=True).reshape(rows, 1)
        for r in range(blk0, blk0 + pages_per_block):
            put_column(gate_scr, step * pages_per_step + r, gcol)

    @pl.when(step == n_steps - 1)
    def _():
        keep = _moba_select_grouped(gate_scr[...], n_past_blocks, pages_per_block)
        m_pages = m_scr[...] + jnp.where(keep, 0.0, NEG_INF)
        s = lax.dot_general(qb, kn_ref[...].astype(BF16), (((1,), (1,)), ((), ())), preferred_element_type=F32)
        s = s * scale + obias_ref[...]
        m_own = jnp.max(s, axis=1, keepdims=True)
        m_tot = jnp.maximum(m_own, jnp.max(m_pages, axis=1, keepdims=True))
        p = jnp.exp(s - m_tot)
        w = jnp.exp(m_pages - m_tot)
        l = jnp.sum(p, axis=1, keepdims=True) + jnp.sum(w * l_scr[...], axis=1, keepdims=True)
        acc = jnp.dot(p.astype(BF16), vn_ref[...].astype(BF16), preferred_element_type=F32)
        for pg in range(n_pages):
            acc = acc + w[:, pg:pg + 1] * acc_scr[pg]
        o_ref[...] = (acc / l).astype(o_ref.dtype)


def _sample_attn(q, k_new, v_new, cache_k, cache_v, layer, page_table, db, t_len, nh, hd, pages_per_step=16):
    n_hyb, n_pool, page, _, _ = cache_k.shape
    n_pages = page_table.shape[1]
    past_len = n_pages * page
    assert past_len % MOBA_BLOCK == 0 and t_len <= MOBA_BLOCK and MOBA_BLOCK % page == 0
    assert nh & (nh - 1) == 0, "head matching uses bit masks"
    ppb = MOBA_BLOCK // page
    assert pages_per_step % ppb == 0 and n_pages % pages_per_step == 0
    rows = t_len * nh
    ck = cache_k.reshape(n_hyb, n_pool, page * nh, hd)
    cv = cache_v.reshape(n_hyb, n_pool, page * nh, hd)
    r = jnp.arange(rows, dtype=jnp.int32)[:, None]
    c = jnp.arange(page * nh, dtype=jnp.int32)[None, :]
    same_head = (r & (nh - 1)) == (c & (nh - 1))
    head_bias = jnp.where(same_head, 0.0, NEG_INF).astype(F32)
    own_bias = jnp.where(same_head[:, :rows] & (c[:, :rows] // nh <= r // nh), 0.0, NEG_INF).astype(F32)

    def page_spec(i):
        return pl.BlockSpec((None, None, page * nh, hd),
                            lambda b, s, pt: (layer, pt[b, s * pages_per_step + i], 0, 0))

    def row_spec(nrows):
        return pl.BlockSpec((None, nrows, hd), lambda b, s, pt: (b, 0, 0))

    full = lambda a: pl.BlockSpec(a.shape, lambda b, s, pt: (0,) * a.ndim)
    out = pl.pallas_call(
        functools.partial(_sample_attn_kernel, nh=nh, n_past_blocks=past_len // MOBA_BLOCK,
                          pages_per_step=pages_per_step, pages_per_block=ppb, scale=hd ** -0.5),
        grid_spec=pltpu.PrefetchScalarGridSpec(
            num_scalar_prefetch=1,
            grid=(db, n_pages // pages_per_step),
            in_specs=[row_spec(rows), row_spec(rows), row_spec(rows), full(head_bias), full(own_bias)]
            + [page_spec(i) for i in range(pages_per_step)] * 2,
            out_specs=row_spec(rows),
            scratch_shapes=[pltpu.VMEM((rows, n_pages), F32), pltpu.VMEM((rows, n_pages), F32),
                            pltpu.VMEM((rows, n_pages), F32), pltpu.VMEM((n_pages, rows, hd), F32)],
        ),
        out_shape=jax.ShapeDtypeStruct((db, rows, hd), BF16),
        compiler_params=_cparams(("parallel", "arbitrary")),
        name="sample_attn",
    )(page_table, q.reshape(db, rows, hd), k_new.reshape(db, rows, hd), v_new.reshape(db, rows, hd),
      head_bias, own_bias, *([ck] * pages_per_step), *([cv] * pages_per_step))
    return out.reshape(db * t_len, nh * hd)


def _s5_kernel(u_ref, perm_ref, unperm_ref, bre_ref, bim_ref, cre_ref, cimn_ref, lre_ref, lim_ref, llre_ref, llim_ref,
               d_ref, h0re_ref, h0im_ref, y_ref, htre_ref, htim_ref, hr_scr, hi_scr, cre_scr, cim_scr, yc_scr,
               *, seg_len, chained, n_chunks):
    c = pl.program_id(1)
    nseg = SUBLANES
    nstate = hr_scr.shape[1]
    n_kb = bre_ref.shape[0]
    ku = bre_ref.shape[1]
    ks = bre_ref.shape[2]

    up = jnp.dot(perm_ref[...], u_ref[...].astype(BF16), preferred_element_type=F32).astype(BF16)
    for kb in range(n_kb):
        ukb = up[:, kb * ku:(kb + 1) * ku]
        hr_scr[:, kb * ks:(kb + 1) * ks] = jnp.dot(ukb, bre_ref[kb], preferred_element_type=F32)
        hi_scr[:, kb * ks:(kb + 1) * ks] = jnp.dot(ukb, bim_ref[kb], preferred_element_type=F32)

    if chained:
        @pl.when(c == 0)
        def _():
            cre_scr[...] = jnp.zeros_like(cre_scr)
            cim_scr[...] = jnp.zeros_like(cim_scr)
            cre_scr[0:1, :] = h0re_ref[...]
            cim_scr[0:1, :] = h0im_ref[...]

    for lc in range(nstate // S5_SCAN_LANES):
        cols = slice(lc * S5_SCAN_LANES, (lc + 1) * S5_SCAN_LANES)
        lr = jnp.broadcast_to(lre_ref[:, cols], (nseg, S5_SCAN_LANES))
        li = jnp.broadcast_to(lim_ref[:, cols], (nseg, S5_SCAN_LANES))

        def advance(i, hr, hi):
            r = pl.multiple_of(i * nseg, nseg)
            nr = lr * hr - li * hi + hr_scr[pl.ds(r, nseg), cols]
            ni = lr * hi + li * hr + hi_scr[pl.ds(r, nseg), cols]
            return r, nr, ni

        def scan_only(i, carry):
            _, nr, ni = advance(i, *carry)
            return nr, ni

        def scan_store(i, carry):
            r, nr, ni = advance(i, *carry)
            hr_scr[pl.ds(r, nseg), cols] = nr
            hi_scr[pl.ds(r, nseg), cols] = ni
            return nr, ni

        if chained:
            er, ei = lax.fori_loop(0, seg_len, scan_only, (cre_scr[:, cols], cim_scr[:, cols]),
                                   unroll=S5_SCAN_UNROLL)
            llr, lli = llre_ref[:, cols], llim_ref[:, cols]
            fr, fi = er[0:1, :], ei[0:1, :]
            for j in range(1, nseg):
                cre_scr[j:j + 1, cols] = fr
                cim_scr[j:j + 1, cols] = fi
                if j < nseg - 1:
                    fr, fi = (er[j:j + 1, :] + (llr * fr - lli * fi), ei[j:j + 1, :] + (llr * fi + lli * fr))
            init = (cre_scr[:, cols], cim_scr[:, cols])
        else:
            init = (h0re_ref[:, cols], h0im_ref[:, cols])

        hr, hi = lax.fori_loop(0, seg_len, scan_store, init, unroll=S5_SCAN_UNROLL)

        if chained:
            last_r, last_i = hr[nseg - 1:nseg, :], hi[nseg - 1:nseg, :]
            cre_scr[:, cols] = jnp.zeros((nseg, S5_SCAN_LANES), F32)
            cim_scr[:, cols] = jnp.zeros((nseg, S5_SCAN_LANES), F32)
            cre_scr[0:1, cols] = last_r
            cim_scr[0:1, cols] = last_i

            @pl.when(c == n_chunks - 1)
            def _():
                htre_ref[:, cols] = last_r
                htim_ref[:, cols] = last_i
        else:
            htre_ref[:, cols] = hr
            htim_ref[:, cols] = hi

    for kb in range(n_kb):
        hre = hr_scr[:, kb * ks:(kb + 1) * ks].astype(BF16)
        him = hi_scr[:, kb * ks:(kb + 1) * ks].astype(BF16)
        ykb = (jnp.dot(hre, cre_ref[kb], preferred_element_type=F32)
               + jnp.dot(him, cimn_ref[kb], preferred_element_type=F32))
        yc_scr[:, kb * ku:(kb + 1) * ku] = ykb

    yc = yc_scr[...]
    hi = yc.astype(BF16)
    r1 = yc - hi.astype(F32)
    mid = r1.astype(BF16)
    lo = (r1 - mid.astype(F32)).astype(BF16)
    unperm = unperm_ref[...]
    y = (jnp.dot(unperm, hi, preferred_element_type=F32) + jnp.dot(unperm, mid, preferred_element_type=F32)
         + jnp.dot(unperm, lo, preferred_element_type=F32))
    y_ref[...] = jax.nn.gelu(y + d_ref[...] * u_ref[...])


def _s5_discretise(a_re, a_im, log_dt, b_re, b_im, c_re, c_im, d_skip, seg_len):
    g, p = a_re.shape
    ch = b_re.shape[-1]
    lam = lax.complex(a_re.astype(F32), a_im.astype(F32))
    dt = jnp.exp(log_dt.astype(F32))[:, None]
    lam_bar = jnp.exp(lam * dt)
    b_bar = ((lam_bar - 1.0) / lam)[..., None] * lax.complex(b_re.astype(F32), b_im.astype(F32))
    lam_seg = lam_bar
    assert seg_len & (seg_len - 1) == 0
    for _ in range(seg_len.bit_length() - 1):
        lam_seg = lam_seg * lam_seg
    gl = S5_LANE_GROUPS
    n_kb = g // gl
    eye = jnp.eye(gl, dtype=F32)

    def pack_b(x):
        x = x.reshape(n_kb, gl, p, ch).transpose(0, 1, 3, 2)
        return jnp.einsum("kgcp,gh->kgchp", x, eye).reshape(n_kb, gl * ch, gl * p)

    def pack_c(x):
        x = x.reshape(n_kb, gl, ch, p).transpose(0, 1, 3, 2)
        return jnp.einsum("kgpc,gh->kgphc", x, eye).reshape(n_kb, gl * p, gl * ch)

    flat = lambda x: x.reshape(1, g * p)
    return dict(
        bre=pack_b(jnp.real(b_bar)).astype(BF16), bim=pack_b(jnp.imag(b_bar)).astype(BF16),
        cre=pack_c(c_re.astype(F32)).astype(BF16), cimn=pack_c(-c_im.astype(F32)).astype(BF16),
        lre=flat(jnp.real(lam_bar)), lim=flat(jnp.imag(lam_bar)),
        llre=flat(jnp.real(lam_seg)), llim=flat(jnp.imag(lam_seg)),
        d=d_skip.astype(F32).reshape(1, g * ch),
    )


def _s5(u, h0_re, h0_im, prm, seg_len, chained, m_out=None):
    n_rows, w = u.shape
    m = n_rows if m_out is None else m_out
    gp = prm["lre"].shape[1]
    chunk = SUBLANES * seg_len
    if chained:
        nseq = h0_re.shape[0]
        n_chunks = n_rows // nseq // chunk
        grid = (nseq, n_chunks)
        u_spec = pl.BlockSpec((chunk, w), lambda b, c: (b * n_chunks + c, 0))
        h_spec = pl.BlockSpec((None, 1, gp), lambda b, c: (b, 0, 0))
        h_shape = jax.ShapeDtypeStruct((nseq, 1, gp), F32)
    else:
        assert n_rows == chunk
        n_chunks = 1
        grid = (1, 1)
        u_spec = pl.BlockSpec((chunk, w), lambda b, c: (0, 0))
        h_spec = pl.BlockSpec((SUBLANES, gp), lambda b, c: (0, 0))
        h_shape = jax.ShapeDtypeStruct((SUBLANES, gp), F32)
    full = lambda a: pl.BlockSpec(a.shape, lambda b, c: (0,) * a.ndim)
    names = ("bre", "bim", "cre", "cimn", "lre", "lim", "llre", "llim", "d")
    r = jnp.arange(chunk)
    perm = (((r % SUBLANES) * seg_len + r // SUBLANES)[:, None] == r[None, :]).astype(BF16)
    consts = [perm, perm.T]
    return pl.pallas_call(
        functools.partial(_s5_kernel, seg_len=seg_len, chained=chained, n_chunks=n_chunks),
        grid=grid,
        in_specs=[u_spec] + [full(a) for a in consts] + [full(prm[n]) for n in names] + [h_spec, h_spec],
        out_specs=[u_spec, h_spec, h_spec],
        out_shape=[jax.ShapeDtypeStruct((m, w), F32), h_shape, h_shape],
        scratch_shapes=[pltpu.VMEM((chunk, gp), F32), pltpu.VMEM((chunk, gp), F32),
                        pltpu.VMEM((SUBLANES, gp), F32), pltpu.VMEM((SUBLANES, gp), F32),
                        pltpu.VMEM((chunk, w), F32)],
        compiler_params=_cparams(("parallel", "arbitrary")),
        name="s5_scan",
    )(u, *consts, *[prm[n] for n in names], h0_re, h0_im)


def _pick_tile(n, target):
    t = min(n, target)
    while n % t:
        t //= 2
    return t


def _row_tile(m):
    return min(range(LANES, MAX_ROW_TILE + 1, LANES), key=lambda t: (-(-m // t) * t, -t))


def _fill_rest(full, sample_rows, mp):
    pad = full.shape[0] - mp - sample_rows.shape[0]
    rest = jnp.concatenate([sample_rows.astype(full.dtype), jnp.zeros((pad, full.shape[1]), full.dtype)], axis=0)
    return lax.dynamic_update_slice(full, rest, (mp, 0))


def _run_trunk(x_prompt, x_sample, prm, s5_prompt, s5_sample, kv, s5_re0, s5_im0, conv0_prompt, conv0_sample):
    bp, tp, d = x_prompt.shape
    bs, ts, _ = x_sample.shape
    mp, ms = bp * tp, bs * ts
    tm = _row_tile(mp + ms)
    m_all = -(-(mp + ms) // tm) * tm
    depth = prm["ffn1_wgu"].shape[0]
    alpha = (2.0 * depth) ** 0.25
    nh, hd = kv["nh"], kv["hd"]
    moba_w = nh * hd
    g_cnt, p_cnt = prm["s5_a_re"].shape[1:]
    gp = g_cnt * p_cnt
    tf = _pick_tile(prm["ffn1_wd"].shape[1], 512)
    sample = slice(mp, mp + ms)

    x32 = jnp.concatenate([x_prompt.reshape(mp, d).astype(F32), x_sample.reshape(ms, d).astype(F32),
                           jnp.zeros((m_all - mp - ms, d), F32)], axis=0)
    xb = x32.astype(BF16)
    row = lambda a: a.reshape(1, -1).astype(F32)
    out = dict(hrp=[], hip=[], cvp=[], ks=[], vs=[], hrs=[], his=[], cvs=[])
    k_stack = v_stack = None
    wgu_b, wd_b = prm["ffn1_wgu"][0].astype(BF16)[None], prm["ffn1_wd"][0].astype(BF16)[None]
    for layer in range(depth):
        i = layer // 2
        g, b = prm["ln_g"][layer], prm["ln_b"][layer]
        x32, xb, wgu_b, wd_b = _ffn(x32, xb, wgu_b, wd_b, 0, row(g[0]), row(b[0]), alpha, tm, tf,
                                    convert_next=(prm["ffn2_wgu"], prm["ffn2_wd"], layer))
        if layer % 2 == 0:
            assert d - moba_w == moba_w
            (q, q_r), (k_stack, k_rest), (v_stack, v_rest), (u, u_r) = _hyb_in(xb, prm["hyb_w_in"], i, mp, k_stack,
                                                                               v_stack)
            attn = _moba_prompt(q, k_stack, v_stack, i, bp, tp, nh, hd, m_all)
            attn_s = _sample_attn(q_r[:ms], k_rest[:ms], v_rest[:ms], kv["cache_k"], kv["cache_v"], i,
                                  kv["page_table"], bs, ts, nh, hd)
            zero = jnp.zeros((bp, 1, gp), F32)
            y, hr_p, hi_p = _s5(u, zero, zero, s5_prompt[i], s5_prompt[i]["seg_len"], True, m_out=m_all)
            y_s, hr_s, hi_s = _s5(u_r[:ms], s5_re0[i].reshape(bs, gp), s5_im0[i].reshape(bs, gp), s5_sample[i],
                                  s5_sample[i]["seg_len"], False)
            x32, xb = _mm_res_ln([_fill_rest(attn, attn_s, mp), _fill_rest(y, y_s, mp)], prm["hyb_w_out"], i, x32,
                                 row(g[1]), row(b[1]), alpha, tm, glu=(prm["s5_glu_w"], row(prm["s5_glu_b"][i])))
            out["ks"].append(k_rest[:ms].reshape(bs, ts, nh, hd))
            out["vs"].append(v_rest[:ms].reshape(bs, ts, nh, hd))
            out["hrp"].append(hr_p.reshape(bp, g_cnt, p_cnt))
            out["hip"].append(hi_p.reshape(bp, g_cnt, p_cnt))
            out["hrs"].append(hr_s.reshape(bs, g_cnt, p_cnt))
            out["his"].append(hi_s.reshape(bs, g_cnt, p_cnt))
        else:
            c, cv_p = _conv_in_core(xb, prm["conv_w_in"], conv0_prompt[i], prm["conv_w"], i, bp, tp,
                                    _pick_tile(tp, 512), _pick_tile(d, 1024))
            z = _mm(xb[sample], prm["conv_w_in"], i, 0, 3 * d, ms, _pick_tile(d, 2048))
            c_s, cv_s = _conv_core(z, conv0_sample[i], prm["conv_w"], i, bs, ts, ts, _pick_tile(d, 512))
            x32, xb = _mm_res_ln([_fill_rest(c, c_s, mp)], prm["conv_w_out"], i, x32, row(g[1]), row(b[1]),
                                 alpha, tm)
            out["cvp"].append(cv_p)
            out["cvs"].append(cv_s)
        if layer < depth - 1:
            x32, xb, wgu_b, wd_b = _ffn(x32, xb, wgu_b, wd_b, 0, row(g[2]), row(b[2]), alpha, tm, tf,
                                        convert_next=(prm["ffn1_wgu"], prm["ffn1_wd"], layer + 1))
        else:
            y_prompt, y_rest = _ffn(x32, xb, wgu_b, wd_b, 0, row(g[2]), row(b[2]), alpha, tm, tf, mp=mp)
    out = {n: jnp.stack(a) for n, a in out.items()}
    n_hyb = k_stack.shape[0]
    out["kp"] = k_stack.reshape(n_hyb, bp, tp, nh, hd)
    out["vp"] = v_stack.reshape(n_hyb, bp, tp, nh, hd)
    return y_prompt.reshape(bp, tp, d), y_rest[:ms].reshape(bs, ts, d), out


def kernel(x_prompt, x_sample, cache_k, cache_v, state_s5_re, state_s5_im, state_conv, page_table, ffn1_wgu, ffn1_wd, ffn2_wgu, ffn2_wd, ln_g, ln_b, hyb_w_in, hyb_w_out, s5_a_re, s5_a_im, s5_log_dt, s5_b_re, s5_b_im, s5_c_re, s5_c_im, s5_d, s5_glu_w, s5_glu_b, conv_w_in, conv_w, conv_w_out):
    n_hyb, _, _, nh, hd = cache_k.shape
    bp, t_prompt, d = x_prompt.shape
    db, t_sample, _ = x_sample.shape
    n_conv, cw = conv_w.shape[:2]
    g_cnt, p_cnt = s5_a_re.shape[1:]
    assert db == SUBLANES, "the sample scan lays the decode batch along the vreg sublanes"
    assert (s5_b_re.shape[-1] * S5_LANE_GROUPS) == 128

    prm = dict(
        ffn1_wgu=ffn1_wgu, ffn1_wd=ffn1_wd, ffn2_wgu=ffn2_wgu, ffn2_wd=ffn2_wd,
        ln_g=ln_g, ln_b=ln_b, hyb_w_in=hyb_w_in.astype(BF16), hyb_w_out=hyb_w_out.astype(BF16),
        s5_a_re=s5_a_re, s5_glu_w=s5_glu_w.astype(BF16), s5_glu_b=s5_glu_b,
        conv_w_in=conv_w_in.astype(BF16), conv_w=conv_w.astype(F32), conv_w_out=conv_w_out.astype(BF16),
    )

    def s5_prms(seg_len):
        out = []
        for i in range(n_hyb):
            sp = _s5_discretise(s5_a_re[i], s5_a_im[i], s5_log_dt[i], s5_b_re[i], s5_b_im[i], s5_c_re[i],
                                s5_c_im[i], s5_d[i], seg_len)
            sp["seg_len"] = seg_len
            out.append(sp)
        return out

    prompt_seg = _pick_tile(t_prompt // SUBLANES, 32)
    kv = dict(nh=nh, hd=hd, cache_k=cache_k, cache_v=cache_v, page_table=page_table)
    conv_zero = jnp.zeros((n_conv, bp, cw - 1, d), x_prompt.dtype)
    y_p, y_s, o = _run_trunk(x_prompt, x_sample, prm, s5_prms(prompt_seg), s5_prms(t_sample), kv, state_s5_re,
                             state_s5_im, conv_zero, state_conv)
    return (y_p, y_s, o["kp"], o["vp"], o["hrp"], o["hip"], o["cvp"], o["ks"], o["vs"], o["hrs"], o["his"], o["cvs"])
```

```python
import functools

import jax
import jax.numpy as jnp
from jax import lax
from jax.experimental import pallas as pl
from jax.experimental.pallas import tpu as pltpu

MOBA_BLOCK = 256
MOBA_TOPK = 3
LN_EPS = 1e-5
LANES = 128
SUBLANES = 8
S5_LANE_GROUPS = 8
S5_SCAN_LANES = 1024
S5_SCAN_UNROLL = 4
CAST_BLOCK_BYTES = 4 * 1024 * 1024
BF16_ROWS = 16
MAX_ROW_TILE = 640
VMEM_LIMIT_BYTES = 56 * 1024 * 1024

F32 = jnp.float32
BF16 = jnp.bfloat16
NEG_INF = float("-inf")


def _cparams(semantics):
    return pltpu.CompilerParams(dimension_semantics=semantics, vmem_limit_bytes=VMEM_LIMIT_BYTES)


def _layer_norm(y, g, b):
    mu = jnp.mean(y, axis=-1, keepdims=True)
    d = y - mu
    var = jnp.mean(d * d, axis=-1, keepdims=True)
    return d * lax.rsqrt(var + LN_EPS) * g + b


def _convert_slab(s, n_steps, src_ref, layer, dst_ref, inbuf, outbuf, sem_in, sem_out, rows):
    last_slab = src_ref.shape[1] // rows - 1

    def row0(k):
        return pl.multiple_of(jnp.minimum(k, last_slab) * rows, rows)

    def fetch(k, slot):
        return pltpu.make_async_copy(src_ref.at[layer, pl.ds(row0(k), rows), :], inbuf.at[slot], sem_in.at[slot])

    def send(k):
        return pltpu.make_async_copy(outbuf, dst_ref.at[0, pl.ds(row0(k), rows), :], sem_out)

    slot = lax.rem(s, 2)

    @pl.when(s == 0)
    def _():
        fetch(0, 0).start()
        outbuf[...] = jnp.zeros_like(outbuf)
        send(0).start()

    fetch(s, slot).wait()
    fetch(s + 1, 1 - slot).start()
    send(s).wait()
    outbuf[...] = inbuf[slot].astype(BF16)
    send(s).start()

    @pl.when(s == n_steps - 1)
    def _():
        fetch(s + 1, 1 - slot).wait()
        send(s).wait()


def _cast_kernel(w_ref, o_ref):
    o_ref[...] = w_ref[...].astype(o_ref.dtype)


def _cast_layer(w, layer):
    _, r, c = w.shape
    rows = max(BF16_ROWS, CAST_BLOCK_BYTES // (4 * c))
    br = _pick_tile(r, 1 << (rows.bit_length() - 1))
    return pl.pallas_call(
        _cast_kernel,
        grid=(r // br,),
        in_specs=[pl.BlockSpec((None, br, c), lambda i: (layer, i, 0))],
        out_specs=pl.BlockSpec((None, br, c), lambda i: (0, i, 0)),
        out_shape=jax.ShapeDtypeStruct((1, r, c), BF16),
        compiler_params=_cparams(("parallel",)),
        name="cast_bf16",
    )(w)


def _convert_rows(n_rows, n_steps):
    return next(r for r in range(BF16_ROWS, n_rows + 1, BF16_ROWS) if n_rows % r == 0 and n_rows // r <= n_steps)


def _ffn_kernel(*refs, alpha, nf, rest, converting):
    x32_ref, xb_ref, wg_ref, wu_ref, wd_ref, g_ref, b_ref = refs[:7]
    if converting:
        (ngu_src, nd_src, o32_ref, aux_ref, ngu_dst, nd_dst, gu_in, gu_out, d_in, d_out, sem_in, sem_out) = refs[7:]
    else:
        o32_ref, aux_ref = refs[7:]
    i, j = pl.program_id(0), pl.program_id(1)
    if converting:
        step, n_steps = i * nf + j, pl.num_programs(0) * nf
        _convert_slab(step, n_steps, ngu_src, converting[0], ngu_dst, gu_in, gu_out, sem_in.at[0], sem_out.at[0],
                      gu_in.shape[1])
        _convert_slab(step, n_steps, nd_src, converting[0], nd_dst, d_in, d_out, sem_in.at[1], sem_out.at[1],
                      d_in.shape[1])

    @pl.when(j == 0)
    def _():
        o32_ref[...] = jnp.zeros_like(o32_ref)

    xb = xb_ref[...]
    gate = jnp.dot(xb, wg_ref[...], preferred_element_type=F32)
    up = jnp.dot(xb, wu_ref[...], preferred_element_type=F32)
    h = (gate * jax.nn.sigmoid(gate)) * up
    o32_ref[...] += jnp.dot(h.astype(BF16), wd_ref[...], preferred_element_type=F32)

    @pl.when(j == nf - 1)
    def _():
        z = _layer_norm(alpha * x32_ref[...] + 0.5 * o32_ref[...], g_ref[...], b_ref[...])
        o32_ref[...] = z
        if rest is None:
            aux_ref[...] = z.astype(BF16)
        else:
            @pl.when(i == pl.num_programs(0) - 1)
            def _():
                aux_ref[...] = z[rest[0]:rest[0] + rest[1], :]


def _ffn(x32, xb, wgu, wd, layer, g, b, alpha, tm, tf, mp=None, convert_next=None):
    m, d = x32.shape
    f = wd.shape[1]
    nf = f // tf
    n_steps = (m // tm) * nf
    if mp is None:
        rest = None
        out_specs = [pl.BlockSpec((tm, d), lambda i, j: (i, 0)), pl.BlockSpec((tm, d), lambda i, j: (i, 0))]
        out_shape = [jax.ShapeDtypeStruct((m, d), F32), jax.ShapeDtypeStruct((m, d), BF16)]
    else:
        rest = _rest_rows(m, mp, tm)
        out_specs = [pl.BlockSpec((tm, d), lambda i, j: (i, 0)), pl.BlockSpec((rest[1], d), lambda i, j: (0, 0))]
        out_shape = [jax.ShapeDtypeStruct((mp, d), F32), jax.ShapeDtypeStruct((rest[1], d), F32)]
    in_specs = [
        pl.BlockSpec((tm, d), lambda i, j: (i, 0)),
        pl.BlockSpec((tm, d), lambda i, j: (i, 0)),
        pl.BlockSpec((None, d, tf), lambda i, j: (layer, 0, j)),
        pl.BlockSpec((None, d, tf), lambda i, j: (layer, 0, j + nf)),
        pl.BlockSpec((None, tf, d), lambda i, j: (layer, j, 0)),
        pl.BlockSpec((1, d), lambda i, j: (0, 0)),
        pl.BlockSpec((1, d), lambda i, j: (0, 0)),
    ]
    operands = [x32, xb, wgu, wgu, wd, g, b]
    scratch, converting = [], None
    if convert_next is not None:
        src_gu, src_d, next_layer = convert_next
        converting = (next_layer,)
        gu_rows, d_rows = _convert_rows(d, n_steps), _convert_rows(f, n_steps)
        any_spec = pl.BlockSpec(memory_space=pl.ANY)
        in_specs += [any_spec, any_spec]
        operands += [src_gu, src_d]
        out_specs += [any_spec, any_spec]
        out_shape += [jax.ShapeDtypeStruct((1, d, 2 * f), BF16), jax.ShapeDtypeStruct((1, f, d), BF16)]
        scratch = [pltpu.VMEM((2, gu_rows, 2 * f), F32), pltpu.VMEM((gu_rows, 2 * f), BF16),
                   pltpu.VMEM((2, d_rows, d), F32), pltpu.VMEM((d_rows, d), BF16),
                   pltpu.SemaphoreType.DMA((2, 2)), pltpu.SemaphoreType.DMA((2,))]
    return pl.pallas_call(
        functools.partial(_ffn_kernel, alpha=alpha, nf=nf, rest=rest, converting=converting),
        grid=(m // tm, nf),
        in_specs=in_specs,
        out_specs=out_specs,
        out_shape=out_shape,
        scratch_shapes=scratch,
        compiler_params=_cparams(("arbitrary", "arbitrary") if converting else ("parallel", "arbitrary")),
        name="ffn_ln",
    )(*operands)


def _mm_kernel(x_ref, w_ref, o_ref):
    o_ref[...] = jnp.dot(x_ref[...], w_ref[...], preferred_element_type=F32)


def _mm(xb, w, layer, col0, n, tm, tn):
    m, k = xb.shape
    assert col0 % tn == 0 and n % tn == 0
    jb0 = col0 // tn
    return pl.pallas_call(
        _mm_kernel,
        grid=(m // tm, n // tn),
        in_specs=[
            pl.BlockSpec((tm, k), lambda i, j: (i, 0)),
            pl.BlockSpec((None, k, tn), lambda i, j: (layer, 0, jb0 + j)),
        ],
        out_specs=pl.BlockSpec((tm, tn), lambda i, j: (i, j)),
        out_shape=jax.ShapeDtypeStruct((m, n), F32),
        compiler_params=_cparams(("parallel", "arbitrary")),
        name="proj",
    )(xb, w)


def _rest_rows(m_all, mp, tm):
    first = mp - (m_all // tm - 1) * tm
    assert m_all % tm == 0 and 0 <= first and first + (m_all - mp) == tm
    return first, m_all - mp


def _hyb_in_kernel(x_ref, xr_ref, w_ref, *refs, n_alias):
    qp_ref, qr_ref, kp_ref, kr_ref, vp_ref, vr_ref, up_ref, ur_ref = refs[n_alias:]
    j, i = pl.program_id(0), pl.program_id(1)
    last = i == pl.num_programs(1) - 1
    w = w_ref[...]
    width = w.shape[1] // 2
    y = jnp.dot(x_ref[...], w, preferred_element_type=F32)

    def emit(lo_ref, hi_ref, lo_rest_ref, hi_rest_ref):
        lo_ref[...] = y[:, :width]
        hi_ref[...] = y[:, width:]

        @pl.when(last)
        def _():
            yr = jnp.dot(xr_ref[...], w, preferred_element_type=F32)
            lo_rest_ref[...] = yr[:, :width]
            hi_rest_ref[...] = yr[:, width:]

    @pl.when(j == 0)
    def _():
        emit(qp_ref, kp_ref, qr_ref, kr_ref)

    @pl.when(j == 1)
    def _():
        emit(vp_ref, up_ref, vr_ref, ur_ref)


def _hyb_in(xb, w, layer, mp, k_stack=None, v_stack=None):
    m, k = xb.shape
    n_layers = w.shape[0]
    width = w.shape[2] // 4
    n_rest = m - mp
    assert mp % n_rest == 0
    tm = _pick_tile(mp, 512)
    aliased = [] if k_stack is None else [k_stack, v_stack]
    n_tiles = mp // tm
    tile = (lambda j, i: i * (1 - j) + (n_tiles - 1) * j, lambda j, i: i * j)
    prompt_spec = [pl.BlockSpec((tm, width), lambda j, i, p=p: (tile[p](j, i), 0)) for p in (0, 1)]
    stack_spec = [pl.BlockSpec((None, tm, width), lambda j, i, p=p: (layer, tile[p](j, i), 0)) for p in (0, 1)]
    rest_spec = pl.BlockSpec((n_rest, width), lambda j, i: (0, 0))
    prompt_shape = jax.ShapeDtypeStruct((mp, width), F32)
    stack_shape = jax.ShapeDtypeStruct((n_layers, mp, width), F32)
    rest_shape = jax.ShapeDtypeStruct((n_rest, width), F32)
    qp, qr, kp, kr, vp, vr, up, ur = pl.pallas_call(
        functools.partial(_hyb_in_kernel, n_alias=len(aliased)),
        grid=(2, n_tiles),
        in_specs=[
            pl.BlockSpec((tm, k), lambda j, i: (i, 0)),
            pl.BlockSpec((n_rest, k), lambda j, i: (mp // n_rest, 0)),
            pl.BlockSpec((None, k, 2 * width), lambda j, i: (layer, 0, j)),
        ] + [pl.BlockSpec(memory_space=pl.ANY)] * len(aliased),
        out_specs=[prompt_spec[0], rest_spec, stack_spec[0], rest_spec, stack_spec[1], rest_spec, prompt_spec[1],
                   rest_spec],
        out_shape=[prompt_shape, rest_shape, stack_shape, rest_shape, stack_shape, rest_shape, prompt_shape,
                   rest_shape],
        input_output_aliases={3: 2, 4: 4} if aliased else {},
        compiler_params=_cparams(("arbitrary", "arbitrary")),
        name="hyb_in_proj",
    )(xb, xb, w, *aliased)
    return (qp, qr), (kp, kr), (vp, vr), (up, ur)


def _mm_res_ln_kernel(*refs, nparts, gated, alpha):
    a_refs = refs[:nparts]
    w_refs = refs[nparts:2 * nparts]
    rest = refs[2 * nparts:]
    if gated:
        gw_ref, gb_ref = rest[:2]
        rest = rest[2:]
    x32_ref, g_ref, b_ref, o32_ref, ob_ref = rest
    y = alpha * x32_ref[...]
    for r, (a_ref, w_ref) in enumerate(zip(a_refs, w_refs)):
        a = a_ref[...]
        if gated and r == nparts - 1:
            gate = jnp.dot(a.astype(BF16), gw_ref[...], preferred_element_type=F32) + gb_ref[...]
            a = a * jax.nn.sigmoid(gate)
        y = y + jnp.dot(a.astype(BF16), w_ref[...], preferred_element_type=F32)
    z = _layer_norm(y, g_ref[...], b_ref[...])
    o32_ref[...] = z
    ob_ref[...] = z.astype(BF16)


def _mm_res_ln(parts, w, layer, x32, g, b, alpha, tm, glu=None):
    m, d = x32.shape
    kp = parts[0].shape[1]
    assert all(p.shape[1] == kp for p in parts)
    once = dict(pipeline_mode=pl.Buffered(1))
    in_specs = [pl.BlockSpec((tm, kp), lambda i: (i, 0)) for _ in parts]
    in_specs += [pl.BlockSpec((None, kp, d), lambda i, r=r: (layer, r, 0), **once) for r in range(len(parts))]
    operands = [*parts, *([w] * len(parts))]
    if glu is not None:
        in_specs += [pl.BlockSpec((None, kp, kp), lambda i: (layer, 0, 0), **once), pl.BlockSpec((1, kp), lambda i: (0, 0))]
        operands += list(glu)
    in_specs += [
        pl.BlockSpec((tm, d), lambda i: (i, 0)),
        pl.BlockSpec((1, d), lambda i: (0, 0)),
        pl.BlockSpec((1, d), lambda i: (0, 0)),
    ]
    return pl.pallas_call(
        functools.partial(_mm_res_ln_kernel, nparts=len(parts), gated=glu is not None, alpha=alpha),
        grid=(m // tm,),
        in_specs=in_specs,
        out_specs=[pl.BlockSpec((tm, d), lambda i: (i, 0)), pl.BlockSpec((tm, d), lambda i: (i, 0))],
        out_shape=[jax.ShapeDtypeStruct((m, d), F32), jax.ShapeDtypeStruct((m, d), BF16)],
        compiler_params=_cparams(("parallel",)),
        name="out_proj_ln",
    )(*operands, x32, g, b)


CONV_PAD = 8
CONV_COL_SLAB = 256


def _conv_tile(project, st_ref, w_ref, c_ref, ns_ref, hbuf, *, tt, nt, cw):
    t = pl.program_id(2)
    lo = CONV_PAD - (cw - 1)
    tn = c_ref.shape[1]
    slab = min(tn, CONV_COL_SLAB)

    @pl.when(t == 0)
    def _():
        hbuf[lo:CONV_PAD, :] = st_ref[...]

    for c0 in range(0, tn, slab):
        cols = slice(c0, c0 + slab)
        bg, cg, v = project(cols)
        hbuf[CONV_PAD:CONV_PAD + tt, cols] = cg * v
        w = w_ref[:, cols]
        conv = w[0:1, :] * hbuf[lo:lo + tt, cols]
        for j in range(1, cw):
            conv = conv + w[j:j + 1, :] * hbuf[lo + j:lo + j + tt, cols]
        c_ref[:, cols] = (bg * conv).astype(c_ref.dtype)
    tail = hbuf[lo + tt:CONV_PAD + tt, :]

    @pl.when(t == nt - 1)
    def _():
        ns_ref[...] = tail

    hbuf[lo:CONV_PAD, :] = tail


def _conv_kernel(bg_ref, cg_ref, v_ref, st_ref, w_ref, c_ref, ns_ref, hbuf, **kw):
    _conv_tile(lambda cols: (bg_ref[:, cols], cg_ref[:, cols], v_ref[:, cols]), st_ref, w_ref, c_ref, ns_ref, hbuf,
               **kw)


def _conv_in_kernel(x_ref, wb_ref, wc_ref, wv_ref, st_ref, w_ref, c_ref, ns_ref, hbuf, **kw):
    def project(cols):
        x = x_ref[...]
        return tuple(jnp.dot(x, w[:, cols], preferred_element_type=F32) for w in (wb_ref, wc_ref, wv_ref))

    _conv_tile(project, st_ref, w_ref, c_ref, ns_ref, hbuf, **kw)


def _conv_core(z, state, conv_w, layer, bsz, t_len, tt, tn):
    m, d3 = z.shape
    d = d3 // 3
    cw = conv_w.shape[1]
    nt, nj = t_len // tt, d // tn
    return pl.pallas_call(
        functools.partial(_conv_kernel, tt=tt, nt=nt, cw=cw),
        grid=(bsz, nj, nt),
        in_specs=[
            pl.BlockSpec((tt, tn), lambda b, j, t: (b * nt + t, j)),
            pl.BlockSpec((tt, tn), lambda b, j, t: (b * nt + t, j + nj)),
            pl.BlockSpec((tt, tn), lambda b, j, t: (b * nt + t, j + 2 * nj)),
            pl.BlockSpec((None, cw - 1, tn), lambda b, j, t: (b, 0, j)),
            pl.BlockSpec((None, cw, tn), lambda b, j, t: (layer, 0, j)),
        ],
        out_specs=[
            pl.BlockSpec((tt, tn), lambda b, j, t: (b * nt + t, j)),
            pl.BlockSpec((None, cw - 1, tn), lambda b, j, t: (b, 0, j)),
        ],
        out_shape=[jax.ShapeDtypeStruct((m, d), BF16), jax.ShapeDtypeStruct((bsz, cw - 1, d), F32)],
        scratch_shapes=[pltpu.VMEM((CONV_PAD + tt, tn), F32)],
        compiler_params=_cparams(("parallel", "parallel", "arbitrary")),
        name="conv_core",
    )(z, z, z, state, conv_w)


def _conv_in_core(xb, w_in, state, conv_w, layer, bsz, t_len, tt, tn):
    m, d = xb.shape
    cw = conv_w.shape[1]
    nt, nj = t_len // tt, d // tn
    return pl.pallas_call(
        functools.partial(_conv_in_kernel, tt=tt, nt=nt, cw=cw),
        grid=(bsz, nj, nt),
        in_specs=[
            pl.BlockSpec((tt, d), lambda b, j, t: (b * nt + t, 0)),
            pl.BlockSpec((None, d, tn), lambda b, j, t: (layer, 0, j)),
            pl.BlockSpec((None, d, tn), lambda b, j, t: (layer, 0, j + nj)),
            pl.BlockSpec((None, d, tn), lambda b, j, t: (layer, 0, j + 2 * nj)),
            pl.BlockSpec((None, cw - 1, tn), lambda b, j, t: (b, 0, j)),
            pl.BlockSpec((None, cw, tn), lambda b, j, t: (layer, 0, j)),
        ],
        out_specs=[
            pl.BlockSpec((tt, tn), lambda b, j, t: (b * nt + t, j)),
            pl.BlockSpec((None, cw - 1, tn), lambda b, j, t: (b, 0, j)),
        ],
        out_shape=[jax.ShapeDtypeStruct((m, d), BF16), jax.ShapeDtypeStruct((bsz, cw - 1, d), F32)],
        scratch_shapes=[pltpu.VMEM((CONV_PAD + tt, tn), F32)],
        compiler_params=_cparams(("parallel", "parallel", "arbitrary")),
        name="conv_in_core",
    )(xb, w_in, w_in, w_in, state, conv_w)


def _moba_select_rows(gate_t, n_valid):
    nb, nq = gate_t.shape
    sub = lax.broadcasted_iota(jnp.int32, (nb, nq), 0)
    valid = sub < n_valid
    gate_t = jnp.where(valid, gate_t, NEG_INF)
    beaten_by = jnp.zeros((nb, nq), jnp.int32)
    for jp in range(nb):
        row = gate_t[jp:jp + 1, :]
        beats = (row > gate_t) | ((row == gate_t) & (sub > jp))
        beaten_by = beaten_by + jnp.where(beats, 1, 0)
    return valid & (beaten_by < MOBA_TOPK)


def _moba_prompt_kernel(q_ref, k_ref, v_ref, o_ref, kb_scr, vt_scr, km_scr, *, nb, scale):
    blk = MOBA_BLOCK
    kb_scr[...] = k_ref[...].astype(BF16)
    for j in range(nb):
        rows = slice(j * blk, (j + 1) * blk)
        km_scr[j:j + 1, :] = jnp.mean(k_ref[rows, :], axis=0, keepdims=True)
        vt_scr[:, rows] = v_ref[rows, :].T.astype(BF16)
    causal = lax.broadcasted_iota(jnp.int32, (blk, blk), 0) <= lax.broadcasted_iota(jnp.int32, (blk, blk), 1)
    causal_bias = jnp.where(causal, 0.0, NEG_INF).astype(F32)

    for qi in range(nb):
        n_keys = (qi + 1) * blk
        q = q_ref[qi * blk:(qi + 1) * blk, :]
        s_all = lax.dot_general(kb_scr[0:n_keys, :], q.astype(BF16), (((1,), (1,)), ((), ())),
                                preferred_element_type=F32) * scale
        if qi > 0:
            gate_t = lax.dot_general(km_scr[...], q, (((1,), (1,)), ((), ())),
                                     precision=lax.Precision.HIGHEST, preferred_element_type=F32)
            sel_bias = jnp.where(_moba_select_rows(gate_t, qi), 0.0, NEG_INF).astype(F32)
        s_blocks = [s_all[j * blk:(j + 1) * blk, :] + sel_bias[j:j + 1, :] for j in range(qi)]
        s_blocks.append(s_all[qi * blk:n_keys, :] + causal_bias)
        top = s_blocks[0]
        for sb in s_blocks[1:]:
            top = jnp.maximum(top, sb)
        m = jnp.max(top, axis=0, keepdims=True)
        p_blocks = [jnp.exp(sb - m) for sb in s_blocks]
        tot = p_blocks[0]
        for pb in p_blocks[1:]:
            tot = tot + pb
        l = jnp.sum(tot, axis=0, keepdims=True)
        p_all = jnp.concatenate([pb.astype(BF16) for pb in p_blocks], axis=0)
        acc = jnp.dot(vt_scr[:, 0:n_keys], p_all, preferred_element_type=F32)
        o_ref[qi * blk:(qi + 1) * blk, :] = (acc / l).T.astype(o_ref.dtype)


def _moba_prompt(q, k_stack, v_stack, layer, bsz, t_len, nh, hd, m_out):
    m = m_out
    assert t_len % MOBA_BLOCK == 0
    nb = t_len // MOBA_BLOCK
    seq_spec = pl.BlockSpec((t_len, hd), lambda b, h: (b, h))
    kv_spec = pl.BlockSpec((None, t_len, hd), lambda b, h: (layer, b, h))
    return pl.pallas_call(
        functools.partial(_moba_prompt_kernel, nb=nb, scale=hd ** -0.5),
        grid=(bsz, nh),
        in_specs=[seq_spec, kv_spec, kv_spec],
        out_specs=seq_spec,
        out_shape=jax.ShapeDtypeStruct((m, nh * hd), BF16),
        scratch_shapes=[pltpu.VMEM((t_len, hd), BF16), pltpu.VMEM((hd, t_len), BF16),
                        pltpu.VMEM((nb, hd), F32)],
        compiler_params=_cparams(("parallel", "parallel")),
        name="moba_prompt",
    )(q, k_stack, v_stack)


def _moba_select_grouped(gate, n_valid_blocks, group):
    rows, nl = gate.shape
    lane = lax.broadcasted_iota(jnp.int32, (rows, nl), 1)
    valid = lane < n_valid_blocks * group
    gate = jnp.where(valid, gate, NEG_INF)
    beaten_by = jnp.zeros((rows, nl), jnp.int32)
    for jp in range(0, nl, group):
        col = gate[:, jp:jp + 1]
        beats = (col > gate) | ((col == gate) & (lane >= jp + group))
        beaten_by = beaten_by + jnp.where(beats, 1, 0)
    return valid & (beaten_by < MOBA_TOPK)


def _fold_lanes(x, op):
    tiles = [x[:, i:i + LANES] for i in range(0, x.shape[1], LANES)]
    out = tiles[0]
    for t in tiles[1:]:
        out = op(out, t)
    return out


def _sample_attn_kernel(pt_ref, q_ref, kn_ref, vn_ref, hbias_ref, obias_ref, *refs, nh, n_past_blocks,
                        pages_per_step, pages_per_block, scale):
    del pt_ref
    kp_refs = refs[:pages_per_step]
    vp_refs = refs[pages_per_step:2 * pages_per_step]
    o_ref, gate_scr, m_scr, l_scr, acc_scr = refs[2 * pages_per_step:]
    step = pl.program_id(1)
    n_steps = pl.num_programs(1)
    rows, hd = q_ref.shape
    n_pages = m_scr.shape[1]
    q = q_ref[...]
    qb = q.astype(BF16)
    q3 = q.reshape(rows // nh, nh, hd)
    lane = lax.broadcasted_iota(jnp.int32, (rows, n_pages), 1)

    def put_column(scr, pg, col):
        scr[...] = jnp.where(lane == pg, col, scr[...])

    for blk0 in range(0, pages_per_step, pages_per_block):
        ksum = jnp.zeros((nh, hd), F32)
        for r in range(blk0, blk0 + pages_per_block):
            pg = step * pages_per_step + r
            k = kp_refs[r][...]
            ksum = ksum + jnp.sum(k.reshape(k.shape[0] // nh, nh, hd), axis=0)
            s = lax.dot_general(qb, k.astype(BF16), (((1,), (1,)), ((), ())), preferred_element_type=F32)
            s = s * scale + hbias_ref[...]
            m = jnp.max(_fold_lanes(s, jnp.maximum), axis=1, keepdims=True)
            p = jnp.exp(s - m)
            put_column(m_scr, pg, m)
            put_column(l_scr, pg, jnp.sum(_fold_lanes(p, jnp.add), axis=1, keepdims=True))
            acc_scr[pg] = jnp.dot(p.astype(BF16), vp_refs[r][...].astype(BF16), preferred_element_type=F32)
        kmean = ksum / MOBA_BLOCK
        gcol = jnp.sum(q3 * kmean[None, :, :], axis=2, keepdims=True).reshape(rows, 1)
        for r in range(blk0, blk0 + pages_per_block):
            put_column(gate_scr, step * pages_per_step + r, gcol)

    @pl.when(step == n_steps - 1)
    def _():
        keep = _moba_select_grouped(gate_scr[...], n_past_blocks, pages_per_block)
        m_pages = m_scr[...] + jnp.where(keep, 0.0, NEG_INF)
        s = lax.dot_general(qb, kn_ref[...].astype(BF16), (((1,), (1,)), ((), ())), preferred_element_type=F32)
        s = s * scale + obias_ref[...]
        m_own = jnp.max(s, axis=1, keepdims=True)
        m_tot = jnp.maximum(m_own, jnp.max(m_pages, axis=1, keepdims=True))
        p = jnp.exp(s - m_tot)
        w = jnp.exp(m_pages - m_tot)
        l = jnp.sum(p, axis=1, keepdims=True) + jnp.sum(w * l_scr[...], axis=1, keepdims=True)
        acc = jnp.dot(p.astype(BF16), vn_ref[...].astype(BF16), preferred_element_type=F32)
        for pg in range(n_pages):
            acc = acc + w[:, pg:pg + 1] * acc_scr[pg]
        o_ref[...] = (acc / l).astype(o_ref.dtype)


def _sample_attn(q, k_new, v_new, cache_k, cache_v, layer, page_table, db, t_len, nh, hd, pages_per_step=16):
    n_hyb, n_pool, page, _, _ = cache_k.shape
    n_pages = page_table.shape[1]
    past_len = n_pages * page
    assert past_len % MOBA_BLOCK == 0 and t_len <= MOBA_BLOCK and MOBA_BLOCK % page == 0
    assert nh & (nh - 1) == 0, "head matching uses bit masks"
    ppb = MOBA_BLOCK // page
    pages_per_step = min(pages_per_step, n_pages)
    assert pages_per_step % ppb == 0 and n_pages % pages_per_step == 0
    rows = t_len * nh
    ck = cache_k.reshape(n_hyb, n_pool, page * nh, hd)
    cv = cache_v.reshape(n_hyb, n_pool, page * nh, hd)
    r = jnp.arange(rows, dtype=jnp.int32)[:, None]
    c = jnp.arange(page * nh, dtype=jnp.int32)[None, :]
    same_head = (r & (nh - 1)) == (c & (nh - 1))
    head_bias = jnp.where(same_head, 0.0, NEG_INF).astype(F32)
    own_bias = jnp.where(same_head[:, :rows] & (c[:, :rows] // nh <= r // nh), 0.0, NEG_INF).astype(F32)

    def page_spec(i):
        return pl.BlockSpec((None, None, page * nh, hd),
                            lambda b, s, pt: (layer, pt[b, s * pages_per_step + i], 0, 0))

    def row_spec(nrows):
        return pl.BlockSpec((None, nrows, hd), lambda b, s, pt: (b, 0, 0))

    full = lambda a: pl.BlockSpec(a.shape, lambda b, s, pt: (0,) * a.ndim)
    out = pl.pallas_call(
        functools.partial(_sample_attn_kernel, nh=nh, n_past_blocks=past_len // MOBA_BLOCK,
                          pages_per_step=pages_per_step, pages_per_block=ppb, scale=hd ** -0.5),
        grid_spec=pltpu.PrefetchScalarGridSpec(
            num_scalar_prefetch=1,
            grid=(db, n_pages // pages_per_step),
            in_specs=[row_spec(rows), row_spec(rows), row_spec(rows), full(head_bias), full(own_bias)]
            + [page_spec(i) for i in range(pages_per_step)] * 2,
            out_specs=row_spec(rows),
            scratch_shapes=[pltpu.VMEM((rows, n_pages), F32), pltpu.VMEM((rows, n_pages), F32),
                            pltpu.VMEM((rows, n_pages), F32), pltpu.VMEM((n_pages, rows, hd), F32)],
        ),
        out_shape=jax.ShapeDtypeStruct((db, rows, hd), BF16),
        compiler_params=_cparams(("parallel", "arbitrary")),
        name="sample_attn",
    )(page_table, q.reshape(db, rows, hd), k_new.reshape(db, rows, hd), v_new.reshape(db, rows, hd),
      head_bias, own_bias, *([ck] * pages_per_step), *([cv] * pages_per_step))
    return out.reshape(db * t_len, nh * hd)


def _s5_kernel(u_ref, perm_ref, unperm_ref, bre_ref, bim_ref, cre_ref, cimn_ref, lre_ref, lim_ref, llre_ref, llim_ref,
               d_ref, h0re_ref, h0im_ref, y_ref, htre_ref, htim_ref, hr_scr, hi_scr, cre_scr, cim_scr, yc_scr,
               *, seg_len, chained, n_chunks):
    c = pl.program_id(1)
    nseg = SUBLANES
    nstate = hr_scr.shape[1]
    n_kb = bre_ref.shape[0]
    ku = bre_ref.shape[1]
    ks = bre_ref.shape[2]

    up = jnp.dot(perm_ref[...], u_ref[...].astype(BF16), preferred_element_type=F32).astype(BF16)
    for kb in range(n_kb):
        ukb = up[:, kb * ku:(kb + 1) * ku]
        hr_scr[:, kb * ks:(kb + 1) * ks] = jnp.dot(ukb, bre_ref[kb], preferred_element_type=F32)
        hi_scr[:, kb * ks:(kb + 1) * ks] = jnp.dot(ukb, bim_ref[kb], preferred_element_type=F32)

    if chained:
        @pl.when(c == 0)
        def _():
            cre_scr[...] = jnp.zeros_like(cre_scr)
            cim_scr[...] = jnp.zeros_like(cim_scr)
            cre_scr[0:1, :] = h0re_ref[...]
            cim_scr[0:1, :] = h0im_ref[...]

    for lc in range(nstate // S5_SCAN_LANES):
        cols = slice(lc * S5_SCAN_LANES, (lc + 1) * S5_SCAN_LANES)
        lr = jnp.broadcast_to(lre_ref[:, cols], (nseg, S5_SCAN_LANES))
        li = jnp.broadcast_to(lim_ref[:, cols], (nseg, S5_SCAN_LANES))

        def advance(i, hr, hi):
            r = pl.multiple_of(i * nseg, nseg)
            nr = lr * hr - li * hi + hr_scr[pl.ds(r, nseg), cols]
            ni = lr * hi + li * hr + hi_scr[pl.ds(r, nseg), cols]
            return r, nr, ni

        def scan_only(i, carry):
            _, nr, ni = advance(i, *carry)
            return nr, ni

        def scan_store(i, carry):
            r, nr, ni = advance(i, *carry)
            hr_scr[pl.ds(r, nseg), cols] = nr
            hi_scr[pl.ds(r, nseg), cols] = ni
            return nr, ni

        if chained:
            er, ei = lax.fori_loop(0, seg_len, scan_only, (cre_scr[:, cols], cim_scr[:, cols]),
                                   unroll=S5_SCAN_UNROLL)
            llr, lli = llre_ref[:, cols], llim_ref[:, cols]
            fr, fi = er[0:1, :], ei[0:1, :]
            for j in range(1, nseg):
                cre_scr[j:j + 1, cols] = fr
                cim_scr[j:j + 1, cols] = fi
                if j < nseg - 1:
                    fr, fi = (er[j:j + 1, :] + (llr * fr - lli * fi), ei[j:j + 1, :] + (llr * fi + lli * fr))
            init = (cre_scr[:, cols], cim_scr[:, cols])
        else:
            init = (h0re_ref[:, cols], h0im_ref[:, cols])

        hr, hi = lax.fori_loop(0, seg_len, scan_store, init, unroll=S5_SCAN_UNROLL)

        if chained:
            last_r, last_i = hr[nseg - 1:nseg, :], hi[nseg - 1:nseg, :]
            cre_scr[:, cols] = jnp.zeros((nseg, S5_SCAN_LANES), F32)
            cim_scr[:, cols] = jnp.zeros((nseg, S5_SCAN_LANES), F32)
            cre_scr[0:1, cols] = last_r
            cim_scr[0:1, cols] = last_i

            @pl.when(c == n_chunks - 1)
            def _():
                htre_ref[:, cols] = last_r
                htim_ref[:, cols] = last_i
        else:
            htre_ref[:, cols] = hr
            htim_ref[:, cols] = hi

    for kb in range(n_kb):
        hre = hr_scr[:, kb * ks:(kb + 1) * ks].astype(BF16)
        him = hi_scr[:, kb * ks:(kb + 1) * ks].astype(BF16)
        ykb = (jnp.dot(hre, cre_ref[kb], preferred_element_type=F32)
               + jnp.dot(him, cimn_ref[kb], preferred_element_type=F32))
        yc_scr[:, kb * ku:(kb + 1) * ku] = ykb

    yc = yc_scr[...]
    hi = yc.astype(BF16)
    r1 = yc - hi.astype(F32)
    mid = r1.astype(BF16)
    lo = (r1 - mid.astype(F32)).astype(BF16)
    unperm = unperm_ref[...]
    y = (jnp.dot(unperm, hi, preferred_element_type=F32) + jnp.dot(unperm, mid, preferred_element_type=F32)
         + jnp.dot(unperm, lo, preferred_element_type=F32))
    y_ref[...] = jax.nn.gelu(y + d_ref[...] * u_ref[...])


def _s5_discretise(a_re, a_im, log_dt, b_re, b_im, c_re, c_im, d_skip, seg_len):
    g, p = a_re.shape
    ch = b_re.shape[-1]
    lam = lax.complex(a_re.astype(F32), a_im.astype(F32))
    dt = jnp.exp(log_dt.astype(F32))[:, None]
    lam_bar = jnp.exp(lam * dt)
    b_bar = ((lam_bar - 1.0) / lam)[..., None] * lax.complex(b_re.astype(F32), b_im.astype(F32))
    lam_seg = lam_bar
    assert seg_len & (seg_len - 1) == 0
    for _ in range(seg_len.bit_length() - 1):
        lam_seg = lam_seg * lam_seg
    gl = S5_LANE_GROUPS
    n_kb = g // gl
    eye = jnp.eye(gl, dtype=F32)

    def pack_b(x):
        x = x.reshape(n_kb, gl, p, ch).transpose(0, 1, 3, 2)
        return jnp.einsum("kgcp,gh->kgchp", x, eye).reshape(n_kb, gl * ch, gl * p)

    def pack_c(x):
        x = x.reshape(n_kb, gl, ch, p).transpose(0, 1, 3, 2)
        return jnp.einsum("kgpc,gh->kgphc", x, eye).reshape(n_kb, gl * p, gl * ch)

    flat = lambda x: x.reshape(1, g * p)
    return dict(
        bre=pack_b(jnp.real(b_bar)).astype(BF16), bim=pack_b(jnp.imag(b_bar)).astype(BF16),
        cre=pack_c(c_re.astype(F32)).astype(BF16), cimn=pack_c(-c_im.astype(F32)).astype(BF16),
        lre=flat(jnp.real(lam_bar)), lim=flat(jnp.imag(lam_bar)),
        llre=flat(jnp.real(lam_seg)), llim=flat(jnp.imag(lam_seg)),
        d=d_skip.astype(F32).reshape(1, g * ch),
    )


def _s5(u, h0_re, h0_im, prm, seg_len, chained, m_out=None):
    n_rows, w = u.shape
    m = n_rows if m_out is None else m_out
    gp = prm["lre"].shape[1]
    chunk = SUBLANES * seg_len
    if chained:
        nseq = h0_re.shape[0]
        n_chunks = n_rows // nseq // chunk
        grid = (nseq, n_chunks)
        u_spec = pl.BlockSpec((chunk, w), lambda b, c: (b * n_chunks + c, 0))
        h_spec = pl.BlockSpec((None, 1, gp), lambda b, c: (b, 0, 0))
        h_shape = jax.ShapeDtypeStruct((nseq, 1, gp), F32)
    else:
        assert n_rows == chunk
        n_chunks = 1
        grid = (1, 1)
        u_spec = pl.BlockSpec((chunk, w), lambda b, c: (0, 0))
        h_spec = pl.BlockSpec((SUBLANES, gp), lambda b, c: (0, 0))
        h_shape = jax.ShapeDtypeStruct((SUBLANES, gp), F32)
    full = lambda a: pl.BlockSpec(a.shape, lambda b, c: (0,) * a.ndim)
    names = ("bre", "bim", "cre", "cimn", "lre", "lim", "llre", "llim", "d")
    r = jnp.arange(chunk)
    perm = (((r % SUBLANES) * seg_len + r // SUBLANES)[:, None] == r[None, :]).astype(BF16)
    consts = [perm, perm.T]
    return pl.pallas_call(
        functools.partial(_s5_kernel, seg_len=seg_len, chained=chained, n_chunks=n_chunks),
        grid=grid,
        in_specs=[u_spec] + [full(a) for a in consts] + [full(prm[n]) for n in names] + [h_spec, h_spec],
        out_specs=[u_spec, h_spec, h_spec],
        out_shape=[jax.ShapeDtypeStruct((m, w), F32), h_shape, h_shape],
        scratch_shapes=[pltpu.VMEM((chunk, gp), F32), pltpu.VMEM((chunk, gp), F32),
                        pltpu.VMEM((SUBLANES, gp), F32), pltpu.VMEM((SUBLANES, gp), F32),
                        pltpu.VMEM((chunk, w), F32)],
        compiler_params=_cparams(("parallel", "arbitrary")),
        name="s5_scan",
    )(u, *consts, *[prm[n] for n in names], h0_re, h0_im)


def _pick_tile(n, target):
    t = min(n, target)
    while n % t:
        t //= 2
    return t


def _row_tile(m):
    return min(range(LANES, MAX_ROW_TILE + 1, LANES), key=lambda t: (-(-m // t) * t, -t))


def _fill_rest(full, sample_rows, mp):
    pad = full.shape[0] - mp - sample_rows.shape[0]
    rest = jnp.concatenate([sample_rows.astype(full.dtype), jnp.zeros((pad, full.shape[1]), full.dtype)], axis=0)
    return lax.dynamic_update_slice(full, rest, (mp, 0))


def _run_trunk(x_prompt, x_sample, prm, s5_prompt, s5_sample, kv, s5_re0, s5_im0, conv0_prompt, conv0_sample):
    bp, tp, d = x_prompt.shape
    bs, ts, _ = x_sample.shape
    mp, ms = bp * tp, bs * ts
    tm = _row_tile(mp + ms)
    m_all = -(-(mp + ms) // tm) * tm
    depth = prm["ffn1_wgu"].shape[0]
    alpha = (2.0 * depth) ** 0.25
    nh, hd = kv["nh"], kv["hd"]
    moba_w = nh * hd
    g_cnt, p_cnt = prm["s5_a_re"].shape[1:]
    gp = g_cnt * p_cnt
    tf = _pick_tile(prm["ffn1_wd"].shape[1], 512)
    sample = slice(mp, mp + ms)

    x32 = jnp.concatenate([x_prompt.reshape(mp, d).astype(F32), x_sample.reshape(ms, d).astype(F32),
                           jnp.zeros((m_all - mp - ms, d), F32)], axis=0)
    xb = x32.astype(BF16)
    row = lambda a: a.reshape(1, -1).astype(F32)
    out = dict(hrp=[], hip=[], cvp=[], ks=[], vs=[], hrs=[], his=[], cvs=[])
    k_stack = v_stack = None
    wgu_b, wd_b = _cast_layer(prm["ffn1_wgu"], 0), _cast_layer(prm["ffn1_wd"], 0)
    for layer in range(depth):
        i = layer // 2
        g, b = prm["ln_g"][layer], prm["ln_b"][layer]
        x32, xb, wgu_b, wd_b = _ffn(x32, xb, wgu_b, wd_b, 0, row(g[0]), row(b[0]), alpha, tm, tf,
                                    convert_next=(prm["ffn2_wgu"], prm["ffn2_wd"], layer))
        if layer % 2 == 0:
            assert d - moba_w == moba_w
            (q, q_r), (k_stack, k_rest), (v_stack, v_rest), (u, u_r) = _hyb_in(xb, prm["hyb_w_in"], i, mp, k_stack,
                                                                               v_stack)
            attn = _moba_prompt(q, k_stack, v_stack, i, bp, tp, nh, hd, m_all)
            attn_s = _sample_attn(q_r[:ms], k_rest[:ms], v_rest[:ms], kv["cache_k"], kv["cache_v"], i,
                                  kv["page_table"], bs, ts, nh, hd)
            zero = jnp.zeros((bp, 1, gp), F32)
            y, hr_p, hi_p = _s5(u, zero, zero, s5_prompt[i], s5_prompt[i]["seg_len"], True, m_out=m_all)
            y_s, hr_s, hi_s = _s5(u_r[:ms], s5_re0[i].reshape(bs, gp), s5_im0[i].reshape(bs, gp), s5_sample[i],
                                  s5_sample[i]["seg_len"], False)
            x32, xb = _mm_res_ln([_fill_rest(attn, attn_s, mp), _fill_rest(y, y_s, mp)], prm["hyb_w_out"], i, x32,
                                 row(g[1]), row(b[1]), alpha, tm, glu=(prm["s5_glu_w"], row(prm["s5_glu_b"][i])))
            out["ks"].append(k_rest[:ms].reshape(bs, ts, nh, hd))
            out["vs"].append(v_rest[:ms].reshape(bs, ts, nh, hd))
            out["hrp"].append(hr_p.reshape(bp, g_cnt, p_cnt))
            out["hip"].append(hi_p.reshape(bp, g_cnt, p_cnt))
            out["hrs"].append(hr_s.reshape(bs, g_cnt, p_cnt))
            out["his"].append(hi_s.reshape(bs, g_cnt, p_cnt))
        else:
            c, cv_p = _conv_in_core(xb, prm["conv_w_in"], conv0_prompt[i], prm["conv_w"], i, bp, tp,
                                    _pick_tile(tp, 512), _pick_tile(d, 1024))
            z = _mm(xb[sample], prm["conv_w_in"], i, 0, 3 * d, ms, _pick_tile(d, 2048))
            c_s, cv_s = _conv_core(z, conv0_sample[i], prm["conv_w"], i, bs, ts, ts, _pick_tile(d, 512))
            x32, xb = _mm_res_ln([_fill_rest(c, c_s, mp)], prm["conv_w_out"], i, x32, row(g[1]), row(b[1]),
                                 alpha, tm)
            out["cvp"].append(cv_p)
            out["cvs"].append(cv_s)
        if layer < depth - 1:
            x32, xb, wgu_b, wd_b = _ffn(x32, xb, wgu_b, wd_b, 0, row(g[2]), row(b[2]), alpha, tm, tf,
                                        convert_next=(prm["ffn1_wgu"], prm["ffn1_wd"], layer + 1))
        else:
            y_prompt, y_rest = _ffn(x32, xb, wgu_b, wd_b, 0, row(g[2]), row(b[2]), alpha, tm, tf, mp=mp)
    out = {n: jnp.stack(a) for n, a in out.items()}
    n_hyb = k_stack.shape[0]
    out["kp"] = k_stack.reshape(n_hyb, bp, tp, nh, hd)
    out["vp"] = v_stack.reshape(n_hyb, bp, tp, nh, hd)
    return y_prompt.reshape(bp, tp, d), y_rest[:ms].reshape(bs, ts, d), out


def kernel(x_prompt, x_sample, cache_k, cache_v, state_s5_re, state_s5_im, state_conv, page_table, ffn1_wgu, ffn1_wd, ffn2_wgu, ffn2_wd, ln_g, ln_b, hyb_w_in, hyb_w_out, s5_a_re, s5_a_im, s5_log_dt, s5_b_re, s5_b_im, s5_c_re, s5_c_im, s5_d, s5_glu_w, s5_glu_b, conv_w_in, conv_w, conv_w_out):
    n_hyb, _, _, nh, hd = cache_k.shape
    bp, t_prompt, d = x_prompt.shape
    db, t_sample, _ = x_sample.shape
    n_conv, cw = conv_w.shape[:2]
    g_cnt, p_cnt = s5_a_re.shape[1:]
    assert db == SUBLANES, "the sample scan lays the decode batch along the vreg sublanes"
    assert (s5_b_re.shape[-1] * S5_LANE_GROUPS) == 128

    prm = dict(
        ffn1_wgu=ffn1_wgu, ffn1_wd=ffn1_wd, ffn2_wgu=ffn2_wgu, ffn2_wd=ffn2_wd,
        ln_g=ln_g, ln_b=ln_b, hyb_w_in=hyb_w_in.astype(BF16), hyb_w_out=hyb_w_out.astype(BF16),
        s5_a_re=s5_a_re, s5_glu_w=s5_glu_w.astype(BF16), s5_glu_b=s5_glu_b,
        conv_w_in=conv_w_in.astype(BF16), conv_w=conv_w.astype(F32), conv_w_out=conv_w_out.astype(BF16),
    )

    def s5_prms(seg_len):
        out = []
        for i in range(n_hyb):
            sp = _s5_discretise(s5_a_re[i], s5_a_im[i], s5_log_dt[i], s5_b_re[i], s5_b_im[i], s5_c_re[i],
                                s5_c_im[i], s5_d[i], seg_len)
            sp["seg_len"] = seg_len
            out.append(sp)
        return out

    prompt_seg = _pick_tile(t_prompt // SUBLANES, 32)
    kv = dict(nh=nh, hd=hd, cache_k=cache_k, cache_v=cache_v, page_table=page_table)
    conv_zero = jnp.zeros((n_conv, bp, cw - 1, d), x_prompt.dtype)
    y_p, y_s, o = _run_trunk(x_prompt, x_sample, prm, s5_prms(prompt_seg), s5_prms(t_sample), kv, state_s5_re,
                             state_s5_im, conv_zero, state_conv)
    return (y_p, y_s, o["kp"], o["vp"], o["hrp"], o["hip"], o["cvp"], o["ks"], o["vs"], o["hrs"], o["his"], o["cvs"])
```

```python
import functools

import jax
import jax.numpy as jnp
from jax import lax
from jax.experimental import pallas as pl
from jax.experimental.pallas import tpu as pltpu

MOBA_BLOCK = 256
MOBA_TOPK = 3
LN_EPS = 1e-5
LANES = 128
SUBLANES = 8
S5_LANE_GROUPS = 8
S5_SCAN_LANES = 1024
S5_SCAN_UNROLL = 4
CAST_BLOCK_BYTES = 4 * 1024 * 1024
BF16_ROWS = 16
MAX_ROW_TILE = 640
VMEM_LIMIT_BYTES = 56 * 1024 * 1024

F32 = jnp.float32
BF16 = jnp.bfloat16
NEG_INF = float("-inf")


def _cparams(semantics):
    return pltpu.CompilerParams(dimension_semantics=semantics, vmem_limit_bytes=VMEM_LIMIT_BYTES)


def _layer_norm(y, g, b):
    mu = jnp.mean(y, axis=-1, keepdims=True)
    d = y - mu
    var = jnp.mean(d * d, axis=-1, keepdims=True)
    return d * lax.rsqrt(var + LN_EPS) * g + b


def _convert_slab(s, n_steps, src_ref, layer, dst_ref, inbuf, outbuf, sem_in, sem_out, rows):
    last_slab = src_ref.shape[1] // rows - 1

    def row0(k):
        return pl.multiple_of(jnp.minimum(k, last_slab) * rows, rows)

    def fetch(k, slot):
        return pltpu.make_async_copy(src_ref.at[layer, pl.ds(row0(k), rows), :], inbuf.at[slot], sem_in.at[slot])

    def send(k):
        return pltpu.make_async_copy(outbuf, dst_ref.at[0, pl.ds(row0(k), rows), :], sem_out)

    slot = lax.rem(s, 2)

    @pl.when(s == 0)
    def _():
        fetch(0, 0).start()
        outbuf[...] = jnp.zeros_like(outbuf)
        send(0).start()

    fetch(s, slot).wait()
    fetch(s + 1, 1 - slot).start()
    send(s).wait()
    outbuf[...] = inbuf[slot].astype(BF16)
    send(s).start()

    @pl.when(s == n_steps - 1)
    def _():
        fetch(s + 1, 1 - slot).wait()
        send(s).wait()


def _cast_kernel(w_ref, o_ref):
    o_ref[...] = w_ref[...].astype(o_ref.dtype)


def _cast_layer(w, layer):
    _, r, c = w.shape
    rows = max(BF16_ROWS, CAST_BLOCK_BYTES // (4 * c))
    br = _pick_tile(r, 1 << (rows.bit_length() - 1))
    return pl.pallas_call(
        _cast_kernel,
        grid=(r // br,),
        in_specs=[pl.BlockSpec((None, br, c), lambda i: (layer, i, 0))],
        out_specs=pl.BlockSpec((None, br, c), lambda i: (0, i, 0)),
        out_shape=jax.ShapeDtypeStruct((1, r, c), BF16),
        compiler_params=_cparams(("parallel",)),
        name="cast_bf16",
    )(w)


def _convert_rows(n_rows, n_steps):
    return next(r for r in range(BF16_ROWS, n_rows + 1, BF16_ROWS) if n_rows % r == 0 and n_rows // r <= n_steps)


def _ffn_kernel(*refs, alpha, nf, rest, converting):
    x32_ref, xb_ref, wg_ref, wu_ref, wd_ref, g_ref, b_ref = refs[:7]
    if converting:
        (ngu_src, nd_src, o32_ref, aux_ref, ngu_dst, nd_dst, gu_in, gu_out, d_in, d_out, sem_in, sem_out) = refs[7:]
    else:
        o32_ref, aux_ref = refs[7:]
    i, j = pl.program_id(0), pl.program_id(1)
    if converting:
        step, n_steps = i * nf + j, pl.num_programs(0) * nf
        _convert_slab(step, n_steps, ngu_src, converting[0], ngu_dst, gu_in, gu_out, sem_in.at[0], sem_out.at[0],
                      gu_in.shape[1])
        _convert_slab(step, n_steps, nd_src, converting[0], nd_dst, d_in, d_out, sem_in.at[1], sem_out.at[1],
                      d_in.shape[1])

    @pl.when(j == 0)
    def _():
        o32_ref[...] = jnp.zeros_like(o32_ref)

    xb = xb_ref[...]
    gate = jnp.dot(xb, wg_ref[...], preferred_element_type=F32)
    up = jnp.dot(xb, wu_ref[...], preferred_element_type=F32)
    h = (gate * jax.nn.sigmoid(gate)) * up
    o32_ref[...] += jnp.dot(h.astype(BF16), wd_ref[...], preferred_element_type=F32)

    @pl.when(j == nf - 1)
    def _():
        z = _layer_norm(alpha * x32_ref[...] + 0.5 * o32_ref[...], g_ref[...], b_ref[...])
        o32_ref[...] = z
        if rest is None:
            aux_ref[...] = z.astype(BF16)
        else:
            @pl.when(i == pl.num_programs(0) - 1)
            def _():
                aux_ref[...] = z[rest[0]:rest[0] + rest[1], :]


def _ffn(x32, xb, wgu, wd, layer, g, b, alpha, tm, tf, mp=None, convert_next=None):
    m, d = x32.shape
    f = wd.shape[1]
    nf = f // tf
    n_steps = (m // tm) * nf
    if mp is None:
        rest = None
        out_specs = [pl.BlockSpec((tm, d), lambda i, j: (i, 0)), pl.BlockSpec((tm, d), lambda i, j: (i, 0))]
        out_shape = [jax.ShapeDtypeStruct((m, d), F32), jax.ShapeDtypeStruct((m, d), BF16)]
    else:
        rest = _rest_rows(m, mp, tm)
        out_specs = [pl.BlockSpec((tm, d), lambda i, j: (i, 0)), pl.BlockSpec((rest[1], d), lambda i, j: (0, 0))]
        out_shape = [jax.ShapeDtypeStruct((mp, d), F32), jax.ShapeDtypeStruct((rest[1], d), F32)]
    in_specs = [
        pl.BlockSpec((tm, d), lambda i, j: (i, 0)),
        pl.BlockSpec((tm, d), lambda i, j: (i, 0)),
        pl.BlockSpec((None, d, tf), lambda i, j: (layer, 0, j)),
        pl.BlockSpec((None, d, tf), lambda i, j: (layer, 0, j + nf)),
        pl.BlockSpec((None, tf, d), lambda i, j: (layer, j, 0)),
        pl.BlockSpec((1, d), lambda i, j: (0, 0)),
        pl.BlockSpec((1, d), lambda i, j: (0, 0)),
    ]
    operands = [x32, xb, wgu, wgu, wd, g, b]
    scratch, converting = [], None
    if convert_next is not None:
        src_gu, src_d, next_layer = convert_next
        converting = (next_layer,)
        gu_rows, d_rows = _convert_rows(d, n_steps), _convert_rows(f, n_steps)
        any_spec = pl.BlockSpec(memory_space=pl.ANY)
        in_specs += [any_spec, any_spec]
        operands += [src_gu, src_d]
        out_specs += [any_spec, any_spec]
        out_shape += [jax.ShapeDtypeStruct((1, d, 2 * f), BF16), jax.ShapeDtypeStruct((1, f, d), BF16)]
        scratch = [pltpu.VMEM((2, gu_rows, 2 * f), F32), pltpu.VMEM((gu_rows, 2 * f), BF16),
                   pltpu.VMEM((2, d_rows, d), F32), pltpu.VMEM((d_rows, d), BF16),
                   pltpu.SemaphoreType.DMA((2, 2)), pltpu.SemaphoreType.DMA((2,))]
    return pl.pallas_call(
        functools.partial(_ffn_kernel, alpha=alpha, nf=nf, rest=rest, converting=converting),
        grid=(m // tm, nf),
        in_specs=in_specs,
        out_specs=out_specs,
        out_shape=out_shape,
        scratch_shapes=scratch,
        compiler_params=_cparams(("arbitrary", "arbitrary") if converting else ("parallel", "arbitrary")),
        name="ffn_ln",
    )(*operands)


def _mm_kernel(x_ref, w_ref, o_ref):
    o_ref[...] = jnp.dot(x_ref[...], w_ref[...], preferred_element_type=F32)


def _mm(xb, w, layer, col0, n, tm, tn):
    m, k = xb.shape
    assert col0 % tn == 0 and n % tn == 0
    jb0 = col0 // tn
    return pl.pallas_call(
        _mm_kernel,
        grid=(m // tm, n // tn),
        in_specs=[
            pl.BlockSpec((tm, k), lambda i, j: (i, 0)),
            pl.BlockSpec((None, k, tn), lambda i, j: (layer, 0, jb0 + j)),
        ],
        out_specs=pl.BlockSpec((tm, tn), lambda i, j: (i, j)),
        out_shape=jax.ShapeDtypeStruct((m, n), F32),
        compiler_params=_cparams(("parallel", "arbitrary")),
        name="proj",
    )(xb, w)


def _rest_rows(m_all, mp, tm):
    first = mp - (m_all // tm - 1) * tm
    assert m_all % tm == 0 and 0 <= first and first + (m_all - mp) == tm
    return first, m_all - mp


def _hyb_in_kernel(x_ref, xr_ref, w_ref, *refs, n_alias):
    qp_ref, qr_ref, kp_ref, kr_ref, vp_ref, vr_ref, up_ref, ur_ref = refs[n_alias:]
    j, i = pl.program_id(0), pl.program_id(1)
    last = i == pl.num_programs(1) - 1
    w = w_ref[...]
    width = w.shape[1] // 2
    y = jnp.dot(x_ref[...], w, preferred_element_type=F32)

    def emit(lo_ref, hi_ref, lo_rest_ref, hi_rest_ref):
        lo_ref[...] = y[:, :width]
        hi_ref[...] = y[:, width:]

        @pl.when(last)
        def _():
            yr = jnp.dot(xr_ref[...], w, preferred_element_type=F32)
            lo_rest_ref[...] = yr[:, :width]
            hi_rest_ref[...] = yr[:, width:]

    @pl.when(j == 0)
    def _():
        emit(qp_ref, kp_ref, qr_ref, kr_ref)

    @pl.when(j == 1)
    def _():
        emit(vp_ref, up_ref, vr_ref, ur_ref)


def _hyb_in(xb, w, layer, mp, k_stack=None, v_stack=None):
    m, k = xb.shape
    n_layers = w.shape[0]
    width = w.shape[2] // 4
    n_rest = m - mp
    assert mp % n_rest == 0
    tm = _pick_tile(mp, 512)
    aliased = [] if k_stack is None else [k_stack, v_stack]
    n_tiles = mp // tm
    tile = (lambda j, i: i * (1 - j) + (n_tiles - 1) * j, lambda j, i: i * j)
    prompt_spec = [pl.BlockSpec((tm, width), lambda j, i, p=p: (tile[p](j, i), 0)) for p in (0, 1)]
    stack_spec = [pl.BlockSpec((None, tm, width), lambda j, i, p=p: (layer, tile[p](j, i), 0)) for p in (0, 1)]
    rest_spec = pl.BlockSpec((n_rest, width), lambda j, i: (0, 0))
    prompt_shape = jax.ShapeDtypeStruct((mp, width), F32)
    stack_shape = jax.ShapeDtypeStruct((n_layers, mp, width), F32)
    rest_shape = jax.ShapeDtypeStruct((n_rest, width), F32)
    qp, qr, kp, kr, vp, vr, up, ur = pl.pallas_call(
        functools.partial(_hyb_in_kernel, n_alias=len(aliased)),
        grid=(2, n_tiles),
        in_specs=[
            pl.BlockSpec((tm, k), lambda j, i: (i, 0)),
            pl.BlockSpec((n_rest, k), lambda j, i: (mp // n_rest, 0)),
            pl.BlockSpec((None, k, 2 * width), lambda j, i: (layer, 0, j)),
        ] + [pl.BlockSpec(memory_space=pl.ANY)] * len(aliased),
        out_specs=[prompt_spec[0], rest_spec, stack_spec[0], rest_spec, stack_spec[1], rest_spec, prompt_spec[1],
                   rest_spec],
        out_shape=[prompt_shape, rest_shape, stack_shape, rest_shape, stack_shape, rest_shape, prompt_shape,
                   rest_shape],
        input_output_aliases={3: 2, 4: 4} if aliased else {},
        compiler_params=_cparams(("arbitrary", "arbitrary")),
        name="hyb_in_proj",
    )(xb, xb, w, *aliased)
    return (qp, qr), (kp, kr), (vp, vr), (up, ur)


def _mm_res_ln_kernel(*refs, nparts, gated, alpha):
    a_refs = refs[:nparts]
    w_refs = refs[nparts:2 * nparts]
    rest = refs[2 * nparts:]
    if gated:
        gw_ref, gb_ref = rest[:2]
        rest = rest[2:]
    x32_ref, g_ref, b_ref, o32_ref, ob_ref = rest
    y = alpha * x32_ref[...]
    for r, (a_ref, w_ref) in enumerate(zip(a_refs, w_refs)):
        a = a_ref[...]
        if gated and r == nparts - 1:
            gate = jnp.dot(a.astype(BF16), gw_ref[...], preferred_element_type=F32) + gb_ref[...]
            a = a * jax.nn.sigmoid(gate)
        y = y + jnp.dot(a.astype(BF16), w_ref[...], preferred_element_type=F32)
    z = _layer_norm(y, g_ref[...], b_ref[...])
    o32_ref[...] = z
    ob_ref[...] = z.astype(BF16)


def _mm_res_ln(parts, w, layer, x32, g, b, alpha, tm, glu=None):
    m, d = x32.shape
    kp = parts[0].shape[1]
    assert all(p.shape[1] == kp for p in parts)
    once = dict(pipeline_mode=pl.Buffered(1))
    in_specs = [pl.BlockSpec((tm, kp), lambda i: (i, 0)) for _ in parts]
    in_specs += [pl.BlockSpec((None, kp, d), lambda i, r=r: (layer, r, 0), **once) for r in range(len(parts))]
    operands = [*parts, *([w] * len(parts))]
    if glu is not None:
        in_specs += [pl.BlockSpec((None, kp, kp), lambda i: (layer, 0, 0), **once), pl.BlockSpec((1, kp), lambda i: (0, 0))]
        operands += list(glu)
    in_specs += [
        pl.BlockSpec((tm, d), lambda i: (i, 0)),
        pl.BlockSpec((1, d), lambda i: (0, 0)),
        pl.BlockSpec((1, d), lambda i: (0, 0)),
    ]
    return pl.pallas_call(
        functools.partial(_mm_res_ln_kernel, nparts=len(parts), gated=glu is not None, alpha=alpha),
        grid=(m // tm,),
        in_specs=in_specs,
        out_specs=[pl.BlockSpec((tm, d), lambda i: (i, 0)), pl.BlockSpec((tm, d), lambda i: (i, 0))],
        out_shape=[jax.ShapeDtypeStruct((m, d), F32), jax.ShapeDtypeStruct((m, d), BF16)],
        compiler_params=_cparams(("parallel",)),
        name="out_proj_ln",
    )(*operands, x32, g, b)


CONV_PAD = 8
CONV_COL_SLAB = 256


def _conv_tile(project, st_ref, w_ref, c_ref, ns_ref, hbuf, *, tt, nt, cw):
    t = pl.program_id(2)
    lo = CONV_PAD - (cw - 1)
    tn = c_ref.shape[1]
    slab = min(tn, CONV_COL_SLAB)

    @pl.when(t == 0)
    def _():
        hbuf[lo:CONV_PAD, :] = st_ref[...]

    for c0 in range(0, tn, slab):
        cols = slice(c0, c0 + slab)
        bg, cg, v = project(cols)
        hbuf[CONV_PAD:CONV_PAD + tt, cols] = cg * v
        w = w_ref[:, cols]
        conv = w[0:1, :] * hbuf[lo:lo + tt, cols]
        for j in range(1, cw):
            conv = conv + w[j:j + 1, :] * hbuf[lo + j:lo + j + tt, cols]
        c_ref[:, cols] = (bg * conv).astype(c_ref.dtype)
    tail = hbuf[lo + tt:CONV_PAD + tt, :]

    @pl.when(t == nt - 1)
    def _():
        ns_ref[...] = tail

    hbuf[lo:CONV_PAD, :] = tail


def _conv_kernel(bg_ref, cg_ref, v_ref, st_ref, w_ref, c_ref, ns_ref, hbuf, **kw):
    _conv_tile(lambda cols: (bg_ref[:, cols], cg_ref[:, cols], v_ref[:, cols]), st_ref, w_ref, c_ref, ns_ref, hbuf,
               **kw)


def _conv_in_kernel(x_ref, wb_ref, wc_ref, wv_ref, st_ref, w_ref, c_ref, ns_ref, hbuf, **kw):
    def project(cols):
        x = x_ref[...]
        return tuple(jnp.dot(x, w[:, cols], preferred_element_type=F32) for w in (wb_ref, wc_ref, wv_ref))

    _conv_tile(project, st_ref, w_ref, c_ref, ns_ref, hbuf, **kw)


def _conv_core(z, state, conv_w, layer, bsz, t_len, tt, tn):
    m, d3 = z.shape
    d = d3 // 3
    cw = conv_w.shape[1]
    nt, nj = t_len // tt, d // tn
    return pl.pallas_call(
        functools.partial(_conv_kernel, tt=tt, nt=nt, cw=cw),
        grid=(bsz, nj, nt),
        in_specs=[
            pl.BlockSpec((tt, tn), lambda b, j, t: (b * nt + t, j)),
            pl.BlockSpec((tt, tn), lambda b, j, t: (b * nt + t, j + nj)),
            pl.BlockSpec((tt, tn), lambda b, j, t: (b * nt + t, j + 2 * nj)),
            pl.BlockSpec((None, cw - 1, tn), lambda b, j, t: (b, 0, j)),
            pl.BlockSpec((None, cw, tn), lambda b, j, t: (layer, 0, j)),
        ],
        out_specs=[
            pl.BlockSpec((tt, tn), lambda b, j, t: (b * nt + t, j)),
            pl.BlockSpec((None, cw - 1, tn), lambda b, j, t: (b, 0, j)),
        ],
        out_shape=[jax.ShapeDtypeStruct((m, d), BF16), jax.ShapeDtypeStruct((bsz, cw - 1, d), F32)],
        scratch_shapes=[pltpu.VMEM((CONV_PAD + tt, tn), F32)],
        compiler_params=_cparams(("parallel", "parallel", "arbitrary")),
        name="conv_core",
    )(z, z, z, state, conv_w)


def _conv_in_core(xb, w_in, state, conv_w, layer, bsz, t_len, tt, tn):
    m, d = xb.shape
    cw = conv_w.shape[1]
    nt, nj = t_len // tt, d // tn
    return pl.pallas_call(
        functools.partial(_conv_in_kernel, tt=tt, nt=nt, cw=cw),
        grid=(bsz, nj, nt),
        in_specs=[
            pl.BlockSpec((tt, d), lambda b, j, t: (b * nt + t, 0)),
            pl.BlockSpec((None, d, tn), lambda b, j, t: (layer, 0, j)),
            pl.BlockSpec((None, d, tn), lambda b, j, t: (layer, 0, j + nj)),
            pl.BlockSpec((None, d, tn), lambda b, j, t: (layer, 0, j + 2 * nj)),
            pl.BlockSpec((None, cw - 1, tn), lambda b, j, t: (b, 0, j)),
            pl.BlockSpec((None, cw, tn), lambda b, j, t: (layer, 0, j)),
        ],
        out_specs=[
            pl.BlockSpec((tt, tn), lambda b, j, t: (b * nt + t, j)),
            pl.BlockSpec((None, cw - 1, tn), lambda b, j, t: (b, 0, j)),
        ],
        out_shape=[jax.ShapeDtypeStruct((m, d), BF16), jax.ShapeDtypeStruct((bsz, cw - 1, d), F32)],
        scratch_shapes=[pltpu.VMEM((CONV_PAD + tt, tn), F32)],
        compiler_params=_cparams(("parallel", "parallel", "arbitrary")),
        name="conv_in_core",
    )(xb, w_in, w_in, w_in, state, conv_w)


def _moba_select_rows(gate_t, n_valid):
    nb, nq = gate_t.shape
    sub = lax.broadcasted_iota(jnp.int32, (nb, nq), 0)
    valid = sub < n_valid
    gate_t = jnp.where(valid, gate_t, NEG_INF)
    beaten_by = jnp.zeros((nb, nq), jnp.int32)
    for jp in range(nb):
        row = gate_t[jp:jp + 1, :]
        beats = (row > gate_t) | ((row == gate_t) & (sub > jp))
        beaten_by = beaten_by + jnp.where(beats, 1, 0)
    return valid & (beaten_by < MOBA_TOPK)


def _moba_prompt_kernel(q_ref, k_ref, v_ref, o_ref, kb_scr, vt_scr, km_scr, *, nb, scale):
    blk = MOBA_BLOCK
    kb_scr[...] = k_ref[...].astype(BF16)
    for j in range(nb):
        rows = slice(j * blk, (j + 1) * blk)
        km_scr[j:j + 1, :] = jnp.mean(k_ref[rows, :], axis=0, keepdims=True)
        vt_scr[:, rows] = v_ref[rows, :].T.astype(BF16)
    causal = lax.broadcasted_iota(jnp.int32, (blk, blk), 0) <= lax.broadcasted_iota(jnp.int32, (blk, blk), 1)
    causal_bias = jnp.where(causal, 0.0, NEG_INF).astype(F32)

    for qi in range(nb):
        n_keys = (qi + 1) * blk
        q = q_ref[qi * blk:(qi + 1) * blk, :]
        s_all = lax.dot_general(kb_scr[0:n_keys, :], q.astype(BF16), (((1,), (1,)), ((), ())),
                                preferred_element_type=F32) * scale
        if qi > 0:
            gate_t = lax.dot_general(km_scr[...], q, (((1,), (1,)), ((), ())),
                                     precision=lax.Precision.HIGHEST, preferred_element_type=F32)
            sel_bias = jnp.where(_moba_select_rows(gate_t, qi), 0.0, NEG_INF).astype(F32)
        s_blocks = [s_all[j * blk:(j + 1) * blk, :] + sel_bias[j:j + 1, :] for j in range(qi)]
        s_blocks.append(s_all[qi * blk:n_keys, :] + causal_bias)
        top = s_blocks[0]
        for sb in s_blocks[1:]:
            top = jnp.maximum(top, sb)
        m = jnp.max(top, axis=0, keepdims=True)
        p_blocks = [jnp.exp(sb - m) for sb in s_blocks]
        tot = p_blocks[0]
        for pb in p_blocks[1:]:
            tot = tot + pb
        l = jnp.sum(tot, axis=0, keepdims=True)
        p_all = jnp.concatenate([pb.astype(BF16) for pb in p_blocks], axis=0)
        acc = jnp.dot(vt_scr[:, 0:n_keys], p_all, preferred_element_type=F32)
        o_ref[qi * blk:(qi + 1) * blk, :] = (acc / l).T.astype(o_ref.dtype)


def _moba_prompt(q, k_stack, v_stack, layer, bsz, t_len, nh, hd, m_out):
    m = m_out
    assert t_len % MOBA_BLOCK == 0
    nb = t_len // MOBA_BLOCK
    seq_spec = pl.BlockSpec((t_len, hd), lambda b, h: (b, h))
    kv_spec = pl.BlockSpec((None, t_len, hd), lambda b, h: (layer, b, h))
    return pl.pallas_call(
        functools.partial(_moba_prompt_kernel, nb=nb, scale=hd ** -0.5),
        grid=(bsz, nh),
        in_specs=[seq_spec, kv_spec, kv_spec],
        out_specs=seq_spec,
        out_shape=jax.ShapeDtypeStruct((m, nh * hd), BF16),
        scratch_shapes=[pltpu.VMEM((t_len, hd), BF16), pltpu.VMEM((hd, t_len), BF16),
                        pltpu.VMEM((nb, hd), F32)],
        compiler_params=_cparams(("parallel", "parallel")),
        name="moba_prompt",
    )(q, k_stack, v_stack)


def _moba_select_grouped(gate, n_valid_blocks, group):
    rows, nl = gate.shape
    lane = lax.broadcasted_iota(jnp.int32, (rows, nl), 1)
    valid = lane < n_valid_blocks * group
    gate = jnp.where(valid, gate, NEG_INF)
    beaten_by = jnp.zeros((rows, nl), jnp.int32)
    for jp in range(0, nl, group):
        col = gate[:, jp:jp + 1]
        beats = (col > gate) | ((col == gate) & (lane >= jp + group))
        beaten_by = beaten_by + jnp.where(beats, 1, 0)
    return valid & (beaten_by < MOBA_TOPK)


def _fold_lanes(x, op):
    tiles = [x[:, i:i + LANES] for i in range(0, x.shape[1], LANES)]
    out = tiles[0]
    for t in tiles[1:]:
        out = op(out, t)
    return out


def _sample_attn_kernel(pt_ref, q_ref, kn_ref, vn_ref, hbias_ref, obias_ref, *refs, nh, n_past_blocks,
                        pages_per_step, pages_per_block, scale):
    del pt_ref
    kp_refs = refs[:pages_per_step]
    vp_refs = refs[pages_per_step:2 * pages_per_step]
    o_ref, gate_scr, m_scr, l_scr, acc_scr = refs[2 * pages_per_step:]
    step = pl.program_id(1)
    n_steps = pl.num_programs(1)
    rows, hd = q_ref.shape
    n_pages = m_scr.shape[1]
    q = q_ref[...]
    qb = q.astype(BF16)
    q3 = q.reshape(rows // nh, nh, hd)
    lane = lax.broadcasted_iota(jnp.int32, (rows, n_pages), 1)

    def put_column(scr, pg, col):
        scr[...] = jnp.where(lane == pg, col, scr[...])

    scores, probs = [], []
    for r in range(pages_per_step):
        k = kp_refs[r][...]
        s = lax.dot_general(qb, k.astype(BF16), (((1,), (1,)), ((), ())), preferred_element_type=F32)
        scores.append(s * scale + hbias_ref[...])
    for blk0 in range(0, pages_per_step, pages_per_block):
        ksum = jnp.zeros((nh, hd), F32)
        for r in range(blk0, blk0 + pages_per_block):
            k = kp_refs[r][...]
            ksum = ksum + jnp.sum(k.reshape(k.shape[0] // nh, nh, hd), axis=0)
        kmean = ksum / MOBA_BLOCK
        gcol = jnp.sum(q3 * kmean[None, :, :], axis=2, keepdims=True).reshape(rows, 1)
        for r in range(blk0, blk0 + pages_per_block):
            put_column(gate_scr, step * pages_per_step + r, gcol)
    for r, s in enumerate(scores):
        pg = step * pages_per_step + r
        m = jnp.max(_fold_lanes(s, jnp.maximum), axis=1, keepdims=True)
        p = jnp.exp(s - m)
        put_column(m_scr, pg, m)
        put_column(l_scr, pg, jnp.sum(_fold_lanes(p, jnp.add), axis=1, keepdims=True))
        probs.append(p.astype(BF16))
    for r, p in enumerate(probs):
        acc_scr[step * pages_per_step + r] = jnp.dot(p, vp_refs[r][...].astype(BF16), preferred_element_type=F32)

    @pl.when(step == n_steps - 1)
    def _():
        keep = _moba_select_grouped(gate_scr[...], n_past_blocks, pages_per_block)
        m_pages = m_scr[...] + jnp.where(keep, 0.0, NEG_INF)
        s = lax.dot_general(qb, kn_ref[...].astype(BF16), (((1,), (1,)), ((), ())), preferred_element_type=F32)
        s = s * scale + obias_ref[...]
        m_own = jnp.max(s, axis=1, keepdims=True)
        m_tot = jnp.maximum(m_own, jnp.max(m_pages, axis=1, keepdims=True))
        p = jnp.exp(s - m_tot)
        w = jnp.exp(m_pages - m_tot)
        l = jnp.sum(p, axis=1, keepdims=True) + jnp.sum(w * l_scr[...], axis=1, keepdims=True)
        acc = jnp.dot(p.astype(BF16), vn_ref[...].astype(BF16), preferred_element_type=F32)
        for pg in range(n_pages):
            acc = acc + w[:, pg:pg + 1] * acc_scr[pg]
        o_ref[...] = (acc / l).astype(o_ref.dtype)


def _sample_attn(q, k_new, v_new, cache_k, cache_v, layer, page_table, db, t_len, nh, hd, pages_per_step=16):
    n_hyb, n_pool, page, _, _ = cache_k.shape
    n_pages = page_table.shape[1]
    past_len = n_pages * page
    assert past_len % MOBA_BLOCK == 0 and t_len <= MOBA_BLOCK and MOBA_BLOCK % page == 0
    assert nh & (nh - 1) == 0, "head matching uses bit masks"
    ppb = MOBA_BLOCK // page
    pages_per_step = min(pages_per_step, n_pages)
    assert pages_per_step % ppb == 0 and n_pages % pages_per_step == 0
    rows = t_len * nh
    ck = cache_k.reshape(n_hyb, n_pool, page * nh, hd)
    cv = cache_v.reshape(n_hyb, n_pool, page * nh, hd)
    r = jnp.arange(rows, dtype=jnp.int32)[:, None]
    c = jnp.arange(page * nh, dtype=jnp.int32)[None, :]
    same_head = (r & (nh - 1)) == (c & (nh - 1))
    head_bias = jnp.where(same_head, 0.0, NEG_INF).astype(F32)
    own_bias = jnp.where(same_head[:, :rows] & (c[:, :rows] // nh <= r // nh), 0.0, NEG_INF).astype(F32)

    def page_spec(i):
        return pl.BlockSpec((None, None, page * nh, hd),
                            lambda b, s, pt: (layer, pt[b, s * pages_per_step + i], 0, 0))

    def row_spec(nrows):
        return pl.BlockSpec((None, nrows, hd), lambda b, s, pt: (b, 0, 0))

    full = lambda a: pl.BlockSpec(a.shape, lambda b, s, pt: (0,) * a.ndim)
    out = pl.pallas_call(
        functools.partial(_sample_attn_kernel, nh=nh, n_past_blocks=past_len // MOBA_BLOCK,
                          pages_per_step=pages_per_step, pages_per_block=ppb, scale=hd ** -0.5),
        grid_spec=pltpu.PrefetchScalarGridSpec(
            num_scalar_prefetch=1,
            grid=(db, n_pages // pages_per_step),
            in_specs=[row_spec(rows), row_spec(rows), row_spec(rows), full(head_bias), full(own_bias)]
            + [page_spec(i) for i in range(pages_per_step)] * 2,
            out_specs=row_spec(rows),
            scratch_shapes=[pltpu.VMEM((rows, n_pages), F32), pltpu.VMEM((rows, n_pages), F32),
                            pltpu.VMEM((rows, n_pages), F32), pltpu.VMEM((n_pages, rows, hd), F32)],
        ),
        out_shape=jax.ShapeDtypeStruct((db, rows, hd), BF16),
        compiler_params=_cparams(("parallel", "arbitrary")),
        name="sample_attn",
    )(page_table, q.reshape(db, rows, hd), k_new.reshape(db, rows, hd), v_new.reshape(db, rows, hd),
      head_bias, own_bias, *([ck] * pages_per_step), *([cv] * pages_per_step))
    return out.reshape(db * t_len, nh * hd)


def _s5_kernel(u_ref, perm_ref, unperm_ref, bre_ref, bim_ref, cre_ref, cimn_ref, lre_ref, lim_ref, llre_ref, llim_ref,
               d_ref, h0re_ref, h0im_ref, y_ref, htre_ref, htim_ref, hr_scr, hi_scr, cre_scr, cim_scr, yc_scr,
               *, seg_len, chained, n_chunks):
    c = pl.program_id(1)
    nseg = SUBLANES
    nstate = hr_scr.shape[1]
    n_kb = bre_ref.shape[0]
    ku = bre_ref.shape[1]
    ks = bre_ref.shape[2]

    up = jnp.dot(perm_ref[...], u_ref[...].astype(BF16), preferred_element_type=F32).astype(BF16)
    for kb in range(n_kb):
        ukb = up[:, kb * ku:(kb + 1) * ku]
        hr_scr[:, kb * ks:(kb + 1) * ks] = jnp.dot(ukb, bre_ref[kb], preferred_element_type=F32)
        hi_scr[:, kb * ks:(kb + 1) * ks] = jnp.dot(ukb, bim_ref[kb], preferred_element_type=F32)

    if chained:
        @pl.when(c == 0)
        def _():
            cre_scr[...] = jnp.zeros_like(cre_scr)
            cim_scr[...] = jnp.zeros_like(cim_scr)
            cre_scr[0:1, :] = h0re_ref[...]
            cim_scr[0:1, :] = h0im_ref[...]

    for lc in range(nstate // S5_SCAN_LANES):
        cols = slice(lc * S5_SCAN_LANES, (lc + 1) * S5_SCAN_LANES)
        lr = jnp.broadcast_to(lre_ref[:, cols], (nseg, S5_SCAN_LANES))
        li = jnp.broadcast_to(lim_ref[:, cols], (nseg, S5_SCAN_LANES))

        def advance(i, hr, hi):
            r = pl.multiple_of(i * nseg, nseg)
            nr = lr * hr - li * hi + hr_scr[pl.ds(r, nseg), cols]
            ni = lr * hi + li * hr + hi_scr[pl.ds(r, nseg), cols]
            return r, nr, ni

        def scan_only(i, carry):
            _, nr, ni = advance(i, *carry)
            return nr, ni

        def scan_store(i, carry):
            r, nr, ni = advance(i, *carry)
            hr_scr[pl.ds(r, nseg), cols] = nr
            hi_scr[pl.ds(r, nseg), cols] = ni
            return nr, ni

        if chained:
            er, ei = lax.fori_loop(0, seg_len, scan_only, (cre_scr[:, cols], cim_scr[:, cols]),
                                   unroll=S5_SCAN_UNROLL)
            llr, lli = llre_ref[:, cols], llim_ref[:, cols]
            fr, fi = er[0:1, :], ei[0:1, :]
            for j in range(1, nseg):
                cre_scr[j:j + 1, cols] = fr
                cim_scr[j:j + 1, cols] = fi
                if j < nseg - 1:
                    fr, fi = (er[j:j + 1, :] + (llr * fr - lli * fi), ei[j:j + 1, :] + (llr * fi + lli * fr))
            init = (cre_scr[:, cols], cim_scr[:, cols])
        else:
            init = (h0re_ref[:, cols], h0im_ref[:, cols])

        hr, hi = lax.fori_loop(0, seg_len, scan_store, init, unroll=S5_SCAN_UNROLL)

        if chained:
            last_r, last_i = hr[nseg - 1:nseg, :], hi[nseg - 1:nseg, :]
            cre_scr[:, cols] = jnp.zeros((nseg, S5_SCAN_LANES), F32)
            cim_scr[:, cols] = jnp.zeros((nseg, S5_SCAN_LANES), F32)
            cre_scr[0:1, cols] = last_r
            cim_scr[0:1, cols] = last_i

            @pl.when(c == n_chunks - 1)
            def _():
                htre_ref[:, cols] = last_r
                htim_ref[:, cols] = last_i
        else:
            htre_ref[:, cols] = hr
            htim_ref[:, cols] = hi

    for kb in range(n_kb):
        hre = hr_scr[:, kb * ks:(kb + 1) * ks].astype(BF16)
        him = hi_scr[:, kb * ks:(kb + 1) * ks].astype(BF16)
        ykb = (jnp.dot(hre, cre_ref[kb], preferred_element_type=F32)
               + jnp.dot(him, cimn_ref[kb], preferred_element_type=F32))
        yc_scr[:, kb * ku:(kb + 1) * ku] = ykb

    yc = yc_scr[...]
    hi = yc.astype(BF16)
    r1 = yc - hi.astype(F32)
    mid = r1.astype(BF16)
    lo = (r1 - mid.astype(F32)).astype(BF16)
    unperm = unperm_ref[...]
    y = (jnp.dot(unperm, hi, preferred_element_type=F32) + jnp.dot(unperm, mid, preferred_element_type=F32)
         + jnp.dot(unperm, lo, preferred_element_type=F32))
    y_ref[...] = jax.nn.gelu(y + d_ref[...] * u_ref[...])


def _s5_discretise(a_re, a_im, log_dt, b_re, b_im, c_re, c_im, d_skip, seg_len):
    g, p = a_re.shape
    ch = b_re.shape[-1]
    lam = lax.complex(a_re.astype(F32), a_im.astype(F32))
    dt = jnp.exp(log_dt.astype(F32))[:, None]
    lam_bar = jnp.exp(lam * dt)
    b_bar = ((lam_bar - 1.0) / lam)[..., None] * lax.complex(b_re.astype(F32), b_im.astype(F32))
    lam_seg = lam_bar
    assert seg_len & (seg_len - 1) == 0
    for _ in range(seg_len.bit_length() - 1):
        lam_seg = lam_seg * lam_seg
    gl = S5_LANE_GROUPS
    n_kb = g // gl
    eye = jnp.eye(gl, dtype=F32)

    def pack_b(x):
        x = x.reshape(n_kb, gl, p, ch).transpose(0, 1, 3, 2)
        return jnp.einsum("kgcp,gh->kgchp", x, eye).reshape(n_kb, gl * ch, gl * p)

    def pack_c(x):
        x = x.reshape(n_kb, gl, ch, p).transpose(0, 1, 3, 2)
        return jnp.einsum("kgpc,gh->kgphc", x, eye).reshape(n_kb, gl * p, gl * ch)

    flat = lambda x: x.reshape(1, g * p)
    return dict(
        bre=pack_b(jnp.real(b_bar)).astype(BF16), bim=pack_b(jnp.imag(b_bar)).astype(BF16),
        cre=pack_c(c_re.astype(F32)).astype(BF16), cimn=pack_c(-c_im.astype(F32)).astype(BF16),
        lre=flat(jnp.real(lam_bar)), lim=flat(jnp.imag(lam_bar)),
        llre=flat(jnp.real(lam_seg)), llim=flat(jnp.imag(lam_seg)),
        d=d_skip.astype(F32).reshape(1, g * ch),
    )


def _s5(u, h0_re, h0_im, prm, seg_len, chained, m_out=None):
    n_rows, w = u.shape
    m = n_rows if m_out is None else m_out
    gp = prm["lre"].shape[1]
    chunk = SUBLANES * seg_len
    if chained:
        nseq = h0_re.shape[0]
        n_chunks = n_rows // nseq // chunk
        grid = (nseq, n_chunks)
        u_spec = pl.BlockSpec((chunk, w), lambda b, c: (b * n_chunks + c, 0))
        h_spec = pl.BlockSpec((None, 1, gp), lambda b, c: (b, 0, 0))
        h_shape = jax.ShapeDtypeStruct((nseq, 1, gp), F32)
    else:
        assert n_rows == chunk
        n_chunks = 1
        grid = (1, 1)
        u_spec = pl.BlockSpec((chunk, w), lambda b, c: (0, 0))
        h_spec = pl.BlockSpec((SUBLANES, gp), lambda b, c: (0, 0))
        h_shape = jax.ShapeDtypeStruct((SUBLANES, gp), F32)
    full = lambda a: pl.BlockSpec(a.shape, lambda b, c: (0,) * a.ndim)
    names = ("bre", "bim", "cre", "cimn", "lre", "lim", "llre", "llim", "d")
    r = jnp.arange(chunk)
    perm = (((r % SUBLANES) * seg_len + r // SUBLANES)[:, None] == r[None, :]).astype(BF16)
    consts = [perm, perm.T]
    return pl.pallas_call(
        functools.partial(_s5_kernel, seg_len=seg_len, chained=chained, n_chunks=n_chunks),
        grid=grid,
        in_specs=[u_spec] + [full(a) for a in consts] + [full(prm[n]) for n in names] + [h_spec, h_spec],
        out_specs=[u_spec, h_spec, h_spec],
        out_shape=[jax.ShapeDtypeStruct((m, w), F32), h_shape, h_shape],
        scratch_shapes=[pltpu.VMEM((chunk, gp), F32), pltpu.VMEM((chunk, gp), F32),
                        pltpu.VMEM((SUBLANES, gp), F32), pltpu.VMEM((SUBLANES, gp), F32),
                        pltpu.VMEM((chunk, w), F32)],
        compiler_params=_cparams(("parallel", "arbitrary")),
        name="s5_scan",
    )(u, *consts, *[prm[n] for n in names], h0_re, h0_im)


def _pick_tile(n, target):
    t = min(n, target)
    while n % t:
        t //= 2
    return t


def _row_tile(m):
    return min(range(LANES, MAX_ROW_TILE + 1, LANES), key=lambda t: (-(-m // t) * t, -t))


def _fill_rest(full, sample_rows, mp):
    pad = full.shape[0] - mp - sample_rows.shape[0]
    rest = jnp.concatenate([sample_rows.astype(full.dtype), jnp.zeros((pad, full.shape[1]), full.dtype)], axis=0)
    return lax.dynamic_update_slice(full, rest, (mp, 0))


def _run_trunk(x_prompt, x_sample, prm, s5_prompt, s5_sample, kv, s5_re0, s5_im0, conv0_prompt, conv0_sample):
    bp, tp, d = x_prompt.shape
    bs, ts, _ = x_sample.shape
    mp, ms = bp * tp, bs * ts
    tm = _row_tile(mp + ms)
    m_all = -(-(mp + ms) // tm) * tm
    depth = prm["ffn1_wgu"].shape[0]
    alpha = (2.0 * depth) ** 0.25
    nh, hd = kv["nh"], kv["hd"]
    moba_w = nh * hd
    g_cnt, p_cnt = prm["s5_a_re"].shape[1:]
    gp = g_cnt * p_cnt
    tf = _pick_tile(prm["ffn1_wd"].shape[1], 512)
    sample = slice(mp, mp + ms)

    x32 = jnp.concatenate([x_prompt.reshape(mp, d).astype(F32), x_sample.reshape(ms, d).astype(F32),
                           jnp.zeros((m_all - mp - ms, d), F32)], axis=0)
    xb = x32.astype(BF16)
    row = lambda a: a.reshape(1, -1).astype(F32)
    out = dict(hrp=[], hip=[], cvp=[], ks=[], vs=[], hrs=[], his=[], cvs=[])
    k_stack = v_stack = None
    wgu_b, wd_b = _cast_layer(prm["ffn1_wgu"], 0), _cast_layer(prm["ffn1_wd"], 0)
    for layer in range(depth):
        i = layer // 2
        g, b = prm["ln_g"][layer], prm["ln_b"][layer]
        x32, xb, wgu_b, wd_b = _ffn(x32, xb, wgu_b, wd_b, 0, row(g[0]), row(b[0]), alpha, tm, tf,
                                    convert_next=(prm["ffn2_wgu"], prm["ffn2_wd"], layer))
        if layer % 2 == 0:
            assert d - moba_w == moba_w
            (q, q_r), (k_stack, k_rest), (v_stack, v_rest), (u, u_r) = _hyb_in(xb, prm["hyb_w_in"], i, mp, k_stack,
                                                                               v_stack)
            attn = _moba_prompt(q, k_stack, v_stack, i, bp, tp, nh, hd, m_all)
            attn_s = _sample_attn(q_r[:ms], k_rest[:ms], v_rest[:ms], kv["cache_k"], kv["cache_v"], i,
                                  kv["page_table"], bs, ts, nh, hd)
            zero = jnp.zeros((bp, 1, gp), F32)
            y, hr_p, hi_p = _s5(u, zero, zero, s5_prompt[i], s5_prompt[i]["seg_len"], True, m_out=m_all)
            y_s, hr_s, hi_s = _s5(u_r[:ms], s5_re0[i].reshape(bs, gp), s5_im0[i].reshape(bs, gp), s5_sample[i],
                                  s5_sample[i]["seg_len"], False)
            x32, xb = _mm_res_ln([_fill_rest(attn, attn_s, mp), _fill_rest(y, y_s, mp)], prm["hyb_w_out"], i, x32,
                                 row(g[1]), row(b[1]), alpha, tm, glu=(prm["s5_glu_w"], row(prm["s5_glu_b"][i])))
            out["ks"].append(k_rest[:ms].reshape(bs, ts, nh, hd))
            out["vs"].append(v_rest[:ms].reshape(bs, ts, nh, hd))
            out["hrp"].append(hr_p.reshape(bp, g_cnt, p_cnt))
            out["hip"].append(hi_p.reshape(bp, g_cnt, p_cnt))
            out["hrs"].append(hr_s.reshape(bs, g_cnt, p_cnt))
            out["his"].append(hi_s.reshape(bs, g_cnt, p_cnt))
        else:
            c, cv_p = _conv_in_core(xb, prm["conv_w_in"], conv0_prompt[i], prm["conv_w"], i, bp, tp,
                                    _pick_tile(tp, 512), _pick_tile(d, 1024))
            z = _mm(xb[sample], prm["conv_w_in"], i, 0, 3 * d, ms, _pick_tile(d, 2048))
            c_s, cv_s = _conv_core(z, conv0_sample[i], prm["conv_w"], i, bs, ts, ts, _pick_tile(d, 512))
            x32, xb = _mm_res_ln([_fill_rest(c, c_s, mp)], prm["conv_w_out"], i, x32, row(g[1]), row(b[1]),
                                 alpha, tm)
            out["cvp"].append(cv_p)
            out["cvs"].append(cv_s)
        if layer < depth - 1:
            x32, xb, wgu_b, wd_b = _ffn(x32, xb, wgu_b, wd_b, 0, row(g[2]), row(b[2]), alpha, tm, tf,
                                        convert_next=(prm["ffn1_wgu"], prm["ffn1_wd"], layer + 1))
        else:
            y_prompt, y_rest = _ffn(x32, xb, wgu_b, wd_b, 0, row(g[2]), row(b[2]), alpha, tm, tf, mp=mp)
    out = {n: jnp.stack(a) for n, a in out.items()}
    n_hyb = k_stack.shape[0]
    out["kp"] = k_stack.reshape(n_hyb, bp, tp, nh, hd)
    out["vp"] = v_stack.reshape(n_hyb, bp, tp, nh, hd)
    return y_prompt.reshape(bp, tp, d), y_rest[:ms].reshape(bs, ts, d), out


def kernel(x_prompt, x_sample, cache_k, cache_v, state_s5_re, state_s5_im, state_conv, page_table, ffn1_wgu, ffn1_wd, ffn2_wgu, ffn2_wd, ln_g, ln_b, hyb_w_in, hyb_w_out, s5_a_re, s5_a_im, s5_log_dt, s5_b_re, s5_b_im, s5_c_re, s5_c_im, s5_d, s5_glu_w, s5_glu_b, conv_w_in, conv_w, conv_w_out):
    n_hyb, _, _, nh, hd = cache_k.shape
    bp, t_prompt, d = x_prompt.shape
    db, t_sample, _ = x_sample.shape
    n_conv, cw = conv_w.shape[:2]
    g_cnt, p_cnt = s5_a_re.shape[1:]
    assert db == SUBLANES, "the sample scan lays the decode batch along the vreg sublanes"
    assert (s5_b_re.shape[-1] * S5_LANE_GROUPS) == 128

    prm = dict(
        ffn1_wgu=ffn1_wgu, ffn1_wd=ffn1_wd, ffn2_wgu=ffn2_wgu, ffn2_wd=ffn2_wd,
        ln_g=ln_g, ln_b=ln_b, hyb_w_in=hyb_w_in.astype(BF16), hyb_w_out=hyb_w_out.astype(BF16),
        s5_a_re=s5_a_re, s5_glu_w=s5_glu_w.astype(BF16), s5_glu_b=s5_glu_b,
        conv_w_in=conv_w_in.astype(BF16), conv_w=conv_w.astype(F32), conv_w_out=conv_w_out.astype(BF16),
    )

    def s5_prms(seg_len):
        out = []
        for i in range(n_hyb):
            sp = _s5_discretise(s5_a_re[i], s5_a_im[i], s5_log_dt[i], s5_b_re[i], s5_b_im[i], s5_c_re[i],
                                s5_c_im[i], s5_d[i], seg_len)
            sp["seg_len"] = seg_len
            out.append(sp)
        return out

    prompt_seg = _pick_tile(t_prompt // SUBLANES, 32)
    kv = dict(nh=nh, hd=hd, cache_k=cache_k, cache_v=cache_v, page_table=page_table)
    conv_zero = jnp.zeros((n_conv, bp, cw - 1, d), x_prompt.dtype)
    y_p, y_s, o = _run_trunk(x_prompt, x_sample, prm, s5_prms(prompt_seg), s5_prms(t_sample), kv, state_s5_re,
                             state_s5_im, conv_zero, state_conv)
    return (y_p, y_s, o["kp"], o["vp"], o["hrp"], o["hip"], o["cvp"], o["ks"], o["vs"], o["hrs"], o["his"], o["cvs"])
```

```python
import functools

import jax
import jax.numpy as jnp
from jax import lax
from jax.experimental import pallas as pl
from jax.experimental.pallas import tpu as pltpu

MOBA_BLOCK = 256
MOBA_TOPK = 3
LN_EPS = 1e-5
LANES = 128
SUBLANES = 8
S5_LANE_GROUPS = 8
S5_SCAN_LANES = 1024
S5_SCAN_UNROLL = 4
CAST_BLOCK_BYTES = 4 * 1024 * 1024
BF16_ROWS = 16
MAX_ROW_TILE = 640
VMEM_LIMIT_BYTES = 56 * 1024 * 1024

F32 = jnp.float32
BF16 = jnp.bfloat16
NEG_INF = float("-inf")


def _cparams(semantics):
    return pltpu.CompilerParams(dimension_semantics=semantics, vmem_limit_bytes=VMEM_LIMIT_BYTES)


def _layer_norm(y, g, b):
    mu = jnp.mean(y, axis=-1, keepdims=True)
    d = y - mu
    var = jnp.mean(d * d, axis=-1, keepdims=True)
    return d * lax.rsqrt(var + LN_EPS) * g + b


def _convert_slab(s, n_steps, src_ref, layer, dst_ref, inbuf, outbuf, sem_in, sem_out, rows):
    last_slab = src_ref.shape[1] // rows - 1

    def row0(k):
        return pl.multiple_of(jnp.minimum(k, last_slab) * rows, rows)

    def fetch(k, slot):
        return pltpu.make_async_copy(src_ref.at[layer, pl.ds(row0(k), rows), :], inbuf.at[slot], sem_in.at[slot])

    def send(k):
        return pltpu.make_async_copy(outbuf, dst_ref.at[0, pl.ds(row0(k), rows), :], sem_out)

    slot = lax.rem(s, 2)

    @pl.when(s == 0)
    def _():
        fetch(0, 0).start()
        outbuf[...] = jnp.zeros_like(outbuf)
        send(0).start()

    fetch(s, slot).wait()
    fetch(s + 1, 1 - slot).start()
    send(s).wait()
    outbuf[...] = inbuf[slot].astype(BF16)
    send(s).start()

    @pl.when(s == n_steps - 1)
    def _():
        fetch(s + 1, 1 - slot).wait()
        send(s).wait()


def _cast_kernel(w_ref, o_ref):
    o_ref[...] = w_ref[...].astype(o_ref.dtype)


def _cast_layer(w, layer):
    _, r, c = w.shape
    rows = max(BF16_ROWS, CAST_BLOCK_BYTES // (4 * c))
    br = _pick_tile(r, 1 << (rows.bit_length() - 1))
    return pl.pallas_call(
        _cast_kernel,
        grid=(r // br,),
        in_specs=[pl.BlockSpec((None, br, c), lambda i: (layer, i, 0))],
        out_specs=pl.BlockSpec((None, br, c), lambda i: (0, i, 0)),
        out_shape=jax.ShapeDtypeStruct((1, r, c), BF16),
        compiler_params=_cparams(("parallel",)),
        name="cast_bf16",
    )(w)


def _convert_rows(n_rows, n_steps):
    return next(r for r in range(BF16_ROWS, n_rows + 1, BF16_ROWS) if n_rows % r == 0 and n_rows // r <= n_steps)


def _ffn_kernel(*refs, alpha, nf, rest, converting):
    x32_ref, xb_ref, wg_ref, wu_ref, wd_ref, g_ref, b_ref = refs[:7]
    if converting:
        (ngu_src, nd_src, o32_ref, aux_ref, ngu_dst, nd_dst, gu_in, gu_out, d_in, d_out, sem_in, sem_out) = refs[7:]
    else:
        o32_ref, aux_ref = refs[7:]
    i, j = pl.program_id(0), pl.program_id(1)
    if converting:
        step, n_steps = i * nf + j, pl.num_programs(0) * nf
        _convert_slab(step, n_steps, ngu_src, converting[0], ngu_dst, gu_in, gu_out, sem_in.at[0], sem_out.at[0],
                      gu_in.shape[1])
        _convert_slab(step, n_steps, nd_src, converting[0], nd_dst, d_in, d_out, sem_in.at[1], sem_out.at[1],
                      d_in.shape[1])

    @pl.when(j == 0)
    def _():
        o32_ref[...] = jnp.zeros_like(o32_ref)

    xb = xb_ref[...]
    gate = jnp.dot(xb, wg_ref[...], preferred_element_type=F32)
    up = jnp.dot(xb, wu_ref[...], preferred_element_type=F32)
    h = (gate * jax.nn.sigmoid(gate)) * up
    o32_ref[...] += jnp.dot(h.astype(BF16), wd_ref[...], preferred_element_type=F32)

    @pl.when(j == nf - 1)
    def _():
        z = _layer_norm(alpha * x32_ref[...] + 0.5 * o32_ref[...], g_ref[...], b_ref[...])
        o32_ref[...] = z
        if rest is None:
            aux_ref[...] = z.astype(BF16)
        else:
            @pl.when(i == pl.num_programs(0) - 1)
            def _():
                aux_ref[...] = z[rest[0]:rest[0] + rest[1], :]


def _ffn(x32, xb, wgu, wd, layer, g, b, alpha, tm, tf, mp=None, convert_next=None):
    m, d = x32.shape
    f = wd.shape[1]
    nf = f // tf
    n_steps = (m // tm) * nf
    if mp is None:
        rest = None
        out_specs = [pl.BlockSpec((tm, d), lambda i, j: (i, 0)), pl.BlockSpec((tm, d), lambda i, j: (i, 0))]
        out_shape = [jax.ShapeDtypeStruct((m, d), F32), jax.ShapeDtypeStruct((m, d), BF16)]
    else:
        rest = _rest_rows(m, mp, tm)
        out_specs = [pl.BlockSpec((tm, d), lambda i, j: (i, 0)), pl.BlockSpec((rest[1], d), lambda i, j: (0, 0))]
        out_shape = [jax.ShapeDtypeStruct((mp, d), F32), jax.ShapeDtypeStruct((rest[1], d), F32)]
    in_specs = [
        pl.BlockSpec((tm, d), lambda i, j: (i, 0)),
        pl.BlockSpec((tm, d), lambda i, j: (i, 0)),
        pl.BlockSpec((None, d, tf), lambda i, j: (layer, 0, j)),
        pl.BlockSpec((None, d, tf), lambda i, j: (layer, 0, j + nf)),
        pl.BlockSpec((None, tf, d), lambda i, j: (layer, j, 0)),
        pl.BlockSpec((1, d), lambda i, j: (0, 0)),
        pl.BlockSpec((1, d), lambda i, j: (0, 0)),
    ]
    operands = [x32, xb, wgu, wgu, wd, g, b]
    scratch, converting = [], None
    if convert_next is not None:
        src_gu, src_d, next_layer = convert_next
        converting = (next_layer,)
        gu_rows, d_rows = _convert_rows(d, n_steps), _convert_rows(f, n_steps)
        any_spec = pl.BlockSpec(memory_space=pl.ANY)
        in_specs += [any_spec, any_spec]
        operands += [src_gu, src_d]
        out_specs += [any_spec, any_spec]
        out_shape += [jax.ShapeDtypeStruct((1, d, 2 * f), BF16), jax.ShapeDtypeStruct((1, f, d), BF16)]
        scratch = [pltpu.VMEM((2, gu_rows, 2 * f), F32), pltpu.VMEM((gu_rows, 2 * f), BF16),
                   pltpu.VMEM((2, d_rows, d), F32), pltpu.VMEM((d_rows, d), BF16),
                   pltpu.SemaphoreType.DMA((2, 2)), pltpu.SemaphoreType.DMA((2,))]
    return pl.pallas_call(
        functools.partial(_ffn_kernel, alpha=alpha, nf=nf, rest=rest, converting=converting),
        grid=(m // tm, nf),
        in_specs=in_specs,
        out_specs=out_specs,
        out_shape=out_shape,
        scratch_shapes=scratch,
        compiler_params=_cparams(("arbitrary", "arbitrary") if converting else ("parallel", "arbitrary")),
        name="ffn_ln",
    )(*operands)


def _mm_kernel(x_ref, w_ref, o_ref):
    o_ref[...] = jnp.dot(x_ref[...], w_ref[...], preferred_element_type=F32)


def _mm(xb, w, layer, col0, n, tm, tn):
    m, k = xb.shape
    assert col0 % tn == 0 and n % tn == 0
    jb0 = col0 // tn
    return pl.pallas_call(
        _mm_kernel,
        grid=(m // tm, n // tn),
        in_specs=[
            pl.BlockSpec((tm, k), lambda i, j: (i, 0)),
            pl.BlockSpec((None, k, tn), lambda i, j: (layer, 0, jb0 + j)),
        ],
        out_specs=pl.BlockSpec((tm, tn), lambda i, j: (i, j)),
        out_shape=jax.ShapeDtypeStruct((m, n), F32),
        compiler_params=_cparams(("parallel", "arbitrary")),
        name="proj",
    )(xb, w)


def _rest_rows(m_all, mp, tm):
    first = mp - (m_all // tm - 1) * tm
    assert m_all % tm == 0 and 0 <= first and first + (m_all - mp) == tm
    return first, m_all - mp


def _hyb_in_kernel(x_ref, xr_ref, w_ref, *refs, n_alias):
    qp_ref, qr_ref, kp_ref, kr_ref, vp_ref, vr_ref, up_ref, ur_ref = refs[n_alias:]
    j, i = pl.program_id(0), pl.program_id(1)
    last = i == pl.num_programs(1) - 1
    w = w_ref[...]
    width = w.shape[1] // 2
    y = jnp.dot(x_ref[...], w, preferred_element_type=F32)

    def emit(lo_ref, hi_ref, lo_rest_ref, hi_rest_ref):
        lo_ref[...] = y[:, :width]
        hi_ref[...] = y[:, width:]

        @pl.when(last)
        def _():
            yr = jnp.dot(xr_ref[...], w, preferred_element_type=F32)
            lo_rest_ref[...] = yr[:, :width]
            hi_rest_ref[...] = yr[:, width:]

    @pl.when(j == 0)
    def _():
        emit(qp_ref, kp_ref, qr_ref, kr_ref)

    @pl.when(j == 1)
    def _():
        emit(vp_ref, up_ref, vr_ref, ur_ref)


def _hyb_in(xb, w, layer, mp, k_stack=None, v_stack=None):
    m, k = xb.shape
    n_layers = w.shape[0]
    width = w.shape[2] // 4
    n_rest = m - mp
    assert mp % n_rest == 0
    tm = _pick_tile(mp, 512)
    aliased = [] if k_stack is None else [k_stack, v_stack]
    n_tiles = mp // tm
    tile = (lambda j, i: i * (1 - j) + (n_tiles - 1) * j, lambda j, i: i * j)
    prompt_spec = [pl.BlockSpec((tm, width), lambda j, i, p=p: (tile[p](j, i), 0)) for p in (0, 1)]
    stack_spec = [pl.BlockSpec((None, tm, width), lambda j, i, p=p: (layer, tile[p](j, i), 0)) for p in (0, 1)]
    rest_spec = pl.BlockSpec((n_rest, width), lambda j, i: (0, 0))
    prompt_shape = jax.ShapeDtypeStruct((mp, width), F32)
    stack_shape = jax.ShapeDtypeStruct((n_layers, mp, width), F32)
    rest_shape = jax.ShapeDtypeStruct((n_rest, width), F32)
    qp, qr, kp, kr, vp, vr, up, ur = pl.pallas_call(
        functools.partial(_hyb_in_kernel, n_alias=len(aliased)),
        grid=(2, n_tiles),
        in_specs=[
            pl.BlockSpec((tm, k), lambda j, i: (i, 0)),
            pl.BlockSpec((n_rest, k), lambda j, i: (mp // n_rest, 0)),
            pl.BlockSpec((None, k, 2 * width), lambda j, i: (layer, 0, j)),
        ] + [pl.BlockSpec(memory_space=pl.ANY)] * len(aliased),
        out_specs=[prompt_spec[0], rest_spec, stack_spec[0], rest_spec, stack_spec[1], rest_spec, prompt_spec[1],
                   rest_spec],
        out_shape=[prompt_shape, rest_shape, stack_shape, rest_shape, stack_shape, rest_shape, prompt_shape,
                   rest_shape],
        input_output_aliases={3: 2, 4: 4} if aliased else {},
        compiler_params=_cparams(("arbitrary", "arbitrary")),
        name="hyb_in_proj",
    )(xb, xb, w, *aliased)
    return (qp, qr), (kp, kr), (vp, vr), (up, ur)


def _mm_res_ln_kernel(*refs, nparts, gated, alpha):
    a_refs = refs[:nparts]
    w_refs = refs[nparts:2 * nparts]
    rest = refs[2 * nparts:]
    if gated:
        gw_ref, gb_ref = rest[:2]
        rest = rest[2:]
    x32_ref, g_ref, b_ref, o32_ref, ob_ref = rest
    y = alpha * x32_ref[...]
    for r, (a_ref, w_ref) in enumerate(zip(a_refs, w_refs)):
        a = a_ref[...]
        if gated and r == nparts - 1:
            gate = jnp.dot(a.astype(BF16), gw_ref[...], preferred_element_type=F32) + gb_ref[...]
            a = a * jax.nn.sigmoid(gate)
        y = y + jnp.dot(a.astype(BF16), w_ref[...], preferred_element_type=F32)
    z = _layer_norm(y, g_ref[...], b_ref[...])
    o32_ref[...] = z
    ob_ref[...] = z.astype(BF16)


def _mm_res_ln(parts, w, layer, x32, g, b, alpha, tm, glu=None):
    m, d = x32.shape
    kp = parts[0].shape[1]
    assert all(p.shape[1] == kp for p in parts)
    once = dict(pipeline_mode=pl.Buffered(1))
    in_specs = [pl.BlockSpec((tm, kp), lambda i: (i, 0)) for _ in parts]
    in_specs += [pl.BlockSpec((None, kp, d), lambda i, r=r: (layer, r, 0), **once) for r in range(len(parts))]
    operands = [*parts, *([w] * len(parts))]
    if glu is not None:
        in_specs += [pl.BlockSpec((None, kp, kp), lambda i: (layer, 0, 0), **once), pl.BlockSpec((1, kp), lambda i: (0, 0))]
        operands += list(glu)
    in_specs += [
        pl.BlockSpec((tm, d), lambda i: (i, 0)),
        pl.BlockSpec((1, d), lambda i: (0, 0)),
        pl.BlockSpec((1, d), lambda i: (0, 0)),
    ]
    return pl.pallas_call(
        functools.partial(_mm_res_ln_kernel, nparts=len(parts), gated=glu is not None, alpha=alpha),
        grid=(m // tm,),
        in_specs=in_specs,
        out_specs=[pl.BlockSpec((tm, d), lambda i: (i, 0)), pl.BlockSpec((tm, d), lambda i: (i, 0))],
        out_shape=[jax.ShapeDtypeStruct((m, d), F32), jax.ShapeDtypeStruct((m, d), BF16)],
        compiler_params=_cparams(("parallel",)),
        name="out_proj_ln",
    )(*operands, x32, g, b)


CONV_PAD = 8
CONV_COL_SLAB = 256


def _conv_tile(project, st_ref, w_ref, c_ref, ns_ref, hbuf, *, tt, nt, cw):
    t = pl.program_id(2)
    lo = CONV_PAD - (cw - 1)
    tn = c_ref.shape[1]
    slab = min(tn, CONV_COL_SLAB)

    @pl.when(t == 0)
    def _():
        hbuf[lo:CONV_PAD, :] = st_ref[...]

    for c0 in range(0, tn, slab):
        cols = slice(c0, c0 + slab)
        bg, cg, v = project(cols)
        hbuf[CONV_PAD:CONV_PAD + tt, cols] = cg * v
        w = w_ref[:, cols]
        conv = w[0:1, :] * hbuf[lo:lo + tt, cols]
        for j in range(1, cw):
            conv = conv + w[j:j + 1, :] * hbuf[lo + j:lo + j + tt, cols]
        c_ref[:, cols] = (bg * conv).astype(c_ref.dtype)
    tail = hbuf[lo + tt:CONV_PAD + tt, :]

    @pl.when(t == nt - 1)
    def _():
        ns_ref[...] = tail

    hbuf[lo:CONV_PAD, :] = tail


def _conv_kernel(bg_ref, cg_ref, v_ref, st_ref, w_ref, c_ref, ns_ref, hbuf, **kw):
    _conv_tile(lambda cols: (bg_ref[:, cols], cg_ref[:, cols], v_ref[:, cols]), st_ref, w_ref, c_ref, ns_ref, hbuf,
               **kw)


def _conv_in_kernel(x_ref, wb_ref, wc_ref, wv_ref, st_ref, w_ref, c_ref, ns_ref, hbuf, **kw):
    def project(cols):
        x = x_ref[...]
        return tuple(jnp.dot(x, w[:, cols], preferred_element_type=F32) for w in (wb_ref, wc_ref, wv_ref))

    _conv_tile(project, st_ref, w_ref, c_ref, ns_ref, hbuf, **kw)


def _conv_core(z, state, conv_w, layer, bsz, t_len, tt, tn):
    m, d3 = z.shape
    d = d3 // 3
    cw = conv_w.shape[1]
    nt, nj = t_len // tt, d // tn
    return pl.pallas_call(
        functools.partial(_conv_kernel, tt=tt, nt=nt, cw=cw),
        grid=(bsz, nj, nt),
        in_specs=[
            pl.BlockSpec((tt, tn), lambda b, j, t: (b * nt + t, j)),
            pl.BlockSpec((tt, tn), lambda b, j, t: (b * nt + t, j + nj)),
            pl.BlockSpec((tt, tn), lambda b, j, t: (b * nt + t, j + 2 * nj)),
            pl.BlockSpec((None, cw - 1, tn), lambda b, j, t: (b, 0, j)),
            pl.BlockSpec((None, cw, tn), lambda b, j, t: (layer, 0, j)),
        ],
        out_specs=[
            pl.BlockSpec((tt, tn), lambda b, j, t: (b * nt + t, j)),
            pl.BlockSpec((None, cw - 1, tn), lambda b, j, t: (b, 0, j)),
        ],
        out_shape=[jax.ShapeDtypeStruct((m, d), BF16), jax.ShapeDtypeStruct((bsz, cw - 1, d), F32)],
        scratch_shapes=[pltpu.VMEM((CONV_PAD + tt, tn), F32)],
        compiler_params=_cparams(("parallel", "parallel", "arbitrary")),
        name="conv_core",
    )(z, z, z, state, conv_w)


def _conv_in_core(xb, w_in, state, conv_w, layer, bsz, t_len, tt, tn):
    m, d = xb.shape
    cw = conv_w.shape[1]
    nt, nj = t_len // tt, d // tn
    return pl.pallas_call(
        functools.partial(_conv_in_kernel, tt=tt, nt=nt, cw=cw),
        grid=(bsz, nj, nt),
        in_specs=[
            pl.BlockSpec((tt, d), lambda b, j, t: (b * nt + t, 0)),
            pl.BlockSpec((None, d, tn), lambda b, j, t: (layer, 0, j)),
            pl.BlockSpec((None, d, tn), lambda b, j, t: (layer, 0, j + nj)),
            pl.BlockSpec((None, d, tn), lambda b, j, t: (layer, 0, j + 2 * nj)),
            pl.BlockSpec((None, cw - 1, tn), lambda b, j, t: (b, 0, j)),
            pl.BlockSpec((None, cw, tn), lambda b, j, t: (layer, 0, j)),
        ],
        out_specs=[
            pl.BlockSpec((tt, tn), lambda b, j, t: (b * nt + t, j)),
            pl.BlockSpec((None, cw - 1, tn), lambda b, j, t: (b, 0, j)),
        ],
        out_shape=[jax.ShapeDtypeStruct((m, d), BF16), jax.ShapeDtypeStruct((bsz, cw - 1, d), F32)],
        scratch_shapes=[pltpu.VMEM((CONV_PAD + tt, tn), F32)],
        compiler_params=_cparams(("parallel", "parallel", "arbitrary")),
        name="conv_in_core",
    )(xb, w_in, w_in, w_in, state, conv_w)


def _moba_select_rows(gate_t, n_valid):
    nb, nq = gate_t.shape
    sub = lax.broadcasted_iota(jnp.int32, (nb, nq), 0)
    valid = sub < n_valid
    gate_t = jnp.where(valid, gate_t, NEG_INF)
    beaten_by = jnp.zeros((nb, nq), jnp.int32)
    for jp in range(nb):
        row = gate_t[jp:jp + 1, :]
        beats = (row > gate_t) | ((row == gate_t) & (sub > jp))
        beaten_by = beaten_by + jnp.where(beats, 1, 0)
    return valid & (beaten_by < MOBA_TOPK)


def _moba_prompt_kernel(q_ref, k_ref, v_ref, o_ref, kb_scr, vt_scr, km_scr, *, nb, scale):
    blk = MOBA_BLOCK
    kb_scr[...] = k_ref[...].astype(BF16)
    for j in range(nb):
        rows = slice(j * blk, (j + 1) * blk)
        km_scr[j:j + 1, :] = jnp.mean(k_ref[rows, :], axis=0, keepdims=True)
        vt_scr[:, rows] = v_ref[rows, :].T.astype(BF16)
    causal = lax.broadcasted_iota(jnp.int32, (blk, blk), 0) <= lax.broadcasted_iota(jnp.int32, (blk, blk), 1)
    causal_bias = jnp.where(causal, 0.0, NEG_INF).astype(F32)

    scored, probs = [], []
    for qi in range(nb):
        n_keys = (qi + 1) * blk
        q = q_ref[qi * blk:(qi + 1) * blk, :]
        s_all = lax.dot_general(kb_scr[0:n_keys, :], q.astype(BF16), (((1,), (1,)), ((), ())),
                                preferred_element_type=F32) * scale
        sel_bias = None
        if qi > 0:
            gate_t = lax.dot_general(km_scr[...], q, (((1,), (1,)), ((), ())),
                                     precision=lax.Precision.HIGHEST, preferred_element_type=F32)
            sel_bias = jnp.where(_moba_select_rows(gate_t, qi), 0.0, NEG_INF).astype(F32)
        scored.append((s_all, sel_bias))
    for qi, (s_all, sel_bias) in enumerate(scored):
        n_keys = (qi + 1) * blk
        s_blocks = [s_all[j * blk:(j + 1) * blk, :] + sel_bias[j:j + 1, :] for j in range(qi)]
        s_blocks.append(s_all[qi * blk:n_keys, :] + causal_bias)
        top = s_blocks[0]
        for sb in s_blocks[1:]:
            top = jnp.maximum(top, sb)
        m = jnp.max(top, axis=0, keepdims=True)
        p_blocks = [jnp.exp(sb - m) for sb in s_blocks]
        tot = p_blocks[0]
        for pb in p_blocks[1:]:
            tot = tot + pb
        l = jnp.sum(tot, axis=0, keepdims=True)
        probs.append((jnp.concatenate([pb.astype(BF16) for pb in p_blocks], axis=0), l))
    for qi, (p_all, l) in enumerate(probs):
        acc = jnp.dot(vt_scr[:, 0:(qi + 1) * blk], p_all, preferred_element_type=F32)
        o_ref[qi * blk:(qi + 1) * blk, :] = (acc / l).T.astype(o_ref.dtype)


def _moba_prompt(q, k_stack, v_stack, layer, bsz, t_len, nh, hd, m_out):
    m = m_out
    assert t_len % MOBA_BLOCK == 0
    nb = t_len // MOBA_BLOCK
    seq_spec = pl.BlockSpec((t_len, hd), lambda b, h: (b, h))
    kv_spec = pl.BlockSpec((None, t_len, hd), lambda b, h: (layer, b, h))
    return pl.pallas_call(
        functools.partial(_moba_prompt_kernel, nb=nb, scale=hd ** -0.5),
        grid=(bsz, nh),
        in_specs=[seq_spec, kv_spec, kv_spec],
        out_specs=seq_spec,
        out_shape=jax.ShapeDtypeStruct((m, nh * hd), BF16),
        scratch_shapes=[pltpu.VMEM((t_len, hd), BF16), pltpu.VMEM((hd, t_len), BF16),
                        pltpu.VMEM((nb, hd), F32)],
        compiler_params=_cparams(("parallel", "parallel")),
        name="moba_prompt",
    )(q, k_stack, v_stack)


def _moba_select_grouped(gate, n_valid_blocks, group):
    rows, nl = gate.shape
    lane = lax.broadcasted_iota(jnp.int32, (rows, nl), 1)
    valid = lane < n_valid_blocks * group
    gate = jnp.where(valid, gate, NEG_INF)
    beaten_by = jnp.zeros((rows, nl), jnp.int32)
    for jp in range(0, nl, group):
        col = gate[:, jp:jp + 1]
        beats = (col > gate) | ((col == gate) & (lane >= jp + group))
        beaten_by = beaten_by + jnp.where(beats, 1, 0)
    return valid & (beaten_by < MOBA_TOPK)


def _fold_lanes(x, op):
    tiles = [x[:, i:i + LANES] for i in range(0, x.shape[1], LANES)]
    out = tiles[0]
    for t in tiles[1:]:
        out = op(out, t)
    return out


def _sample_attn_kernel(pt_ref, q_ref, kn_ref, vn_ref, hbias_ref, obias_ref, *refs, nh, n_past_blocks,
                        pages_per_step, pages_per_block, scale):
    del pt_ref
    kp_refs = refs[:pages_per_step]
    vp_refs = refs[pages_per_step:2 * pages_per_step]
    o_ref, gate_scr, m_scr, l_scr, acc_scr = refs[2 * pages_per_step:]
    step = pl.program_id(1)
    n_steps = pl.num_programs(1)
    rows, hd = q_ref.shape
    n_pages = m_scr.shape[1]
    q = q_ref[...]
    qb = q.astype(BF16)
    q3 = q.reshape(rows // nh, nh, hd)
    lane = lax.broadcasted_iota(jnp.int32, (rows, n_pages), 1)

    def put_column(scr, pg, col):
        scr[...] = jnp.where(lane == pg, col, scr[...])

    scores, probs = [], []
    for r in range(pages_per_step):
        k = kp_refs[r][...]
        s = lax.dot_general(qb, k.astype(BF16), (((1,), (1,)), ((), ())), preferred_element_type=F32)
        scores.append(s * scale + hbias_ref[...])
    for blk0 in range(0, pages_per_step, pages_per_block):
        ksum = jnp.zeros((nh, hd), F32)
        for r in range(blk0, blk0 + pages_per_block):
            k = kp_refs[r][...]
            ksum = ksum + jnp.sum(k.reshape(k.shape[0] // nh, nh, hd), axis=0)
        kmean = ksum / MOBA_BLOCK
        gcol = jnp.sum(q3 * kmean[None, :, :], axis=2, keepdims=True).reshape(rows, 1)
        for r in range(blk0, blk0 + pages_per_block):
            put_column(gate_scr, step * pages_per_step + r, gcol)
    for r, s in enumerate(scores):
        pg = step * pages_per_step + r
        m = jnp.max(_fold_lanes(s, jnp.maximum), axis=1, keepdims=True)
        p = jnp.exp(s - m)
        put_column(m_scr, pg, m)
        put_column(l_scr, pg, jnp.sum(_fold_lanes(p, jnp.add), axis=1, keepdims=True))
        probs.append(p.astype(BF16))
    for r, p in enumerate(probs):
        acc_scr[step * pages_per_step + r] = jnp.dot(p, vp_refs[r][...].astype(BF16), preferred_element_type=F32)

    @pl.when(step == n_steps - 1)
    def _():
        keep = _moba_select_grouped(gate_scr[...], n_past_blocks, pages_per_block)
        m_pages = m_scr[...] + jnp.where(keep, 0.0, NEG_INF)
        s = lax.dot_general(qb, kn_ref[...].astype(BF16), (((1,), (1,)), ((), ())), preferred_element_type=F32)
        s = s * scale + obias_ref[...]
        m_own = jnp.max(s, axis=1, keepdims=True)
        m_tot = jnp.maximum(m_own, jnp.max(m_pages, axis=1, keepdims=True))
        p = jnp.exp(s - m_tot)
        w = jnp.exp(m_pages - m_tot)
        l = jnp.sum(p, axis=1, keepdims=True) + jnp.sum(w * l_scr[...], axis=1, keepdims=True)
        acc = jnp.dot(p.astype(BF16), vn_ref[...].astype(BF16), preferred_element_type=F32)
        for pg in range(n_pages):
            acc = acc + w[:, pg:pg + 1] * acc_scr[pg]
        o_ref[...] = (acc / l).astype(o_ref.dtype)


def _sample_attn(q, k_new, v_new, cache_k, cache_v, layer, page_table, db, t_len, nh, hd, pages_per_step=16):
    n_hyb, n_pool, page, _, _ = cache_k.shape
    n_pages = page_table.shape[1]
    past_len = n_pages * page
    assert past_len % MOBA_BLOCK == 0 and t_len <= MOBA_BLOCK and MOBA_BLOCK % page == 0
    assert nh & (nh - 1) == 0, "head matching uses bit masks"
    ppb = MOBA_BLOCK // page
    pages_per_step = min(pages_per_step, n_pages)
    assert pages_per_step % ppb == 0 and n_pages % pages_per_step == 0
    rows = t_len * nh
    ck = cache_k.reshape(n_hyb, n_pool, page * nh, hd)
    cv = cache_v.reshape(n_hyb, n_pool, page * nh, hd)
    r = jnp.arange(rows, dtype=jnp.int32)[:, None]
    c = jnp.arange(page * nh, dtype=jnp.int32)[None, :]
    same_head = (r & (nh - 1)) == (c & (nh - 1))
    head_bias = jnp.where(same_head, 0.0, NEG_INF).astype(F32)
    own_bias = jnp.where(same_head[:, :rows] & (c[:, :rows] // nh <= r // nh), 0.0, NEG_INF).astype(F32)

    def page_spec(i):
        return pl.BlockSpec((None, None, page * nh, hd),
                            lambda b, s, pt: (layer, pt[b, s * pages_per_step + i], 0, 0))

    def row_spec(nrows):
        return pl.BlockSpec((None, nrows, hd), lambda b, s, pt: (b, 0, 0))

    full = lambda a: pl.BlockSpec(a.shape, lambda b, s, pt: (0,) * a.ndim)
    out = pl.pallas_call(
        functools.partial(_sample_attn_kernel, nh=nh, n_past_blocks=past_len // MOBA_BLOCK,
                          pages_per_step=pages_per_step, pages_per_block=ppb, scale=hd ** -0.5),
        grid_spec=pltpu.PrefetchScalarGridSpec(
            num_scalar_prefetch=1,
            grid=(db, n_pages // pages_per_step),
            in_specs=[row_spec(rows), row_spec(rows), row_spec(rows), full(head_bias), full(own_bias)]
            + [page_spec(i) for i in range(pages_per_step)] * 2,
            out_specs=row_spec(rows),
            scratch_shapes=[pltpu.VMEM((rows, n_pages), F32), pltpu.VMEM((rows, n_pages), F32),
                            pltpu.VMEM((rows, n_pages), F32), pltpu.VMEM((n_pages, rows, hd), F32)],
        ),
        out_shape=jax.ShapeDtypeStruct((db, rows, hd), BF16),
        compiler_params=_cparams(("parallel", "arbitrary")),
        name="sample_attn",
    )(page_table, q.reshape(db, rows, hd), k_new.reshape(db, rows, hd), v_new.reshape(db, rows, hd),
      head_bias, own_bias, *([ck] * pages_per_step), *([cv] * pages_per_step))
    return out.reshape(db * t_len, nh * hd)


def _s5_kernel(u_ref, perm_ref, unperm_ref, bre_ref, bim_ref, cre_ref, cimn_ref, lre_ref, lim_ref, llre_ref, llim_ref,
               d_ref, h0re_ref, h0im_ref, y_ref, htre_ref, htim_ref, hr_scr, hi_scr, cre_scr, cim_scr, yc_scr,
               *, seg_len, chained, n_chunks):
    c = pl.program_id(1)
    nseg = SUBLANES
    nstate = hr_scr.shape[1]
    n_kb = bre_ref.shape[0]
    ku = bre_ref.shape[1]
    ks = bre_ref.shape[2]

    up = jnp.dot(perm_ref[...], u_ref[...].astype(BF16), preferred_element_type=F32).astype(BF16)
    for kb in range(n_kb):
        ukb = up[:, kb * ku:(kb + 1) * ku]
        hr_scr[:, kb * ks:(kb + 1) * ks] = jnp.dot(ukb, bre_ref[kb], preferred_element_type=F32)
        hi_scr[:, kb * ks:(kb + 1) * ks] = jnp.dot(ukb, bim_ref[kb], preferred_element_type=F32)

    if chained:
        @pl.when(c == 0)
        def _():
            cre_scr[...] = jnp.zeros_like(cre_scr)
            cim_scr[...] = jnp.zeros_like(cim_scr)
            cre_scr[0:1, :] = h0re_ref[...]
            cim_scr[0:1, :] = h0im_ref[...]

    for lc in range(nstate // S5_SCAN_LANES):
        cols = slice(lc * S5_SCAN_LANES, (lc + 1) * S5_SCAN_LANES)
        lr = jnp.broadcast_to(lre_ref[:, cols], (nseg, S5_SCAN_LANES))
        li = jnp.broadcast_to(lim_ref[:, cols], (nseg, S5_SCAN_LANES))

        def advance(i, hr, hi):
            r = pl.multiple_of(i * nseg, nseg)
            nr = lr * hr - li * hi + hr_scr[pl.ds(r, nseg), cols]
            ni = lr * hi + li * hr + hi_scr[pl.ds(r, nseg), cols]
            return r, nr, ni

        def scan_only(i, carry):
            _, nr, ni = advance(i, *carry)
            return nr, ni

        def scan_store(i, carry):
            r, nr, ni = advance(i, *carry)
            hr_scr[pl.ds(r, nseg), cols] = nr
            hi_scr[pl.ds(r, nseg), cols] = ni
            return nr, ni

        if chained:
            er, ei = lax.fori_loop(0, seg_len, scan_only, (cre_scr[:, cols], cim_scr[:, cols]),
                                   unroll=S5_SCAN_UNROLL)
            llr, lli = llre_ref[:, cols], llim_ref[:, cols]
            fr, fi = er[0:1, :], ei[0:1, :]
            for j in range(1, nseg):
                cre_scr[j:j + 1, cols] = fr
                cim_scr[j:j + 1, cols] = fi
                if j < nseg - 1:
                    fr, fi = (er[j:j + 1, :] + (llr * fr - lli * fi), ei[j:j + 1, :] + (llr * fi + lli * fr))
            init = (cre_scr[:, cols], cim_scr[:, cols])
        else:
            init = (h0re_ref[:, cols], h0im_ref[:, cols])

        hr, hi = lax.fori_loop(0, seg_len, scan_store, init, unroll=S5_SCAN_UNROLL)

        if chained:
            last_r, last_i = hr[nseg - 1:nseg, :], hi[nseg - 1:nseg, :]
            cre_scr[:, cols] = jnp.zeros((nseg, S5_SCAN_LANES), F32)
            cim_scr[:, cols] = jnp.zeros((nseg, S5_SCAN_LANES), F32)
            cre_scr[0:1, cols] = last_r
            cim_scr[0:1, cols] = last_i

            @pl.when(c == n_chunks - 1)
            def _():
                htre_ref[:, cols] = last_r
                htim_ref[:, cols] = last_i
        else:
            htre_ref[:, cols] = hr
            htim_ref[:, cols] = hi

    for kb in range(n_kb):
        hre = hr_scr[:, kb * ks:(kb + 1) * ks].astype(BF16)
        him = hi_scr[:, kb * ks:(kb + 1) * ks].astype(BF16)
        ykb = (jnp.dot(hre, cre_ref[kb], preferred_element_type=F32)
               + jnp.dot(him, cimn_ref[kb], preferred_element_type=F32))
        yc_scr[:, kb * ku:(kb + 1) * ku] = ykb

    yc = yc_scr[...]
    hi = yc.astype(BF16)
    r1 = yc - hi.astype(F32)
    mid = r1.astype(BF16)
    lo = (r1 - mid.astype(F32)).astype(BF16)
    unperm = unperm_ref[...]
    y = (jnp.dot(unperm, hi, preferred_element_type=F32) + jnp.dot(unperm, mid, preferred_element_type=F32)
         + jnp.dot(unperm, lo, preferred_element_type=F32))
    y_ref[...] = jax.nn.gelu(y + d_ref[...] * u_ref[...])


def _s5_discretise(a_re, a_im, log_dt, b_re, b_im, c_re, c_im, d_skip, seg_len):
    g, p = a_re.shape
    ch = b_re.shape[-1]
    lam = lax.complex(a_re.astype(F32), a_im.astype(F32))
    dt = jnp.exp(log_dt.astype(F32))[:, None]
    lam_bar = jnp.exp(lam * dt)
    b_bar = ((lam_bar - 1.0) / lam)[..., None] * lax.complex(b_re.astype(F32), b_im.astype(F32))
    lam_seg = lam_bar
    assert seg_len & (seg_len - 1) == 0
    for _ in range(seg_len.bit_length() - 1):
        lam_seg = lam_seg * lam_seg
    gl = S5_LANE_GROUPS
    n_kb = g // gl
    eye = jnp.eye(gl, dtype=F32)

    def pack_b(x):
        x = x.reshape(n_kb, gl, p, ch).transpose(0, 1, 3, 2)
        return jnp.einsum("kgcp,gh->kgchp", x, eye).reshape(n_kb, gl * ch, gl * p)

    def pack_c(x):
        x = x.reshape(n_kb, gl, ch, p).transpose(0, 1, 3, 2)
        return jnp.einsum("kgpc,gh->kgphc", x, eye).reshape(n_kb, gl * p, gl * ch)

    flat = lambda x: x.reshape(1, g * p)
    return dict(
        bre=pack_b(jnp.real(b_bar)).astype(BF16), bim=pack_b(jnp.imag(b_bar)).astype(BF16),
        cre=pack_c(c_re.astype(F32)).astype(BF16), cimn=pack_c(-c_im.astype(F32)).astype(BF16),
        lre=flat(jnp.real(lam_bar)), lim=flat(jnp.imag(lam_bar)),
        llre=flat(jnp.real(lam_seg)), llim=flat(jnp.imag(lam_seg)),
        d=d_skip.astype(F32).reshape(1, g * ch),
    )


def _s5(u, h0_re, h0_im, prm, seg_len, chained, m_out=None):
    n_rows, w = u.shape
    m = n_rows if m_out is None else m_out
    gp = prm["lre"].shape[1]
    chunk = SUBLANES * seg_len
    if chained:
        nseq = h0_re.shape[0]
        n_chunks = n_rows // nseq // chunk
        grid = (nseq, n_chunks)
        u_spec = pl.BlockSpec((chunk, w), lambda b, c: (b * n_chunks + c, 0))
        h_spec = pl.BlockSpec((None, 1, gp), lambda b, c: (b, 0, 0))
        h_shape = jax.ShapeDtypeStruct((nseq, 1, gp), F32)
    else:
        assert n_rows == chunk
        n_chunks = 1
        grid = (1, 1)
        u_spec = pl.BlockSpec((chunk, w), lambda b, c: (0, 0))
        h_spec = pl.BlockSpec((SUBLANES, gp), lambda b, c: (0, 0))
        h_shape = jax.ShapeDtypeStruct((SUBLANES, gp), F32)
    full = lambda a: pl.BlockSpec(a.shape, lambda b, c: (0,) * a.ndim)
    names = ("bre", "bim", "cre", "cimn", "lre", "lim", "llre", "llim", "d")
    r = jnp.arange(chunk)
    perm = (((r % SUBLANES) * seg_len + r // SUBLANES)[:, None] == r[None, :]).astype(BF16)
    consts = [perm, perm.T]
    return pl.pallas_call(
        functools.partial(_s5_kernel, seg_len=seg_len, chained=chained, n_chunks=n_chunks),
        grid=grid,
        in_specs=[u_spec] + [full(a) for a in consts] + [full(prm[n]) for n in names] + [h_spec, h_spec],
        out_specs=[u_spec, h_spec, h_spec],
        out_shape=[jax.ShapeDtypeStruct((m, w), F32), h_shape, h_shape],
        scratch_shapes=[pltpu.VMEM((chunk, gp), F32), pltpu.VMEM((chunk, gp), F32),
                        pltpu.VMEM((SUBLANES, gp), F32), pltpu.VMEM((SUBLANES, gp), F32),
                        pltpu.VMEM((chunk, w), F32)],
        compiler_params=_cparams(("parallel", "arbitrary")),
        name="s5_scan",
    )(u, *consts, *[prm[n] for n in names], h0_re, h0_im)


def _pick_tile(n, target):
    t = min(n, target)
    while n % t:
        t //= 2
    return t


def _row_tile(m):
    return min(range(LANES, MAX_ROW_TILE + 1, LANES), key=lambda t: (-(-m // t) * t, -t))


def _fill_rest(full, sample_rows, mp):
    pad = full.shape[0] - mp - sample_rows.shape[0]
    rest = jnp.concatenate([sample_rows.astype(full.dtype), jnp.zeros((pad, full.shape[1]), full.dtype)], axis=0)
    return lax.dynamic_update_slice(full, rest, (mp, 0))


def _run_trunk(x_prompt, x_sample, prm, s5_prompt, s5_sample, kv, s5_re0, s5_im0, conv0_prompt, conv0_sample):
    bp, tp, d = x_prompt.shape
    bs, ts, _ = x_sample.shape
    mp, ms = bp * tp, bs * ts
    tm = _row_tile(mp + ms)
    m_all = -(-(mp + ms) // tm) * tm
    depth = prm["ffn1_wgu"].shape[0]
    alpha = (2.0 * depth) ** 0.25
    nh, hd = kv["nh"], kv["hd"]
    moba_w = nh * hd
    g_cnt, p_cnt = prm["s5_a_re"].shape[1:]
    gp = g_cnt * p_cnt
    tf = _pick_tile(prm["ffn1_wd"].shape[1], 512)
    sample = slice(mp, mp + ms)

    x32 = jnp.concatenate([x_prompt.reshape(mp, d).astype(F32), x_sample.reshape(ms, d).astype(F32),
                           jnp.zeros((m_all - mp - ms, d), F32)], axis=0)
    xb = x32.astype(BF16)
    row = lambda a: a.reshape(1, -1).astype(F32)
    out = dict(hrp=[], hip=[], cvp=[], ks=[], vs=[], hrs=[], his=[], cvs=[])
    k_stack = v_stack = None
    wgu_b, wd_b = _cast_layer(prm["ffn1_wgu"], 0), _cast_layer(prm["ffn1_wd"], 0)
    for layer in range(depth):
        i = layer // 2
        g, b = prm["ln_g"][layer], prm["ln_b"][layer]
        x32, xb, wgu_b, wd_b = _ffn(x32, xb, wgu_b, wd_b, 0, row(g[0]), row(b[0]), alpha, tm, tf,
                                    convert_next=(prm["ffn2_wgu"], prm["ffn2_wd"], layer))
        if layer % 2 == 0:
            assert d - moba_w == moba_w
            (q, q_r), (k_stack, k_rest), (v_stack, v_rest), (u, u_r) = _hyb_in(xb, prm["hyb_w_in"], i, mp, k_stack,
                                                                               v_stack)
            attn = _moba_prompt(q, k_stack, v_stack, i, bp, tp, nh, hd, m_all)
            attn_s = _sample_attn(q_r[:ms], k_rest[:ms], v_rest[:ms], kv["cache_k"], kv["cache_v"], i,
                                  kv["page_table"], bs, ts, nh, hd)
            zero = jnp.zeros((bp, 1, gp), F32)
            y, hr_p, hi_p = _s5(u, zero, zero, s5_prompt[i], s5_prompt[i]["seg_len"], True, m_out=m_all)
            y_s, hr_s, hi_s = _s5(u_r[:ms], s5_re0[i].reshape(bs, gp), s5_im0[i].reshape(bs, gp), s5_sample[i],
                                  s5_sample[i]["seg_len"], False)
            x32, xb = _mm_res_ln([_fill_rest(attn, attn_s, mp), _fill_rest(y, y_s, mp)], prm["hyb_w_out"], i, x32,
                                 row(g[1]), row(b[1]), alpha, tm, glu=(prm["s5_glu_w"], row(prm["s5_glu_b"][i])))
            out["ks"].append(k_rest[:ms].reshape(bs, ts, nh, hd))
            out["vs"].append(v_rest[:ms].reshape(bs, ts, nh, hd))
            out["hrp"].append(hr_p.reshape(bp, g_cnt, p_cnt))
            out["hip"].append(hi_p.reshape(bp, g_cnt, p_cnt))
            out["hrs"].append(hr_s.reshape(bs, g_cnt, p_cnt))
            out["his"].append(hi_s.reshape(bs, g_cnt, p_cnt))
        else:
            c, cv_p = _conv_in_core(xb, prm["conv_w_in"], conv0_prompt[i], prm["conv_w"], i, bp, tp,
                                    _pick_tile(tp, 512), _pick_tile(d, 1024))
            z = _mm(xb[sample], prm["conv_w_in"], i, 0, 3 * d, ms, _pick_tile(d, 2048))
            c_s, cv_s = _conv_core(z, conv0_sample[i], prm["conv_w"], i, bs, ts, ts, _pick_tile(d, 512))
            x32, xb = _mm_res_ln([_fill_rest(c, c_s, mp)], prm["conv_w_out"], i, x32, row(g[1]), row(b[1]),
                                 alpha, tm)
            out["cvp"].append(cv_p)
            out["cvs"].append(cv_s)
        if layer < depth - 1:
            x32, xb, wgu_b, wd_b = _ffn(x32, xb, wgu_b, wd_b, 0, row(g[2]), row(b[2]), alpha, tm, tf,
                                        convert_next=(prm["ffn1_wgu"], prm["ffn1_wd"], layer + 1))
        else:
            y_prompt, y_rest = _ffn(x32, xb, wgu_b, wd_b, 0, row(g[2]), row(b[2]), alpha, tm, tf, mp=mp)
    out = {n: jnp.stack(a) for n, a in out.items()}
    n_hyb = k_stack.shape[0]
    out["kp"] = k_stack.reshape(n_hyb, bp, tp, nh, hd)
    out["vp"] = v_stack.reshape(n_hyb, bp, tp, nh, hd)
    return y_prompt.reshape(bp, tp, d), y_rest[:ms].reshape(bs, ts, d), out


def kernel(x_prompt, x_sample, cache_k, cache_v, state_s5_re, state_s5_im, state_conv, page_table, ffn1_wgu, ffn1_wd, ffn2_wgu, ffn2_wd, ln_g, ln_b, hyb_w_in, hyb_w_out, s5_a_re, s5_a_im, s5_log_dt, s5_b_re, s5_b_im, s5_c_re, s5_c_im, s5_d, s5_glu_w, s5_glu_b, conv_w_in, conv_w, conv_w_out):
    n_hyb, _, _, nh, hd = cache_k.shape
    bp, t_prompt, d = x_prompt.shape
    db, t_sample, _ = x_sample.shape
    n_conv, cw = conv_w.shape[:2]
    g_cnt, p_cnt = s5_a_re.shape[1:]
    assert db == SUBLANES, "the sample scan lays the decode batch along the vreg sublanes"
    assert (s5_b_re.shape[-1] * S5_LANE_GROUPS) == 128

    prm = dict(
        ffn1_wgu=ffn1_wgu, ffn1_wd=ffn1_wd, ffn2_wgu=ffn2_wgu, ffn2_wd=ffn2_wd,
        ln_g=ln_g, ln_b=ln_b, hyb_w_in=hyb_w_in.astype(BF16), hyb_w_out=hyb_w_out.astype(BF16),
        s5_a_re=s5_a_re, s5_glu_w=s5_glu_w.astype(BF16), s5_glu_b=s5_glu_b,
        conv_w_in=conv_w_in.astype(BF16), conv_w=conv_w.astype(F32), conv_w_out=conv_w_out.astype(BF16),
    )

    def s5_prms(seg_len):
        out = []
        for i in range(n_hyb):
            sp = _s5_discretise(s5_a_re[i], s5_a_im[i], s5_log_dt[i], s5_b_re[i], s5_b_im[i], s5_c_re[i],
                                s5_c_im[i], s5_d[i], seg_len)
            sp["seg_len"] = seg_len
            out.append(sp)
        return out

    prompt_seg = _pick_tile(t_prompt // SUBLANES, 32)
    kv = dict(nh=nh, hd=hd, cache_k=cache_k, cache_v=cache_v, page_table=page_table)
    conv_zero = jnp.zeros((n_conv, bp, cw - 1, d), x_prompt.dtype)
    y_p, y_s, o = _run_trunk(x_prompt, x_sample, prm, s5_prms(prompt_seg), s5_prms(t_sample), kv, state_s5_re,
                             state_s5_im, conv_zero, state_conv)
    return (y_p, y_s, o["kp"], o["vp"], o["hrp"], o["hip"], o["cvp"], o["ks"], o["vs"], o["hrs"], o["his"], o["cvs"])
```
